```python
import math
import jax
import jax.numpy as jnp
from jax import lax
import numpy as np

D_MODEL = 1024
BATCH = 4
SEQ = 4096
DEPTH = 2
DEC_BATCH = 32
DEC_SEQ = 8
PAST_LEN = 16384
PAGE_SIZE = 128

SSM_EXPAND = 2
D_INNER = SSM_EXPAND * D_MODEL
SSM_HEAD_DIM = 64
SSM_HEADS = D_INNER // SSM_HEAD_DIM
SSM_GROUPS = 4
SSM_STATE = 128
CONV_W = 4
CONV_CH = D_INNER + 2 * SSM_GROUPS * SSM_STATE
SSD_CHUNK = 128
RMS_EPS = 1e-5
ATTN_HEAD_DIM = 64
ATTN_HEADS = D_MODEL // ATTN_HEAD_DIM
KV_HEADS = ATTN_HEADS // 2
ATTN_WIDTH = ATTN_HEADS * ATTN_HEAD_DIM
KV_WIDTH = KV_HEADS * ATTN_HEAD_DIM
MOBA_BLOCK = 256
MOBA_TOPK = 3
Q_BLOCK = 64
N_EXPERTS = 32
TOP_K = 4
D_FF = D_MODEL
SWIGLU_LIMIT = 7.0
SWIGLU_ALPHA = 1.702
DN_ALPHA = (2.0 * DEPTH) ** 0.25
DN_BETA = (8.0 * DEPTH) ** -0.25
LN_EPS = 1e-5
SPLITS = (D_INNER, CONV_CH, SSM_HEADS, ATTN_WIDTH, KV_WIDTH, KV_WIDTH, D_MODEL, D_MODEL)
N_IN = sum(SPLITS)

kernel_name = 'hybrid_ssd_moba_moe_decode_step'


def _split_points():
    pts, acc = [], 0
    for s in SPLITS[:-1]:
        acc += s
        pts.append(acc)
    return pts


def _layer_norm(x, g, b):
    xf = x.astype(jnp.float32)
    mu = jnp.mean(xf, axis=-1, keepdims=True)
    var = jnp.mean(jnp.square(xf - mu), axis=-1, keepdims=True)
    return ((xf - mu) * lax.rsqrt(var + LN_EPS) * g + b).astype(x.dtype)


def _causal_conv(xbc, conv_prev, w, b):
    t = xbc.shape[1]
    xp = jnp.concatenate([conv_prev.astype(xbc.dtype), xbc], axis=1)
    y = b + xp[:, 0:t] * w[0]
    for j in range(1, CONV_W):
        y = y + xp[:, j:j + t] * w[j]
    return jax.nn.silu(y), xp[:, t:]


def _ssd_scan(x, dt, a, bm, cm, h0):
    bsz, t = x.shape[:2]
    ln = math.gcd(t, SSD_CHUNK)
    nc = t // ln
    g, hg = SSM_GROUPS, SSM_HEADS // SSM_GROUPS
    xc = x.reshape(bsz, nc, ln, g, hg, SSM_HEAD_DIM)
    dtc = dt.reshape(bsz, nc, ln, g, hg)
    bc = bm.reshape(bsz, nc, ln, g, SSM_STATE)
    cc = cm.reshape(bsz, nc, ln, g, SSM_STATE)
    la = jnp.cumsum(dtc * a.reshape(g, hg), axis=2)
    lat = jnp.moveaxis(la, 2, -1)
    seg = lat[..., :, None] - lat[..., None, :]
    causal = jnp.tril(jnp.ones((ln, ln), dtype=bool))
    decay = jnp.exp(jnp.where(causal, seg, -jnp.inf))
    cb = jnp.einsum('bclgn,bcsgn->bcgls', cc, bc)
    xdt = xc * dtc[..., None]
    y_diag = jnp.einsum('bcghls,bcsghp->bclghp', cb[:, :, :, None] * decay, xdt)
    to_end = jnp.exp(la[:, :, -1:] - la)
    states = jnp.einsum('bclgn,bclghp->bcghpn', bc, xdt * to_end[..., None])
    chunk_decay = jnp.exp(la[:, :, -1])

    def step(h, inp):
        s, d = inp
        return h * d[..., None, None] + s, h

    h_init = h0.reshape(bsz, g, hg, SSM_HEAD_DIM, SSM_STATE)
    h_last, h_in = lax.scan(step, h_init, (jnp.moveaxis(states, 1, 0), jnp.moveaxis(chunk_decay, 1, 0)))
    h_in = jnp.moveaxis(h_in, 0, 1)
    y_off = jnp.einsum('bclgn,bcghpn->bclghp', cc, h_in) * jnp.exp(la)[..., None]
    y = (y_diag + y_off).reshape(bsz, t, SSM_HEADS, SSM_HEAD_DIM)
    return y, h_last.reshape(bsz, SSM_HEADS, SSM_HEAD_DIM, SSM_STATE)


def _ssd_branch(z, xbc, dt_raw, conv_prev, h0, conv_w, conv_b, dt_bias, a_log, d_skip, norm_g):
    bsz, t = z.shape[:2]
    xbc, conv_new = _causal_conv(xbc, conv_prev, conv_w, conv_b)
    xs, bm, cm = jnp.split(xbc.astype(jnp.float32), [D_INNER, D_INNER + SSM_GROUPS * SSM_STATE], axis=-1)
    xh = xs.reshape(bsz, t, SSM_HEADS, SSM_HEAD_DIM)
    dt = jax.nn.softplus(dt_raw.astype(jnp.float32) + dt_bias.astype(jnp.float32))
    a = -jnp.exp(a_log.astype(jnp.float32))
    y, h_new = _ssd_scan(xh, dt, a,
                         bm.reshape(bsz, t, SSM_GROUPS, SSM_STATE),
                         cm.reshape(bsz, t, SSM_GROUPS, SSM_STATE),
                         h0.astype(jnp.float32))
    y = y + d_skip.astype(jnp.float32)[:, None] * xh
    y = y.reshape(bsz, t, D_INNER) * jax.nn.silu(z.astype(jnp.float32))
    yg = y.reshape(bsz, t, SSM_GROUPS, D_INNER // SSM_GROUPS)
    yg = yg * lax.rsqrt(jnp.mean(yg * yg, axis=-1, keepdims=True) + RMS_EPS)
    y = yg.reshape(bsz, t, D_INNER) * norm_g
    return y.astype(z.dtype), h_new.astype(h0.dtype), conv_new


def _alibi_slopes():
    return 2.0 ** (-8.0 * (jnp.arange(ATTN_HEADS, dtype=jnp.float32) + 1.0) / ATTN_HEADS)


def _moba_attention(q, k, v, q_start):
    bsz, tq = q.shape[:2]
    ln = k.shape[1]
    grp = ATTN_HEADS // KV_HEADS
    nb = -(-ln // MOBA_BLOCK)
    pad = nb * MOBA_BLOCK - ln
    if pad:
        k = jnp.pad(k, ((0, 0), (0, pad), (0, 0), (0, 0)))
        v = jnp.pad(v, ((0, 0), (0, pad), (0, 0), (0, 0)))
    kb = k.reshape(bsz, nb, MOBA_BLOCK, KV_HEADS, ATTN_HEAD_DIM)
    vb = v.reshape(bsz, nb, MOBA_BLOCK, KV_HEADS, ATTN_HEAD_DIM)
    k_mean = jnp.mean(kb.astype(jnp.float32), axis=2)
    n_sel = min(MOBA_TOPK, nb)
    nsk = n_sel * MOBA_BLOCK
    qb = math.gcd(tq, Q_BLOCK)
    nq = tq // qb
    slopes = _alibi_slopes().reshape(KV_HEADS, grp)
    scale = ATTN_HEAD_DIM ** -0.5
    bi = jnp.arange(bsz)[:, None, None, None, None]
    hi = jnp.arange(KV_HEADS)[None, None, :, None, None]
    offs = jnp.arange(MOBA_BLOCK)

    def one_block(args):
        qc, i = args
        start = q_start + i * qb
        pos = start + jnp.arange(qb)
        own = start // MOBA_BLOCK
        qg = qc.reshape(bsz, qb, KV_HEADS, grp, ATTN_HEAD_DIM)
        gate = jnp.einsum('btkgd,bnkd->btkgn', qg.astype(jnp.float32), k_mean)
        gate = jnp.where(jnp.arange(nb) < own, gate, -jnp.inf)
        _, idx = lax.top_k(gate, n_sel)
        valid = jnp.arange(n_sel) < own
        k_sel = kb[bi, idx, :, hi]
        v_sel = vb[bi, idx, :, hi].reshape(bsz, qb, KV_HEADS, grp, nsk, ATTN_HEAD_DIM)
        s_sel = jnp.einsum('btkgd,btkgsjd->btkgsj', qg, k_sel).astype(jnp.float32) * scale
        kpos = idx[..., None] * MOBA_BLOCK + offs
        dist = (pos[None, :, None, None, None, None] - kpos).astype(jnp.float32)
        s_sel = s_sel - slopes[:, :, None, None] * dist
        s_sel = jnp.where(valid[:, None], s_sel, -jnp.inf).reshape(bsz, qb, KV_HEADS, grp, nsk)
        k_own = lax.dynamic_index_in_dim(kb, own, axis=1, keepdims=False)
        v_own = lax.dynamic_index_in_dim(vb, own, axis=1, keepdims=False)
        s_own = jnp.einsum('btkgd,bjkd->btkgj', qg, k_own).astype(jnp.float32) * scale
        dist_own = pos[:, None] - (own * MOBA_BLOCK + offs)[None, :]
        dist_own = dist_own[:, None, None, :]
        s_own = s_own - slopes[:, :, None] * dist_own.astype(jnp.float32)
        s_own = jnp.where(dist_own >= 0, s_own, -jnp.inf)
        p = jax.nn.softmax(jnp.concatenate([s_sel, s_own], axis=-1), axis=-1).astype(v.dtype)
        o = (jnp.einsum('btkgs,btkgsd->btkgd', p[..., :nsk], v_sel)
             + jnp.einsum('btkgj,bjkd->btkgd', p[..., nsk:], v_own))
        return o.reshape(bsz, qb, ATTN_HEADS, ATTN_HEAD_DIM)

    qs = jnp.moveaxis(q.reshape(bsz, nq, qb, ATTN_HEADS, ATTN_HEAD_DIM), 1, 0)
    out = lax.map(one_block, (qs, jnp.arange(nq)))
    return jnp.moveaxis(out, 0, 1).reshape(bsz, tq, ATTN_WIDTH)


def _moe(h, w_r, b_r, w_gu, b_gu, w_dn, b_dn):
    shp = h.shape
    hf = h.reshape(-1, shp[-1])
    logits = (hf @ w_r + b_r).astype(jnp.float32)
    top_v, top_i = lax.top_k(logits, TOP_K)
    gates = jnp.sum(jax.nn.one_hot(top_i, N_EXPERTS, dtype=jnp.float32)
                    * jax.nn.softmax(top_v, axis=-1)[..., None], axis=1)
    out = jnp.zeros(hf.shape, jnp.float32)
    for e in range(N_EXPERTS):
        g, u = jnp.split(hf @ w_gu[e] + b_gu[e], 2, axis=-1)
        g = jnp.minimum(g, SWIGLU_LIMIT)
        u = jnp.clip(u, -SWIGLU_LIMIT, SWIGLU_LIMIT)
        act = (u + 1.0) * (g * jax.nn.sigmoid(SWIGLU_ALPHA * g))
        out = out + gates[:, e:e + 1] * (act @ w_dn[e] + b_dn[e])
    return out.reshape(shp).astype(h.dtype)


def setup_inputs(seed: int = 0) -> dict:
    key = jax.random.key(seed)
    keys = list(jax.random.split(key, 40))

    def nrm(shape, s):
        return jax.random.normal(keys.pop(), shape, jnp.float32) * s

    n_pages = PAST_LEN // PAGE_SIZE
    n_used = DEC_BATCH * n_pages
    n_pool = n_used + max(1, n_used // 4)
    perm = jax.random.permutation(keys.pop(), n_pool)
    page_table = perm[:n_used].reshape(DEC_BATCH, n_pages).astype(jnp.int32)

    d = D_MODEL
    x_prompt = nrm((BATCH, SEQ, d), 1.0)
    x_sample = nrm((DEC_BATCH, DEC_SEQ, d), 1.0)
    c_prompt = nrm((BATCH, d), 1.0)
    c_sample = nrm((DEC_BATCH, d), 1.0)
    cache_k = nrm((DEPTH, n_pool, PAGE_SIZE, KV_HEADS, ATTN_HEAD_DIM), 1.0)
    cache_v = nrm((DEPTH, n_pool, PAGE_SIZE, KV_HEADS, ATTN_HEAD_DIM), 1.0)
    state_ssm = nrm((DEPTH, DEC_BATCH, SSM_HEADS, SSM_HEAD_DIM, SSM_STATE), 0.5)
    state_conv = nrm((DEPTH, DEC_BATCH, CONV_W - 1, CONV_CH), 1.0)

    w_ada = nrm((DEPTH, d, 6 * d), 0.5 * d ** -0.5)
    b_ada = nrm((DEPTH, 6 * d), 0.02)
    col_scale = jnp.concatenate([jnp.ones((sum(SPLITS[:5]),), jnp.float32),
                                 jnp.full((SPLITS[5],), DN_BETA, jnp.float32),
                                 jnp.ones((sum(SPLITS[6:]),), jnp.float32)])
    w_in = nrm((DEPTH, d, N_IN), d ** -0.5) * col_scale
    conv_w = nrm((DEPTH, CONV_W, CONV_CH), CONV_W ** -0.5)
    conv_b = nrm((DEPTH, CONV_CH), 0.02)
    dt0 = jnp.exp(jax.random.uniform(keys.pop(), (DEPTH, SSM_HEADS), jnp.float32,
                                     math.log(1e-3), math.log(1e-1)))
    dt_bias = dt0 + jnp.log(-jnp.expm1(-dt0))
    a_log = jnp.log(jax.random.uniform(keys.pop(), (DEPTH, SSM_HEADS), jnp.float32, 1.0, 16.0))
    d_skip = 1.0 + nrm((DEPTH, SSM_HEADS), 0.05)
    ssm_norm_g = 1.0 + nrm((DEPTH, D_INNER), 0.05)
    w_o_ssm = nrm((DEPTH, D_INNER, d), DN_BETA * D_INNER ** -0.5)
    w_o_attn = nrm((DEPTH, ATTN_WIDTH, d), DN_BETA * ATTN_WIDTH ** -0.5)
    w_out = nrm((DEPTH, d, d), DN_BETA * d ** -0.5)
    ln1_g = 1.0 + nrm((DEPTH, d), 0.05)
    ln1_b = nrm((DEPTH, d), 0.02)
    ln2_g = 1.0 + nrm((DEPTH, d), 0.05)
    ln2_b = nrm((DEPTH, d), 0.02)
    w_router = nrm((DEPTH, d, N_EXPERTS), d ** -0.5)
    b_router = nrm((DEPTH, N_EXPERTS), 0.01)
    w_gu = nrm((DEPTH, N_EXPERTS, d, 2 * D_FF), d ** -0.5)
    b_gu = nrm((DEPTH, N_EXPERTS, 2 * D_FF), 0.02)
    w_down = nrm((DEPTH, N_EXPERTS, D_FF, d), DN_BETA * D_FF ** -0.5)
    b_down = nrm((DEPTH, N_EXPERTS, d), 0.02)
    return {'x_prompt': x_prompt, 'x_sample': x_sample, 'c_prompt': c_prompt, 'c_sample': c_sample,
            'cache_k': cache_k, 'cache_v': cache_v, 'state_ssm': state_ssm, 'state_conv': state_conv,
            'page_table': page_table, 'w_ada': w_ada, 'b_ada': b_ada, 'w_in': w_in,
            'conv_w': conv_w, 'conv_b': conv_b, 'dt_bias': dt_bias, 'a_log': a_log, 'd_skip': d_skip,
            'ssm_norm_g': ssm_norm_g, 'w_o_ssm': w_o_ssm, 'w_o_attn': w_o_attn, 'w_out': w_out,
            'ln1_g': ln1_g, 'ln1_b': ln1_b, 'ln2_g': ln2_g, 'ln2_b': ln2_b,
            'w_router': w_router, 'b_router': b_router, 'w_gu': w_gu, 'b_gu': b_gu,
            'w_down': w_down, 'b_down': b_down}


def reference(x_prompt, x_sample, c_prompt, c_sample, cache_k, cache_v, state_ssm, state_conv, page_table,
              w_ada, b_ada, w_in, conv_w, conv_b, dt_bias, a_log, d_skip, ssm_norm_g,
              w_o_ssm, w_o_attn, w_out, ln1_g, ln1_b, ln2_g, ln2_b,
              w_router, b_router, w_gu, b_gu, w_down, b_down):
    past_len = page_table.shape[1] * cache_k.shape[2]
    split_points = _split_points()

    def layer(x, c, l, attend, h0, conv_prev):
        bsz, t = x.shape[:2]
        mod = (jax.nn.silu(c) @ w_ada[l] + b_ada[l])[:, None, :]
        sh_m, sc_m, g_m, sh_f, sc_f, g_f = jnp.split(mod, 6, axis=-1)
        h = x * (1.0 + sc_m) + sh_m
        z, xbc, dt_raw, q, k, v, gs, ga = jnp.split(h @ w_in[l], split_points, axis=-1)
        y_s, h_new, conv_new = _ssd_branch(z, xbc, dt_raw, conv_prev, h0, conv_w[l], conv_b[l],
                                           dt_bias[l], a_log[l], d_skip[l], ssm_norm_g[l])
        q = q.reshape(bsz, t, ATTN_HEADS, ATTN_HEAD_DIM)
        k = k.reshape(bsz, t, KV_HEADS, ATTN_HEAD_DIM)
        v = v.reshape(bsz, t, KV_HEADS, ATTN_HEAD_DIM)
        y_a = attend(q, k, v, l)
        merged = jax.nn.sigmoid(gs) * (y_s @ w_o_ssm[l]) + jax.nn.sigmoid(ga) * (y_a @ w_o_attn[l])
        x = _layer_norm(DN_ALPHA * x + g_m * (merged @ w_out[l]), ln1_g[l], ln1_b[l])
        hf = x * (1.0 + sc_f) + sh_f
        ffn = _moe(hf, w_router[l], b_router[l], w_gu[l], b_gu[l], w_down[l], b_down[l])
        x = _layer_norm(DN_ALPHA * x + g_f * ffn, ln2_g[l], ln2_b[l])
        return x, k, v, h_new, conv_new

    def attend_prompt(q, k, v, l):
        return _moba_attention(q, k, v, 0)

    def attend_sample(q, k, v, l):
        nbat = q.shape[0]
        pk = cache_k[l, page_table].reshape(nbat, past_len, KV_HEADS, ATTN_HEAD_DIM)
        pv = cache_v[l, page_table].reshape(nbat, past_len, KV_HEADS, ATTN_HEAD_DIM)
        k_all = jnp.concatenate([pk, k.astype(pk.dtype)], axis=1)
        v_all = jnp.concatenate([pv, v.astype(pv.dtype)], axis=1)
        return _moba_attention(q.astype(pk.dtype), k_all, v_all, past_len).astype(q.dtype)

    h0_p = jnp.zeros((x_prompt.shape[0], SSM_HEADS, SSM_HEAD_DIM, SSM_STATE), x_prompt.dtype)
    conv0_p = jnp.zeros((x_prompt.shape[0], CONV_W - 1, CONV_CH), x_prompt.dtype)
    yp, ys = x_prompt, x_sample
    kp, vp, hp, cvp, ksm, vsm, hsm, cvs = [], [], [], [], [], [], [], []
    for l in range(DEPTH):
        yp, k_new, v_new, h_new, cv_new = layer(yp, c_prompt, l, attend_prompt, h0_p, conv0_p)
        kp.append(k_new)
        vp.append(v_new)
        hp.append(h_new)
        cvp.append(cv_new)
        ys, k_new, v_new, h_new, cv_new = layer(ys, c_sample, l, attend_sample, state_ssm[l], state_conv[l])
        ksm.append(k_new)
        vsm.append(v_new)
        hsm.append(h_new)
        cvs.append(cv_new)
    return (yp, ys, jnp.stack(kp), jnp.stack(vp), jnp.stack(hp), jnp.stack(cvp),
            jnp.stack(ksm), jnp.stack(vsm), jnp.stack(hsm), jnp.stack(cvs))
```

```python
import functools
import math

import jax
import jax.numpy as jnp
from jax import lax
from jax.experimental import pallas as pl
from jax.experimental.pallas import tpu as pltpu

F32 = jnp.float32
BF16 = jnp.bfloat16
NEG_INF = float("-inf")

LANES = 128
SUBLANES = 8
VMEM_LIMIT = 56 * 1024 * 1024

SSM_HEAD_DIM = 64
SSM_GROUPS = 4
SSM_STATE = 128
CONV_W = 4
SSD_CHUNK = 128
RMS_EPS = 1e-5
ATTN_HEAD_DIM = 64
MOBA_BLOCK = 256
MOBA_TOPK = 3
TOP_K = 4
SWIGLU_LIMIT = 7.0
SWIGLU_ALPHA = 1.702
LN_EPS = 1e-5


def _cparams(n_axes):
    return pltpu.CompilerParams(dimension_semantics=("arbitrary",) * n_axes,
                                vmem_limit_bytes=VMEM_LIMIT)


def _dot(a, b):
    return jnp.dot(a, b, preferred_element_type=F32)


def _dot_nt(a, b):
    return lax.dot_general(a, b, (((1,), (1,)), ((), ())), preferred_element_type=F32)


def _split2(a):
    hi = a.astype(BF16)
    lo = (a - hi.astype(F32)).astype(BF16)
    return hi, lo


def _split3(a):
    hi = a.astype(BF16)
    r = a - hi.astype(F32)
    mid = r.astype(BF16)
    lo = (r - mid.astype(F32)).astype(BF16)
    return hi, mid, lo


def _dot_nt_precise(a, b):
    ah, al = _split2(a)
    bh, bl = _split2(b)
    return _dot_nt(ah, bh) + (_dot_nt(ah, bl) + _dot_nt(al, bh))


def _dot_precise(a, b):
    ah, al = _split2(a)
    bh, bl = _split2(b)
    return _dot(ah, bh) + (_dot(ah, bl) + _dot(al, bh))


def _silu(x):
    return x * jax.nn.sigmoid(x)


def _softplus(x):
    return jnp.maximum(x, 0.0) + jnp.log1p(jnp.exp(-jnp.abs(x)))


def _layer_norm(x, g, b):
    mu = jnp.mean(x, axis=-1, keepdims=True)
    xc = x - mu
    var = jnp.mean(xc * xc, axis=-1, keepdims=True)
    return xc * lax.rsqrt(var + LN_EPS) * g + b


def _topk_mask(score, lane, k):
    width = score.shape[-1]
    sel = jnp.zeros(score.shape, dtype=jnp.bool_)
    vals = []
    g = score
    for _ in range(k):
        m = jnp.max(g, axis=-1, keepdims=True)
        idx = jnp.min(jnp.where(g == m, lane, width), axis=-1, keepdims=True)
        pick = jnp.logical_and(lane == idx, m > NEG_INF)
        sel = jnp.logical_or(sel, pick)
        g = jnp.where(pick, NEG_INF, g)
        vals.append(m)
    return sel, vals


def _ada_kernel(c_ref, w_ref, b_ref, o_ref):
    s = _silu(c_ref[...]).astype(BF16)
    o_ref[0] = _dot(s, w_ref[0].astype(BF16)) + b_ref[0]


def _ada(c_all, w_ada, b_ada):
    depth, d, n6 = w_ada.shape
    rows = c_all.shape[0]
    tn = 1024
    return pl.pallas_call(
        _ada_kernel,
        out_shape=jax.ShapeDtypeStruct((depth, rows, n6), F32),
        grid=(depth, n6 // tn),
        in_specs=[pl.BlockSpec((rows, d), lambda l, j: (0, 0)),
                  pl.BlockSpec((1, d, tn), lambda l, j: (l, 0, j)),
                  pl.BlockSpec((1, 1, tn), lambda l, j: (l, 0, j))],
        out_specs=pl.BlockSpec((1, rows, tn), lambda l, j: (l, 0, j)),
        compiler_params=_cparams(2),
        name="ada",
    )(c_all, w_ada, b_ada.reshape(depth, 1, n6))


def _inproj_kernel(x_ref, sc_ref, sh_ref, w_ref, wdt_ref, wdtT_ref, o_ref, dt_ref, dtT_ref, h_scr):
    @pl.when(pl.program_id(1) == 0)
    def _():
        hb = (x_ref[...] * sc_ref[0] + sh_ref[0]).astype(BF16)
        h_scr[...] = hb
        dt_ref[...] = _dot(hb, wdt_ref[...])
        dtT_ref[...] = _dot_nt(wdtT_ref[...], hb)

    o_ref[...] = _dot(h_scr[...], w_ref[...])


def _inproj(x, sc1, sh, w_main, w_dt, w_dtT, tm, tiles_per_group):
    n, d = x.shape
    n_main = w_main.shape[1]
    tn = 1024
    r = sc1.shape[1]
    nh = w_dtT.shape[0]
    grp = lambda i, j: (i // tiles_per_group, 0, 0)
    return pl.pallas_call(
        _inproj_kernel,
        out_shape=(jax.ShapeDtypeStruct((n, n_main), F32),
                   jax.ShapeDtypeStruct((n, LANES), F32),
                   jax.ShapeDtypeStruct((nh, n), F32)),
        grid=(n // tm, n_main // tn),
        in_specs=[pl.BlockSpec((tm, d), lambda i, j: (i, 0)),
                  pl.BlockSpec((1, r, d), grp),
                  pl.BlockSpec((1, r, d), grp),
                  pl.BlockSpec((d, tn), lambda i, j: (0, j)),
                  pl.BlockSpec((d, LANES), lambda i, j: (0, 0)),
                  pl.BlockSpec((nh, d), lambda i, j: (0, 0))],
        out_specs=(pl.BlockSpec((tm, tn), lambda i, j: (i, j)),
                   pl.BlockSpec((tm, LANES), lambda i, j: (i, 0)),
                   pl.BlockSpec((nh, tm), lambda i, j: (0, i))),
        scratch_shapes=[pltpu.VMEM((tm, d), BF16)],
        compiler_params=_cparams(2),
        name="inproj",
    )(x, sc1, sh, w_main, w_dt, w_dtT)


def _ssd_kernel(lr, d_inner, xbc_ref, z_ref, dt_ref, dtT_ref, h0_ref, cprev_ref, cw_ref, cb_ref,
                dtb_ref, dtbT_ref, a_ref, aT_ref, dskip_ref, ng_ref,
                y_ref, hout_ref, xp_scr, h_scr, y_scr, dt_scr, dtT_scr):
    L = SSD_CHUNK
    gs = SSM_GROUPS * SSM_STATE
    n_heads = d_inner // SSM_HEAD_DIM
    pairs_per_group = n_heads // SSM_GROUPS // 2
    c = pl.program_id(1)

    @pl.when(c == 0)
    def _():
        xp_scr[0:SUBLANES, :] = cprev_ref[0]
        h_scr[...] = h0_ref[0]

    if lr < L:
        xp_scr[SUBLANES + lr:SUBLANES + L, :] = jnp.zeros((L - lr, xp_scr.shape[1]), F32)
    xp_scr[SUBLANES:SUBLANES + lr, :] = xbc_ref[...]

    acc = cb_ref[...] + cw_ref[0:1, :] * xp_scr[SUBLANES - 3:SUBLANES - 3 + L, :]
    for j in range(1, CONV_W):
        acc = acc + cw_ref[j:j + 1, :] * xp_scr[SUBLANES - 3 + j:SUBLANES - 3 + j + L, :]
    tail = xp_scr[lr:lr + SUBLANES, :]
    xp_scr[0:SUBLANES, :] = tail
    xc = _silu(acc)

    if lr < L:
        dt_scr[...] = jnp.zeros(dt_scr.shape, F32)
        dt_scr[0:lr, :] = dt_ref[...]
        dtT_scr[...] = jnp.zeros(dtT_scr.shape, F32)
        dtT_scr[:, 0:lr] = dtT_ref[0]
        dt_raw = dt_scr[...]
        dtT_raw = dtT_scr[...]
    else:
        dt_raw = dt_ref[...]
        dtT_raw = dtT_ref[0]
    row = lax.broadcasted_iota(jnp.int32, (L, LANES), 0)
    col = lax.broadcasted_iota(jnp.int32, (n_heads, L), 1)
    dtv = jnp.where(row < lr, _softplus(dt_raw + dtb_ref[...]), 0.0)
    dtvT = jnp.where(col < lr, _softplus(dtT_raw + dtbT_ref[...]), 0.0)
    dA = dtv * a_ref[...]
    dAT = dtvT * aT_ref[...]

    ri = lax.broadcasted_iota(jnp.int32, (L, L), 0)
    ci = lax.broadcasted_iota(jnp.int32, (L, L), 1)
    causal = ri >= ci
    tri = jnp.where(causal, 1.0, 0.0).astype(BF16)
    triT = jnp.where(ri <= ci, 1.0, 0.0).astype(BF16)
    d1, d2, d3 = _split3(dA)
    la = _dot(tri, d1) + (_dot(tri, d2) + _dot(tri, d3))
    e1, e2, e3 = _split3(dAT)
    laT = _dot(e1, triT) + (_dot(e2, triT) + _dot(e3, triT))

    la_last = la[L - 1:L, :]
    to_end = jnp.exp(la_last - la)
    ela = jnp.exp(la)
    cdec = jnp.exp(la_last)

    lane = lax.broadcasted_iota(jnp.int32, (L, LANES), 1)
    lo_half = lane < SSM_HEAD_DIM
    lane1 = lax.broadcasted_iota(jnp.int32, (1, LANES), 1)
    eye = jnp.where(ri == ci, 1.0, 0.0).astype(BF16)

    def pair_bcast(arr, p):
        rows = arr.shape[0]
        a0 = jnp.broadcast_to(arr[:, 2 * p:2 * p + 1], (rows, LANES))
        a1 = jnp.broadcast_to(arr[:, 2 * p + 1:2 * p + 2], (rows, LANES))
        return jnp.where(lo_half[:rows], a0, a1)

    for g in range(SSM_GROUPS):
        bg = xc[:, d_inner + g * SSM_STATE:d_inner + (g + 1) * SSM_STATE].astype(BF16)
        cg = xc[:, d_inner + gs + g * SSM_STATE:d_inner + gs + (g + 1) * SSM_STATE].astype(BF16)
        cb = _dot_nt(cg, bg)
        bgT = _dot_nt(eye, bg).astype(BF16)
        for pp in range(pairs_per_group):
            p = g * pairs_per_group + pp
            sl = slice(p * LANES, (p + 1) * LANES)
            xs_p = xc[:, sl]
            xdt_p = xs_p * pair_bcast(dtv, p)
            ydiag = None
            for a in range(2):
                h = 2 * p + a
                seg = jnp.broadcast_to(la[:, h:h + 1], (L, L)) - laT[h:h + 1, :]
                decay = jnp.exp(jnp.where(causal, seg, NEG_INF))
                m = (cb * decay).astype(BF16)
                half = lo_half if a == 0 else jnp.logical_not(lo_half)
                part = _dot(m, jnp.where(half, xdt_p, 0.0).astype(BF16))
                ydiag = part if ydiag is None else ydiag + part
            hin = h_scr[:, sl]
            yoff = _dot(cg, hin.astype(BF16)) * pair_bcast(ela, p)
            y_scr[:, sl] = ydiag + yoff + dskip_ref[:, sl] * xs_p
            s_new = _dot(bgT, (xdt_p * pair_bcast(to_end, p)).astype(BF16))
            cd_p = jnp.where(lane1 < SSM_HEAD_DIM,
                             jnp.broadcast_to(cdec[:, 2 * p:2 * p + 1], (1, LANES)),
                             jnp.broadcast_to(cdec[:, 2 * p + 1:2 * p + 2], (1, LANES)))
            h_scr[:, sl] = hin * cd_p + s_new

    gw = d_inner // SSM_GROUPS
    for g in range(SSM_GROUPS):
        sl = slice(g * gw, (g + 1) * gw)
        yg = y_scr[0:lr, sl] * _silu(z_ref[:, sl])
        ms = jnp.mean(yg * yg, axis=-1, keepdims=True)
        y_ref[:, sl] = (yg * lax.rsqrt(ms + RMS_EPS) * ng_ref[:, sl]).astype(y_ref.dtype)

    @pl.when(c == pl.num_programs(1) - 1)
    def _():
        hout_ref[0] = h_scr[...]


def _ssd(proj, dt, dtT_seq, h0T, cprev8, cw, cb, dtb, dtbT, a, aT, dskip, ng, bsz, t, lr,
         z_blk, xbc_blk, d_inner, conv_ch):
    n = proj.shape[0]
    nc = t // lr
    n_heads = d_inner // SSM_HEAD_DIM
    L = SSD_CHUNK
    const2 = lambda b, c: (0, 0)
    return pl.pallas_call(
        functools.partial(_ssd_kernel, lr, d_inner),
        out_shape=(jax.ShapeDtypeStruct((n, d_inner), BF16 if lr % 16 == 0 else F32),
                   jax.ShapeDtypeStruct((bsz, SSM_STATE, d_inner), F32)),
        grid=(bsz, nc),
        in_specs=[pl.BlockSpec((lr, conv_ch), lambda b, c: (b * nc + c, xbc_blk)),
                  pl.BlockSpec((lr, d_inner), lambda b, c: (b * nc + c, z_blk)),
                  pl.BlockSpec((lr, LANES), lambda b, c: (b * nc + c, 0)),
                  pl.BlockSpec((1, n_heads, lr), lambda b, c: (b, 0, c)),
                  pl.BlockSpec((1, SSM_STATE, d_inner), lambda b, c: (b, 0, 0)),
                  pl.BlockSpec((1, SUBLANES, conv_ch), lambda b, c: (b, 0, 0)),
                  pl.BlockSpec((CONV_W, conv_ch), const2),
                  pl.BlockSpec((1, conv_ch), const2),
                  pl.BlockSpec((1, LANES), const2),
                  pl.BlockSpec((n_heads, L), const2),
                  pl.BlockSpec((1, LANES), const2),
                  pl.BlockSpec((n_heads, L), const2),
                  pl.BlockSpec((1, d_inner), const2),
                  pl.BlockSpec((1, d_inner), const2)],
        out_specs=(pl.BlockSpec((lr, d_inner), lambda b, c: (b * nc + c, 0)),
                   pl.BlockSpec((1, SSM_STATE, d_inner), lambda b, c: (b, 0, 0))),
        scratch_shapes=[pltpu.VMEM((SUBLANES + L, conv_ch), F32),
                        pltpu.VMEM((SSM_STATE, d_inner), F32),
                        pltpu.VMEM((L, d_inner), F32),
                        pltpu.VMEM((L, LANES), F32),
                        pltpu.VMEM((n_heads, L), F32)],
        compiler_params=_cparams(2),
        name="ssd",
    )(proj, proj, dt, dtT_seq, h0T, cprev8, cw, cb, dtb, dtbT, a, aT, dskip, ng)


def _attn_prompt_kernel(nb, slopes_ref, q_ref, k_ref, v_ref, o_ref,
                        kb_scr, vb_scr, km_scr, selb_scr, m_scr, l_scr, acc_scr):
    bs = MOBA_BLOCK
    hd = ATTN_HEAD_DIM
    kvh = pl.program_id(1)
    qi = pl.program_id(2)
    par = kvh % 2

    @pl.when(jnp.logical_and(par == 0, qi == 0))
    def _():
        km_scr[...] = jnp.zeros(km_scr.shape, F32)
        for i in range(nb):
            kblk = k_ref[i * bs:(i + 1) * bs, :]
            kb_scr[i * bs:(i + 1) * bs, :] = kblk.astype(BF16)
            vb_scr[i * bs:(i + 1) * bs, :] = v_ref[i * bs:(i + 1) * bs, :].astype(BF16)
            km_scr[i:i + 1, :] = jnp.sum(kblk, axis=0, keepdims=True) * (1.0 / bs)

    lane = lax.broadcasted_iota(jnp.int32, (bs, LANES), 1)
    is_even = par == 0
    mine = (lane >= hd) == (par == 1)
    q = q_ref[...]
    qr = pltpu.roll(q, hd, 1)
    q0 = jnp.where(mine, jnp.where(is_even, q, qr), 0.0)
    q1 = jnp.where(mine, jnp.where(is_even, qr, q), 0.0)
    qq = jnp.concatenate([q0, q1], axis=0)

    lane2 = lax.broadcasted_iota(jnp.int32, (2 * bs, LANES), 1)
    gate = _dot_nt_precise(qq, km_scr[...])
    gate = jnp.where(lane2 < qi, gate, NEG_INF)
    sel, _ = _topk_mask(gate, lane2, MOBA_TOPK)
    selb_scr[...] = jnp.where(sel, 0.0, NEG_INF)

    slope0 = slopes_ref[2 * kvh]
    slope1 = slopes_ref[2 * kvh + 1]
    rowi = lax.broadcasted_iota(jnp.int32, (2 * bs, 1), 0)
    slope_col = jnp.where(rowi < bs, slope0, slope1)
    r2 = lax.broadcasted_iota(jnp.int32, (2 * bs, bs), 0)
    c2 = lax.broadcasted_iota(jnp.int32, (2 * bs, bs), 1)
    rel = (jnp.where(r2 >= bs, r2 - bs, r2) - c2)
    relf = rel.astype(F32)
    bias = -slope_col * relf

    qb = (qq * (hd ** -0.5)).astype(BF16)

    off = pl.multiple_of(qi * bs, bs)
    s = _dot_nt(qb, kb_scr[pl.ds(off, bs), :]) + bias
    s = jnp.where(rel >= 0, s, NEG_INF)
    m0 = jnp.max(s, axis=-1, keepdims=True)
    p = jnp.exp(s - m0)
    m_scr[...] = m0
    l_scr[...] = jnp.sum(p, axis=-1, keepdims=True)
    acc_scr[...] = _dot(p.astype(BF16), vb_scr[pl.ds(off, bs), :])

    def body(j, carry):
        offj = pl.multiple_of(j * bs, bs)
        selcol = jnp.max(jnp.where(lane2 == j, selb_scr[...], NEG_INF), axis=-1, keepdims=True)
        dist0 = ((qi - j) * bs).astype(F32)
        rowb = selcol - slope_col * dist0
        sj = _dot_nt(qb, kb_scr[pl.ds(offj, bs), :]) + bias + rowb
        m_old = m_scr[...]
        m_new = jnp.maximum(m_old, jnp.max(sj, axis=-1, keepdims=True))
        alpha = jnp.exp(m_old - m_new)
        pj = jnp.exp(sj - m_new)
        l_scr[...] = alpha * l_scr[...] + jnp.sum(pj, axis=-1, keepdims=True)
        acc_scr[...] = alpha * acc_scr[...] + _dot(pj.astype(BF16), vb_scr[pl.ds(offj, bs), :])
        m_scr[...] = m_new
        return carry

    lax.fori_loop(0, qi, body, 0)

    o = acc_scr[...] / l_scr[...]
    o0 = o[0:bs]
    o1 = o[bs:2 * bs]
    o0r = pltpu.roll(o0, hd, 1)
    o1r = pltpu.roll(o1, hd, 1)
    out = jnp.where(lane < hd, jnp.where(is_even, o0, o0r), jnp.where(is_even, o1r, o1))
    o_ref[...] = out.astype(o_ref.dtype)


def _attn_prompt(proj, slopes, bsz, t, n_kv, q_col, k_col, v_col):
    n = proj.shape[0]
    bs = MOBA_BLOCK
    nb = t // bs
    grid_spec = pltpu.PrefetchScalarGridSpec(
        num_scalar_prefetch=1,
        grid=(bsz, n_kv, nb),
        in_specs=[pl.BlockSpec((bs, LANES), lambda b, h, i, s: (b * nb + i, q_col // LANES + h)),
                  pl.BlockSpec((t, LANES), lambda b, h, i, s: (b, k_col // LANES + h // 2)),
                  pl.BlockSpec((t, LANES), lambda b, h, i, s: (b, v_col // LANES + h // 2))],
        out_specs=pl.BlockSpec((bs, LANES), lambda b, h, i, s: (b * nb + i, h)),
        scratch_shapes=[pltpu.VMEM((t, LANES), BF16),
                        pltpu.VMEM((t, LANES), BF16),
                        pltpu.VMEM((LANES, LANES), F32),
                        pltpu.VMEM((2 * bs, LANES), F32),
                        pltpu.VMEM((2 * bs, 1), F32),
                        pltpu.VMEM((2 * bs, 1), F32),
                        pltpu.VMEM((2 * bs, LANES), F32)])
    return pl.pallas_call(
        functools.partial(_attn_prompt_kernel, nb),
        out_shape=jax.ShapeDtypeStruct((n, n_kv * LANES), BF16),
        grid_spec=grid_spec,
        compiler_params=_cparams(3),
        name="attn_prompt",
    )(slopes, proj, proj, proj)


def _diag_extract(o, n_kv, rows_per_kv):
    hd = ATTN_HEAD_DIM
    tiles = []
    for k in range(n_kv):
        tl = o[k * rows_per_kv:(k + 1) * rows_per_kv, (k // 2) * LANES:(k // 2 + 1) * LANES]
        if k % 2 == 1:
            tl = pltpu.roll(tl, hd, 1)
        tiles.append(tl)
    return jnp.concatenate(tiles, axis=0)


def _attn_sample_kernel(bps, n_kv, tq, past_len, pt_ref, *refs):
    npg = 2 * bps
    kp = refs[0:npg]
    vp = refs[npg:2 * npg]
    qbd_ref, slope_ref, trow_ref, knew_ref, vnew_ref, o_ref, st_scr, km_scr, kn_scr, vn_scr = refs[2 * npg:]
    bs = MOBA_BLOCK
    hd = ATTN_HEAD_DIM
    rows = qbd_ref.shape[1]
    rpk = rows // n_kv
    s_id = pl.program_id(1)
    n_steps = pl.num_programs(1)
    nblk = past_len // bs

    @pl.when(s_id == 0)
    def _():
        km_scr[...] = jnp.zeros(km_scr.shape, F32)

    qf = qbd_ref[0]
    qb = (qf * (hd ** -0.5)).astype(BF16)
    slope = slope_ref[...]
    lane = lax.broadcasted_iota(jnp.int32, (rows, LANES), 1)
    cpos = lax.broadcasted_iota(jnp.int32, (rows, bs), 1).astype(F32)
    slope_c = jnp.concatenate([slope, slope], axis=1) * cpos

    for bi in range(bps):
        j = s_id * bps + bi
        kblk = jnp.concatenate([kp[2 * bi][0], kp[2 * bi + 1][0]], axis=0)
        vblk = jnp.concatenate([vp[2 * bi][0], vp[2 * bi + 1][0]], axis=0)
        km_scr[pl.ds(j, 1), :] = jnp.sum(kblk, axis=0, keepdims=True) * (1.0 / bs)
        sc = _dot_nt(qb, kblk.astype(BF16)) + slope_c
        m = jnp.max(sc, axis=-1, keepdims=True)
        p = jnp.exp(sc - m)
        l = jnp.sum(p, axis=-1, keepdims=True)
        o = _dot(p.astype(BF16), vblk.astype(BF16))
        od = _diag_extract(o, n_kv, rpk)
        st_scr[j] = jnp.where(lane < hd, od, jnp.where(lane == hd, m, l))

    @pl.when(s_id == n_steps - 1)
    def _():
        trow = trow_ref[...]
        gate = _dot_nt_precise(qf, km_scr[...])
        gate = jnp.where(lane < nblk, gate, NEG_INF)
        sel, _ = _topk_mask(gate, lane, MOBA_TOPK)
        selneg = jnp.where(sel, 0.0, NEG_INF)

        kn_scr[...] = jnp.zeros(kn_scr.shape, F32)
        vn_scr[...] = jnp.zeros(vn_scr.shape, F32)
        kn_scr[0:tq, :] = knew_ref[...]
        vn_scr[0:tq, :] = vnew_ref[...]
        lanef = lane.astype(F32)
        so = _dot_nt(qb, kn_scr[...].astype(BF16)) - slope * (trow - lanef)
        so = jnp.where(jnp.logical_and(lanef <= trow, lane < tq), so, NEG_INF)
        m_run = jnp.max(so, axis=-1, keepdims=True)
        po = jnp.exp(so - m_run)
        den = jnp.sum(po, axis=-1, keepdims=True)
        num = _diag_extract(_dot(po.astype(BF16), vn_scr[...].astype(BF16)), n_kv, rpk)

        slope1 = slope[:, 0:1]
        base = -slope1 * (float(past_len) + trow[:, 0:1])

        def body(j, carry):
            m_run, num, den = carry
            tile = st_scr[j]
            selcol = jnp.max(jnp.where(lane == j, selneg, NEG_INF), axis=-1, keepdims=True)
            mj = tile[:, hd:hd + 1] + (base + slope1 * (j * bs).astype(F32)) + selcol
            lj = tile[:, hd + 1:hd + 2]
            m_new = jnp.maximum(m_run, mj)
            a = jnp.exp(m_run - m_new)
            b = jnp.exp(mj - m_new)
            return m_new, a * num + b * tile, a * den + b * lj

        m_run, num, den = lax.fori_loop(0, nblk, body, (m_run, num, den))
        o_ref[0] = num / den


def _attn_sample(cache_k4, cache_v4, layer, page_table, qbd, slope_rows, t_rows, k_new, v_new, tq, n_kv, past_len):
    nbat, rows, kvw = qbd.shape
    page = cache_k4.shape[2]
    bs = MOBA_BLOCK
    nblk = past_len // bs
    bps = 4
    while nblk % bps:
        bps //= 2
    n_steps = nblk // bps
    ppb = bs // page
    assert ppb == 2
    npg = ppb * bps

    def page_spec(r):
        return pl.BlockSpec((None, 1, page, kvw), lambda b, s, pt: (layer, pt[b, s * npg + r], 0, 0))

    in_specs = ([page_spec(r) for r in range(npg)] + [page_spec(r) for r in range(npg)]
                + [pl.BlockSpec((1, rows, kvw), lambda b, s, pt: (b, 0, 0)),
                   pl.BlockSpec((rows, LANES), lambda b, s, pt: (0, 0)),
                   pl.BlockSpec((rows, LANES), lambda b, s, pt: (0, 0)),
                   pl.BlockSpec((tq, kvw), lambda b, s, pt: (b, 0)),
                   pl.BlockSpec((tq, kvw), lambda b, s, pt: (b, 0))])
    grid_spec = pltpu.PrefetchScalarGridSpec(
        num_scalar_prefetch=1,
        grid=(nbat, n_steps),
        in_specs=in_specs,
        out_specs=pl.BlockSpec((1, rows, LANES), lambda b, s, pt: (b, 0, 0)),
        scratch_shapes=[pltpu.VMEM((nblk, rows, LANES), F32),
                        pltpu.VMEM((LANES, kvw), F32),
                        pltpu.VMEM((LANES, kvw), F32),
                        pltpu.VMEM((LANES, kvw), F32)])
    assert nblk <= LANES
    return pl.pallas_call(
        functools.partial(_attn_sample_kernel, bps, n_kv, tq, past_len),
        out_shape=jax.ShapeDtypeStruct((nbat, rows, LANES), F32),
        grid_spec=grid_spec,
        compiler_params=_cparams(2),
        name="attn_sample",
    )(page_table, *([cache_k4] * npg), *([cache_v4] * npg), qbd, slope_rows, t_rows, k_new, v_new)


def _outproj_kernel(alpha, n_exp, ys_ref, ya_ref, gs_ref, ga_ref, x_ref, gm_ref, scf_ref, shf_ref,
                    wos_ref, woa_ref, wout_ref, g1_ref, b1_ref, wr_ref, br_ref,
                    x1_ref, hf_ref, gates_ref):
    merged = (jax.nn.sigmoid(gs_ref[...]) * _dot(ys_ref[...].astype(BF16), wos_ref[...])
              + jax.nn.sigmoid(ga_ref[...]) * _dot(ya_ref[...], woa_ref[...]))
    upd = _dot(merged.astype(BF16), wout_ref[...])
    x1 = _layer_norm(alpha * x_ref[...] + gm_ref[0] * upd, g1_ref[...], b1_ref[...])
    x1_ref[...] = x1
    hf = x1 * scf_ref[0] + shf_ref[0]
    hf_ref[...] = hf.astype(BF16)
    logits = _dot_precise(hf, wr_ref[...]) + br_ref[...]
    lane = lax.broadcasted_iota(jnp.int32, logits.shape, 1)
    logits = jnp.where(lane < n_exp, logits, NEG_INF)
    g = logits
    picks, vals = [], []
    for _ in range(TOP_K):
        m = jnp.max(g, axis=-1, keepdims=True)
        idx = jnp.min(jnp.where(g == m, lane, LANES), axis=-1, keepdims=True)
        pick = lane == idx
        g = jnp.where(pick, NEG_INF, g)
        picks.append(pick)
        vals.append(m)
    es = [jnp.exp(v - vals[0]) for v in vals]
    inv = 1.0 / (es[0] + es[1] + es[2] + es[3])
    gates = jnp.zeros(logits.shape, F32)
    for pick, e in zip(picks, es):
        gates = gates + jnp.where(pick, e * inv, 0.0)
    gates_ref[...] = gates


def _outproj(alpha, n_exp, ys, ya, proj, x, gm, scf1, shf, wos, woa, wout, g1, b1, wr, br, tm, tiles_per_group,
             gs_blk, ga_blk):
    n, d = x.shape
    r = gm.shape[1]
    grp = lambda i: (i // tiles_per_group, 0, 0)
    const = lambda i: (0, 0)
    return pl.pallas_call(
        functools.partial(_outproj_kernel, alpha, n_exp),
        out_shape=(jax.ShapeDtypeStruct((n, d), F32),
                   jax.ShapeDtypeStruct((n, d), BF16),
                   jax.ShapeDtypeStruct((n, LANES), F32)),
        grid=(n // tm,),
        in_specs=[pl.BlockSpec((tm, ys.shape[1]), lambda i: (i, 0)),
                  pl.BlockSpec((tm, ya.shape[1]), lambda i: (i, 0)),
                  pl.BlockSpec((tm, d), lambda i: (i, gs_blk)),
                  pl.BlockSpec((tm, d), lambda i: (i, ga_blk)),
                  pl.BlockSpec((tm, d), lambda i: (i, 0)),
                  pl.BlockSpec((1, r, d), grp),
                  pl.BlockSpec((1, r, d), grp),
                  pl.BlockSpec((1, r, d), grp),
                  pl.BlockSpec(wos.shape, const),
                  pl.BlockSpec(woa.shape, const),
                  pl.BlockSpec(wout.shape, const),
                  pl.BlockSpec((1, d), const),
                  pl.BlockSpec((1, d), const),
                  pl.BlockSpec((d, LANES), const),
                  pl.BlockSpec((1, LANES), const)],
        out_specs=(pl.BlockSpec((tm, d), lambda i: (i, 0)),
                   pl.BlockSpec((tm, d), lambda i: (i, 0)),
                   pl.BlockSpec((tm, LANES), lambda i: (i, 0))),
        compiler_params=_cparams(1),
        name="outproj",
    )(ys, ya, proj, proj, x, gm, scf1, shf, wos, woa, wout, g1, b1, wr, br)


def _moe_kernel(alpha, d_ff, hf_ref, gates_ref, x1_ref, gf_ref, wgu_ref, bgu_ref, wdn_ref, bdn_ref,
                g2_ref, b2_ref, o_ref, acc_scr):
    e = pl.program_id(1)

    @pl.when(e == 0)
    def _():
        acc_scr[...] = jnp.zeros(acc_scr.shape, F32)

    gu = _dot(hf_ref[...], wgu_ref[0]) + bgu_ref[0]
    gg = jnp.minimum(gu[:, :d_ff], SWIGLU_LIMIT)
    uu = jnp.clip(gu[:, d_ff:], -SWIGLU_LIMIT, SWIGLU_LIMIT)
    act = (uu + 1.0) * (gg * jax.nn.sigmoid(SWIGLU_ALPHA * gg))
    y = _dot(act.astype(BF16), wdn_ref[0]) + bdn_ref[0]
    gates = gates_ref[...]
    lane = lax.broadcasted_iota(jnp.int32, gates.shape, 1)
    gcol = jnp.sum(jnp.where(lane == e, gates, 0.0), axis=-1, keepdims=True)
    acc_scr[...] += gcol * y

    @pl.when(e == pl.num_programs(1) - 1)
    def _():
        o_ref[...] = _layer_norm(alpha * x1_ref[...] + gf_ref[0] * acc_scr[...], g2_ref[...], b2_ref[...])


def _moe(alpha, hf, gates, x1, gf, wgu, bgu, wdn, bdn, g2, b2, tm, tiles_per_group):
    n, d = x1.shape
    n_exp, _, two_ff = wgu.shape
    d_ff = two_ff // 2
    r = gf.shape[1]
    return pl.pallas_call(
        functools.partial(_moe_kernel, alpha, d_ff),
        out_shape=jax.ShapeDtypeStruct((n, d), F32),
        grid=(n // tm, n_exp),
        in_specs=[pl.BlockSpec((tm, d), lambda i, e: (i, 0)),
                  pl.BlockSpec((tm, LANES), lambda i, e: (i, 0)),
                  pl.BlockSpec((tm, d), lambda i, e: (i, 0)),
                  pl.BlockSpec((1, r, d), lambda i, e: (i // tiles_per_group, 0, 0)),
                  pl.BlockSpec((1, d, two_ff), lambda i, e: (e, 0, 0)),
                  pl.BlockSpec((1, 1, two_ff), lambda i, e: (e, 0, 0)),
                  pl.BlockSpec((1, d_ff, d), lambda i, e: (e, 0, 0)),
                  pl.BlockSpec((1, 1, d), lambda i, e: (e, 0, 0)),
                  pl.BlockSpec((1, d), lambda i, e: (0, 0)),
                  pl.BlockSpec((1, d), lambda i, e: (0, 0))],
        out_specs=pl.BlockSpec((tm, d), lambda i, e: (i, 0)),
        scratch_shapes=[pltpu.VMEM((tm, d), F32)],
        compiler_params=_cparams(2),
        name="moe",
    )(hf, gates, x1, gf, wgu, bgu, wdn, bdn, g2, b2)


def _pad_lanes(v, fill=0.0):
    return jnp.pad(v, (0, LANES - v.shape[0]), constant_values=fill).reshape(1, LANES)


def kernel(x_prompt, x_sample, c_prompt, c_sample, cache_k, cache_v, state_ssm, state_conv, page_table, w_ada, b_ada, w_in, conv_w, conv_b, dt_bias, a_log, d_skip, ssm_norm_g, w_o_ssm, w_o_attn, w_out, ln1_g, ln1_b, ln2_g, ln2_b, w_router, b_router, w_gu, b_gu, w_down, b_down):
    depth, d, _ = w_ada.shape
    bp, tp, _ = x_prompt.shape
    bsm, ts, _ = x_sample.shape
    n_heads = dt_bias.shape[1]
    d_inner = n_heads * SSM_HEAD_DIM
    conv_ch = conv_w.shape[2]
    kvw = cache_k.shape[3] * cache_k.shape[4]
    n_kv = cache_k.shape[3]
    aw = w_o_attn.shape[1]
    n_q = aw // ATTN_HEAD_DIM
    n_exp = w_router.shape[2]
    page = cache_k.shape[2]
    past_len = page_table.shape[1] * page
    alpha = float((2.0 * depth) ** 0.25)
    assert d_inner == 2 * d and aw == d and kvw == d // 2 and n_q == 2 * n_kv
    assert conv_ch == d_inner + 2 * SSM_GROUPS * SSM_STATE and conv_ch == 3 * d
    assert tp % MOBA_BLOCK == 0 and tp % SSD_CHUNK == 0 and ts == SUBLANES and n_heads <= LANES

    o_z, o_xbc, o_dt = 0, d_inner, d_inner + conv_ch
    o_q = o_dt + n_heads
    o_k, o_v = o_q + aw, o_q + aw + kvw
    o_gs, o_ga = o_v + kvw, o_v + kvw + d
    c_z, c_q, c_gs, c_ga = 0, d_inner, d_inner + d, d_inner + 2 * d
    c_k = d_inner + 3 * d
    c_v = c_k + kvw
    c_xbc = c_v + kvw
    assert c_xbc % conv_ch == 0

    slopes = 2.0 ** (-8.0 * (jnp.arange(n_q, dtype=F32) + 1.0) / n_q)

    n_c = bsm + bp
    n_c_pad = -(-n_c // SUBLANES) * SUBLANES
    c_all = jnp.concatenate([c_sample, c_prompt, jnp.zeros((n_c_pad - n_c, d), F32)], axis=0)
    mod = _ada(c_all, w_ada, b_ada)

    grp = n_q // n_kv
    rows = n_kv * grp * ts
    r_idx = jnp.arange(rows)
    r_head = (r_idx // (grp * ts)) * grp + (r_idx // ts) % grp
    slope_rows = jnp.broadcast_to(slopes[r_head][:, None], (rows, LANES))
    t_rows = jnp.broadcast_to((r_idx % ts).astype(F32)[:, None], (rows, LANES))
    eye_kv = jnp.eye(n_kv, dtype=F32)

    cache_k4 = cache_k.reshape(depth, cache_k.shape[1], page, kvw)
    cache_v4 = cache_v.reshape(depth, cache_v.shape[1], page, kvw)

    xp = x_prompt.reshape(bp * tp, d)
    xs = x_sample.reshape(bsm * ts, d)
    tm_p = 1024 if tp % 1024 == 0 else MOBA_BLOCK
    tm_o = 512 if tp % 512 == 0 else MOBA_BLOCK
    n_s = bsm * ts

    outs = {k: [] for k in ("kp", "vp", "hp", "cvp", "ks", "vs", "hs", "cvs")}
    for l in range(depth):
        wl = w_in[l]
        w_main = jnp.concatenate([wl[:, o_z:o_z + d_inner], wl[:, o_q:o_q + aw], wl[:, o_gs:o_gs + d],
                                  wl[:, o_ga:o_ga + d], wl[:, o_k:o_k + kvw], wl[:, o_v:o_v + kvw],
                                  wl[:, o_xbc:o_xbc + conv_ch]], axis=1).astype(BF16)
        w_dt = jnp.pad(wl[:, o_dt:o_dt + n_heads], ((0, 0), (0, LANES - n_heads))).astype(BF16)
        w_dtT = wl[:, o_dt:o_dt + n_heads].T.astype(BF16)
        dtb = _pad_lanes(dt_bias[l])
        dtbT = jnp.broadcast_to(dt_bias[l][:, None], (n_heads, SSD_CHUNK))
        a_neg = -jnp.exp(a_log[l])
        a_row = _pad_lanes(a_neg)
        aT = jnp.broadcast_to(a_neg[:, None], (n_heads, SSD_CHUNK))
        dskip = jnp.repeat(d_skip[l], SSM_HEAD_DIM).reshape(1, d_inner)
        ng = ssm_norm_g[l].reshape(1, d_inner)
        cw = conv_w[l]
        cb = conv_b[l].reshape(1, conv_ch)
        wos = w_o_ssm[l].astype(BF16)
        woa = w_o_attn[l].astype(BF16)
        wout = w_out[l].astype(BF16)
        g1, b1 = ln1_g[l].reshape(1, d), ln1_b[l].reshape(1, d)
        g2, b2 = ln2_g[l].reshape(1, d), ln2_b[l].reshape(1, d)
        wr = jnp.pad(w_router[l], ((0, 0), (0, LANES - n_exp)))
        br = _pad_lanes(b_router[l])
        wgu = w_gu[l].astype(BF16)
        bgu = b_gu[l].reshape(n_exp, 1, -1)
        wdn = w_down[l].astype(BF16)
        bdn = b_down[l].reshape(n_exp, 1, d)

        mod_l = mod[l]
        sh_m, sc_m, g_m, sh_f, sc_f, g_f = [mod_l[:, i * d:(i + 1) * d] for i in range(6)]

        def group_mod(v, is_prompt):
            if is_prompt:
                return v[bsm:bsm + bp].reshape(bp, 1, d)
            return jnp.repeat(v[:bsm], ts, axis=0).reshape(1, n_s, d)

        def run(x, is_prompt, h0T, cprev8):
            bsz, t = (bp, tp) if is_prompt else (bsm, ts)
            n = bsz * t
            tm = tm_p if is_prompt else n
            tmo = tm_o if is_prompt else n
            tpg = (t // tm) if is_prompt else 1
            tpgo = (t // tmo) if is_prompt else 1
            gm = lambda v: group_mod(v, is_prompt)
            proj, dt, dtT = _inproj(x, gm(1.0 + sc_m), gm(sh_m), w_main, w_dt, w_dtT, tm, tpg)
            dtT_seq = dtT.reshape(n_heads, bsz, t).transpose(1, 0, 2)
            lr = SSD_CHUNK if is_prompt else t
            y_s, h_newT = _ssd(proj, dt, dtT_seq, h0T, cprev8, cw, cb, dtb, dtbT, a_row, aT, dskip, ng,
                               bsz, t, lr, c_z // d_inner, c_xbc // conv_ch, d_inner, conv_ch)
            k_new = proj[:, c_k:c_k + kvw]
            v_new = proj[:, c_v:c_v + kvw]
            if is_prompt:
                y_a = _attn_prompt(proj, slopes, bsz, t, n_kv, c_q, c_k, c_v)
            else:
                q5 = proj[:, c_q:c_q + aw].reshape(bsz, t, n_kv, grp, ATTN_HEAD_DIM)
                q5 = q5.transpose(0, 2, 3, 1, 4).reshape(bsz, n_kv, grp * t, ATTN_HEAD_DIM)
                qbd = (q5[:, :, :, None, :] * eye_kv[None, :, None, :, None]).reshape(bsz, rows, kvw)
                o = _attn_sample(cache_k4, cache_v4, l, page_table, qbd, slope_rows, t_rows, k_new, v_new,
                                 t, n_kv, past_len)
                o = o[:, :, :ATTN_HEAD_DIM].reshape(bsz, n_kv, grp, t, ATTN_HEAD_DIM)
                y_a = o.transpose(0, 3, 1, 2, 4).reshape(n, aw).astype(BF16)
            x1, hf, gates = _outproj(alpha, n_exp, y_s, y_a, proj, x, gm(g_m), gm(1.0 + sc_f), gm(sh_f),
                                     wos, woa, wout, g1, b1, wr, br, tmo, tpgo, c_gs // d, c_ga // d)
            x2 = _moe(alpha, hf, gates, x1, gm(g_f), wgu, bgu, wdn, bdn, g2, b2, tmo, tpgo)
            conv_new = proj[:, c_xbc:c_xbc + conv_ch].reshape(bsz, t, conv_ch)[:, t - (CONV_W - 1):]
            h_new = h_newT.transpose(0, 2, 1).reshape(bsz, n_heads, SSM_HEAD_DIM, SSM_STATE)
            hd4 = (bsz, t, n_kv, ATTN_HEAD_DIM)
            return x2, k_new.reshape(hd4), v_new.reshape(hd4), h_new, conv_new

        h0_p = jnp.zeros((bp, SSM_STATE, d_inner), F32)
        cv0_p = jnp.zeros((bp, SUBLANES, conv_ch), F32)
        xp, k1, v1, h1, cv1 = run(xp, True, h0_p, cv0_p)
        outs["kp"].append(k1); outs["vp"].append(v1); outs["hp"].append(h1); outs["cvp"].append(cv1)

        h0_s = state_ssm[l].reshape(bsm, d_inner, SSM_STATE).transpose(0, 2, 1)
        cv0_s = jnp.pad(state_conv[l], ((0, 0), (SUBLANES - (CONV_W - 1), 0), (0, 0)))
        xs, k1, v1, h1, cv1 = run(xs, False, h0_s, cv0_s)
        outs["ks"].append(k1); outs["vs"].append(v1); outs["hs"].append(h1); outs["cvs"].append(cv1)

    st = lambda k: jnp.stack(outs[k])
    return (xp.reshape(bp, tp, d), xs.reshape(bsm, ts, d), st("kp"), st("vp"), st("hp"), st("cvp"),
            st("ks"), st("vs"), st("hs"), st("cvs"))
```

```python
import functools
import math

import jax
import jax.numpy as jnp
from jax import lax
from jax.experimental import pallas as pl
from jax.experimental.pallas import tpu as pltpu

F32 = jnp.float32
BF16 = jnp.bfloat16
NEG_INF = float("-inf")

LANES = 128
SUBLANES = 8
VMEM_LIMIT = 56 * 1024 * 1024

SSM_HEAD_DIM = 64
SSM_GROUPS = 4
SSM_STATE = 128
CONV_W = 4
SSD_CHUNK = 128
RMS_EPS = 1e-5
ATTN_HEAD_DIM = 64
MOBA_BLOCK = 256
MOBA_TOPK = 3
TOP_K = 4
SWIGLU_LIMIT = 7.0
SWIGLU_ALPHA = 1.702
LN_EPS = 1e-5


def _cparams(n_axes):
    return pltpu.CompilerParams(dimension_semantics=("arbitrary",) * n_axes,
                                vmem_limit_bytes=VMEM_LIMIT)


def _dot(a, b):
    return jnp.dot(a, b, preferred_element_type=F32)


def _dot_nt(a, b):
    return lax.dot_general(a, b, (((1,), (1,)), ((), ())), preferred_element_type=F32)


def _split2(a):
    hi = a.astype(BF16)
    lo = (a - hi.astype(F32)).astype(BF16)
    return hi, lo


def _split3(a):
    hi = a.astype(BF16)
    r = a - hi.astype(F32)
    mid = r.astype(BF16)
    lo = (r - mid.astype(F32)).astype(BF16)
    return hi, mid, lo


def _dot_nt_precise(a, b):
    ah, al = _split2(a)
    bh, bl = _split2(b)
    return _dot_nt(ah, bh) + (_dot_nt(ah, bl) + _dot_nt(al, bh))


def _dot_precise(a, b):
    ah, al = _split2(a)
    bh, bl = _split2(b)
    return _dot(ah, bh) + (_dot(ah, bl) + _dot(al, bh))


def _silu(x):
    return x * jax.nn.sigmoid(x)


def _softplus(x):
    return jnp.maximum(x, 0.0) + jnp.log1p(jnp.exp(-jnp.abs(x)))


def _layer_norm(x, g, b):
    mu = jnp.mean(x, axis=-1, keepdims=True)
    xc = x - mu
    var = jnp.mean(xc * xc, axis=-1, keepdims=True)
    return xc * lax.rsqrt(var + LN_EPS) * g + b


def _topk_mask(score, lane, k):
    width = score.shape[-1]
    sel = jnp.zeros(score.shape, dtype=jnp.bool_)
    vals = []
    g = score
    for _ in range(k):
        m = jnp.max(g, axis=-1, keepdims=True)
        idx = jnp.min(jnp.where(g == m, lane, width), axis=-1, keepdims=True)
        pick = jnp.logical_and(lane == idx, m > NEG_INF)
        sel = jnp.logical_or(sel, pick)
        g = jnp.where(pick, NEG_INF, g)
        vals.append(m)
    return sel, vals


def _ada_kernel(c_ref, w_ref, b_ref, o_ref):
    s = _silu(c_ref[...]).astype(BF16)
    o_ref[0] = _dot(s, w_ref[0].astype(BF16)) + b_ref[0]


def _ada(c_all, w_ada, b_ada):
    depth, d, n6 = w_ada.shape
    rows = c_all.shape[0]
    tn = 1024
    return pl.pallas_call(
        _ada_kernel,
        out_shape=jax.ShapeDtypeStruct((depth, rows, n6), F32),
        grid=(depth, n6 // tn),
        in_specs=[pl.BlockSpec((rows, d), lambda l, j: (0, 0)),
                  pl.BlockSpec((1, d, tn), lambda l, j: (l, 0, j)),
                  pl.BlockSpec((1, 1, tn), lambda l, j: (l, 0, j))],
        out_specs=pl.BlockSpec((1, rows, tn), lambda l, j: (l, 0, j)),
        compiler_params=_cparams(2),
        name="ada",
    )(c_all, w_ada, b_ada.reshape(depth, 1, n6))


def _inproj_kernel(x_ref, sc_ref, sh_ref, w_ref, wdt_ref, wdtT_ref, o_ref, dt_ref, dtT_ref, h_scr):
    @pl.when(pl.program_id(1) == 0)
    def _():
        hb = (x_ref[...] * sc_ref[0] + sh_ref[0]).astype(BF16)
        h_scr[...] = hb
        dt_ref[...] = _dot(hb, wdt_ref[...])
        dtT_ref[...] = _dot_nt(wdtT_ref[...], hb)

    o_ref[...] = _dot(h_scr[...], w_ref[...])


def _inproj(x, sc1, sh, w_main, w_dt, w_dtT, tm, tiles_per_group):
    n, d = x.shape
    n_main = w_main.shape[1]
    tn = 1024
    r = sc1.shape[1]
    nh = w_dtT.shape[0]
    grp = lambda i, j: (i // tiles_per_group, 0, 0)
    return pl.pallas_call(
        _inproj_kernel,
        out_shape=(jax.ShapeDtypeStruct((n, n_main), F32),
                   jax.ShapeDtypeStruct((n, LANES), F32),
                   jax.ShapeDtypeStruct((nh, n), F32)),
        grid=(n // tm, n_main // tn),
        in_specs=[pl.BlockSpec((tm, d), lambda i, j: (i, 0)),
                  pl.BlockSpec((1, r, d), grp),
                  pl.BlockSpec((1, r, d), grp),
                  pl.BlockSpec((d, tn), lambda i, j: (0, j)),
                  pl.BlockSpec((d, LANES), lambda i, j: (0, 0)),
                  pl.BlockSpec((nh, d), lambda i, j: (0, 0))],
        out_specs=(pl.BlockSpec((tm, tn), lambda i, j: (i, j)),
                   pl.BlockSpec((tm, LANES), lambda i, j: (i, 0)),
                   pl.BlockSpec((nh, tm), lambda i, j: (0, i))),
        scratch_shapes=[pltpu.VMEM((tm, d), BF16)],
        compiler_params=_cparams(2),
        name="inproj",
    )(x, sc1, sh, w_main, w_dt, w_dtT)


def _ssd_kernel(lr, d_inner, xbc_ref, z_ref, dt_ref, dtT_ref, h0_ref, cprev_ref, cw_ref, cb_ref,
                dtb_ref, dtbT_ref, a_ref, aT_ref, dskip_ref, ng_ref,
                y_ref, hout_ref, xp_scr, h_scr, y_scr, dt_scr, dtT_scr):
    L = SSD_CHUNK
    gs = SSM_GROUPS * SSM_STATE
    n_heads = d_inner // SSM_HEAD_DIM
    pairs_per_group = n_heads // SSM_GROUPS // 2
    c = pl.program_id(1)

    @pl.when(c == 0)
    def _():
        xp_scr[0:SUBLANES, :] = cprev_ref[0]
        h_scr[...] = h0_ref[0]

    if lr < L:
        xp_scr[SUBLANES + lr:SUBLANES + L, :] = jnp.zeros((L - lr, xp_scr.shape[1]), F32)
    xp_scr[SUBLANES:SUBLANES + lr, :] = xbc_ref[...]

    acc = cb_ref[...] + cw_ref[0:1, :] * xp_scr[SUBLANES - 3:SUBLANES - 3 + L, :]
    for j in range(1, CONV_W):
        acc = acc + cw_ref[j:j + 1, :] * xp_scr[SUBLANES - 3 + j:SUBLANES - 3 + j + L, :]
    tail = xp_scr[lr:lr + SUBLANES, :]
    xp_scr[0:SUBLANES, :] = tail
    xc = _silu(acc)

    if lr < L:
        dt_scr[...] = jnp.zeros(dt_scr.shape, F32)
        dt_scr[0:lr, :] = dt_ref[...]
        dtT_scr[...] = jnp.zeros(dtT_scr.shape, F32)
        dtT_scr[:, 0:lr] = dtT_ref[0]
        dt_raw = dt_scr[...]
        dtT_raw = dtT_scr[...]
    else:
        dt_raw = dt_ref[...]
        dtT_raw = dtT_ref[0]
    row = lax.broadcasted_iota(jnp.int32, (L, LANES), 0)
    col = lax.broadcasted_iota(jnp.int32, (n_heads, L), 1)
    dtv = jnp.where(row < lr, _softplus(dt_raw + dtb_ref[...]), 0.0)
    dtvT = jnp.where(col < lr, _softplus(dtT_raw + dtbT_ref[...]), 0.0)
    dA = dtv * a_ref[...]
    dAT = dtvT * aT_ref[...]

    ri = lax.broadcasted_iota(jnp.int32, (L, L), 0)
    ci = lax.broadcasted_iota(jnp.int32, (L, L), 1)
    causal = ri >= ci
    tri = jnp.where(causal, 1.0, 0.0).astype(BF16)
    triT = jnp.where(ri <= ci, 1.0, 0.0).astype(BF16)
    d1, d2, d3 = _split3(dA)
    la = _dot(tri, d1) + (_dot(tri, d2) + _dot(tri, d3))
    e1, e2, e3 = _split3(dAT)
    laT = _dot(e1, triT) + (_dot(e2, triT) + _dot(e3, triT))

    la_last = la[L - 1:L, :]
    to_end = jnp.exp(la_last - la)
    ela = jnp.exp(la)
    cdec = jnp.exp(la_last)

    lane = lax.broadcasted_iota(jnp.int32, (L, LANES), 1)
    lo_half = lane < SSM_HEAD_DIM
    lane1 = lax.broadcasted_iota(jnp.int32, (1, LANES), 1)
    eye = jnp.where(ri == ci, 1.0, 0.0).astype(BF16)

    def pair_bcast(arr, p):
        rows = arr.shape[0]
        a0 = jnp.broadcast_to(arr[:, 2 * p:2 * p + 1], (rows, LANES))
        a1 = jnp.broadcast_to(arr[:, 2 * p + 1:2 * p + 2], (rows, LANES))
        return jnp.where(lo_half[:rows], a0, a1)

    for g in range(SSM_GROUPS):
        bg = xc[:, d_inner + g * SSM_STATE:d_inner + (g + 1) * SSM_STATE].astype(BF16)
        cg = xc[:, d_inner + gs + g * SSM_STATE:d_inner + gs + (g + 1) * SSM_STATE].astype(BF16)
        cb = _dot_nt(cg, bg)
        bgT = _dot_nt(eye, bg).astype(BF16)
        for pp in range(pairs_per_group):
            p = g * pairs_per_group + pp
            sl = slice(p * LANES, (p + 1) * LANES)
            xs_p = xc[:, sl]
            xdt_p = xs_p * pair_bcast(dtv, p)
            ydiag = None
            for a in range(2):
                h = 2 * p + a
                seg = jnp.broadcast_to(la[:, h:h + 1], (L, L)) - laT[h:h + 1, :]
                decay = jnp.exp(jnp.where(causal, seg, NEG_INF))
                m = (cb * decay).astype(BF16)
                half = lo_half if a == 0 else jnp.logical_not(lo_half)
                part = _dot(m, jnp.where(half, xdt_p, 0.0).astype(BF16))
                ydiag = part if ydiag is None else ydiag + part
            hin = h_scr[:, sl]
            yoff = _dot(cg, hin.astype(BF16)) * pair_bcast(ela, p)
            y_scr[:, sl] = ydiag + yoff + dskip_ref[:, sl] * xs_p
            s_new = _dot(bgT, (xdt_p * pair_bcast(to_end, p)).astype(BF16))
            cd_p = jnp.where(lane1 < SSM_HEAD_DIM,
                             jnp.broadcast_to(cdec[:, 2 * p:2 * p + 1], (1, LANES)),
                             jnp.broadcast_to(cdec[:, 2 * p + 1:2 * p + 2], (1, LANES)))
            h_scr[:, sl] = hin * cd_p + s_new

    gw = d_inner // SSM_GROUPS
    for g in range(SSM_GROUPS):
        sl = slice(g * gw, (g + 1) * gw)
        yg = y_scr[0:lr, sl] * _silu(z_ref[:, sl])
        ms = jnp.mean(yg * yg, axis=-1, keepdims=True)
        y_ref[:, sl] = (yg * lax.rsqrt(ms + RMS_EPS) * ng_ref[:, sl]).astype(y_ref.dtype)

    @pl.when(c == pl.num_programs(1) - 1)
    def _():
        hout_ref[0] = h_scr[...]


def _ssd(proj, dt, dtT_seq, h0T, cprev8, cw, cb, dtb, dtbT, a, aT, dskip, ng, bsz, t, lr,
         z_blk, xbc_blk, d_inner, conv_ch):
    n = proj.shape[0]
    nc = t // lr
    n_heads = d_inner // SSM_HEAD_DIM
    L = SSD_CHUNK
    const2 = lambda b, c: (0, 0)
    return pl.pallas_call(
        functools.partial(_ssd_kernel, lr, d_inner),
        out_shape=(jax.ShapeDtypeStruct((n, d_inner), BF16 if lr % 16 == 0 else F32),
                   jax.ShapeDtypeStruct((bsz, SSM_STATE, d_inner), F32)),
        grid=(bsz, nc),
        in_specs=[pl.BlockSpec((lr, conv_ch), lambda b, c: (b * nc + c, xbc_blk)),
                  pl.BlockSpec((lr, d_inner), lambda b, c: (b * nc + c, z_blk)),
                  pl.BlockSpec((lr, LANES), lambda b, c: (b * nc + c, 0)),
                  pl.BlockSpec((1, n_heads, lr), lambda b, c: (b, 0, c)),
                  pl.BlockSpec((1, SSM_STATE, d_inner), lambda b, c: (b, 0, 0)),
                  pl.BlockSpec((1, SUBLANES, conv_ch), lambda b, c: (b, 0, 0)),
                  pl.BlockSpec((CONV_W, conv_ch), const2),
                  pl.BlockSpec((1, conv_ch), const2),
                  pl.BlockSpec((1, LANES), const2),
                  pl.BlockSpec((n_heads, L), const2),
                  pl.BlockSpec((1, LANES), const2),
                  pl.BlockSpec((n_heads, L), const2),
                  pl.BlockSpec((1, d_inner), const2),
                  pl.BlockSpec((1, d_inner), const2)],
        out_specs=(pl.BlockSpec((lr, d_inner), lambda b, c: (b * nc + c, 0)),
                   pl.BlockSpec((1, SSM_STATE, d_inner), lambda b, c: (b, 0, 0))),
        scratch_shapes=[pltpu.VMEM((SUBLANES + L, conv_ch), F32),
                        pltpu.VMEM((SSM_STATE, d_inner), F32),
                        pltpu.VMEM((L, d_inner), F32),
                        pltpu.VMEM((L, LANES), F32),
                        pltpu.VMEM((n_heads, L), F32)],
        compiler_params=_cparams(2),
        name="ssd",
    )(proj, proj, dt, dtT_seq, h0T, cprev8, cw, cb, dtb, dtbT, a, aT, dskip, ng)


def _topk_mask_rows(score, row, k):
    big = score.shape[0]
    sel = jnp.zeros(score.shape, dtype=jnp.bool_)
    g = score
    for _ in range(k):
        m = jnp.max(g, axis=0, keepdims=True)
        idx = jnp.min(jnp.where(g == m, row, big), axis=0, keepdims=True)
        pick = jnp.logical_and(row == idx, m > NEG_INF)
        sel = jnp.logical_or(sel, pick)
        g = jnp.where(pick, NEG_INF, g)
    return sel


def _attn_prompt_kernel(nb, slopes_ref, q_ref, k_ref, v_ref, o_ref,
                        kb_scr, vt_scr, km_scr, sel_scr, bias_scr, qt_scr, m_scr, l_scr, acc_scr):
    bs = MOBA_BLOCK
    hd = ATTN_HEAD_DIM
    pr = pl.program_id(1)
    qi = pl.program_id(2)
    nbp = km_scr.shape[0]
    n_ch = 4

    @pl.when(qi == 0)
    def _():
        km_scr[...] = jnp.zeros(km_scr.shape, F32)
        for i in range(nb):
            kblk = k_ref[i * bs:(i + 1) * bs, :]
            kb_scr[i] = kblk.astype(BF16)
            vt_scr[i] = v_ref[i * bs:(i + 1) * bs, :].T.astype(BF16)
            km_scr[i:i + 1, :] = jnp.sum(kblk, axis=0, keepdims=True) * (1.0 / bs)

    lane = lax.broadcasted_iota(jnp.int32, (bs, LANES), 1)
    qts = []
    for par in range(2):
        qp = q_ref[:, par * LANES:(par + 1) * LANES]
        qr = pltpu.roll(qp, hd, 1)
        mine = (lane >= hd) if par == 1 else (lane < hd)
        q_g0 = jnp.where(mine, qp if par == 0 else qr, 0.0)
        q_g1 = jnp.where(mine, qr if par == 0 else qp, 0.0)
        qts += [q_g0.T, q_g1.T]

    rowb = lax.broadcasted_iota(jnp.int32, (nbp, bs), 0)
    rk = lax.broadcasted_iota(jnp.int32, (bs, bs), 0)
    cq = lax.broadcasted_iota(jnp.int32, (bs, bs), 1)
    rel = (cq - rk).astype(F32)
    km = km_scr[...]
    slopes = [slopes_ref[n_ch * pr + c] for c in range(n_ch)]
    kd = kb_scr[qi]
    for c in range(n_ch):
        par = c // 2
        gate = _dot_precise(km, qts[c])
        gate = jnp.where(rowb < qi, gate, NEG_INF)
        sel_scr[c] = jnp.where(_topk_mask_rows(gate, rowb, MOBA_TOPK), 0.0, NEG_INF)
        bias = -slopes[c] * rel
        bias_scr[c] = bias
        qtb = (qts[c] * (hd ** -0.5)).astype(BF16)
        qt_scr[c] = qtb
        s = _dot(kd, qtb) + bias
        s = jnp.where(rel >= 0.0, s, NEG_INF)
        m0 = jnp.max(s, axis=0, keepdims=True)
        p = jnp.exp(s - m0)
        m_scr[c] = m0
        l_scr[c] = jnp.sum(p, axis=0, keepdims=True)
        acc_scr[c] = _dot(vt_scr[qi, par * hd:(par + 1) * hd, :], p.astype(BF16))

    def body(j, carry):
        kj = kb_scr[j]
        dist0 = ((qi - j) * bs).astype(F32)
        ss = [_dot(kj, qt_scr[c]) for c in range(n_ch)]
        for c in range(n_ch):
            par = c // 2
            colb = sel_scr[c, pl.ds(j, 1), :] - slopes[c] * dist0
            sj = ss[c] + bias_scr[c] + colb
            m_old = m_scr[c]
            m_new = jnp.maximum(m_old, jnp.max(sj, axis=0, keepdims=True))
            alpha = jnp.exp(m_old - m_new)
            pj = jnp.exp(sj - m_new)
            l_scr[c] = alpha * l_scr[c] + jnp.sum(pj, axis=0, keepdims=True)
            pv = _dot(vt_scr[j, par * hd:(par + 1) * hd, :], pj.astype(BF16))
            acc_scr[c] = alpha * acc_scr[c] + pv
            m_scr[c] = m_new
        return carry

    lax.fori_loop(0, qi, body, 0)

    for par in range(2):
        o_g0 = acc_scr[2 * par] / l_scr[2 * par]
        o_g1 = acc_scr[2 * par + 1] / l_scr[2 * par + 1]
        o_ref[:, par * LANES:(par + 1) * LANES] = jnp.concatenate([o_g0, o_g1], axis=0).T.astype(o_ref.dtype)


def _attn_prompt(proj, slopes, bsz, t, n_kv, q_col, k_col, v_col):
    n = proj.shape[0]
    bs = MOBA_BLOCK
    nb = t // bs
    nbp = -(-nb // 16) * 16
    n_pairs = n_kv // 2
    qw = 2 * LANES
    hd = ATTN_HEAD_DIM
    grid_spec = pltpu.PrefetchScalarGridSpec(
        num_scalar_prefetch=1,
        grid=(bsz, n_pairs, nb),
        in_specs=[pl.BlockSpec((bs, qw), lambda b, h, i, s: (b * nb + i, q_col // qw + h)),
                  pl.BlockSpec((t, LANES), lambda b, h, i, s: (b, k_col // LANES + h)),
                  pl.BlockSpec((t, LANES), lambda b, h, i, s: (b, v_col // LANES + h))],
        out_specs=pl.BlockSpec((bs, qw), lambda b, h, i, s: (b * nb + i, h)),
        scratch_shapes=[pltpu.VMEM((nb, bs, LANES), BF16),
                        pltpu.VMEM((nb, LANES, bs), BF16),
                        pltpu.VMEM((nbp, LANES), F32),
                        pltpu.VMEM((4, nbp, bs), F32),
                        pltpu.VMEM((4, bs, bs), F32),
                        pltpu.VMEM((4, LANES, bs), BF16),
                        pltpu.VMEM((4, 1, bs), F32),
                        pltpu.VMEM((4, 1, bs), F32),
                        pltpu.VMEM((4, hd, bs), F32)])
    return pl.pallas_call(
        functools.partial(_attn_prompt_kernel, nb),
        out_shape=jax.ShapeDtypeStruct((n, n_kv * LANES), BF16),
        grid_spec=grid_spec,
        compiler_params=_cparams(3),
        name="attn_prompt",
    )(slopes, proj, proj, proj)


def _diag_extract(o, n_kv, rows_per_kv):
    hd = ATTN_HEAD_DIM
    tiles = []
    for k in range(n_kv):
        tl = o[k * rows_per_kv:(k + 1) * rows_per_kv, (k // 2) * LANES:(k // 2 + 1) * LANES]
        if k % 2 == 1:
            tl = pltpu.roll(tl, hd, 1)
        tiles.append(tl)
    return jnp.concatenate(tiles, axis=0)


def _attn_sample_kernel(bps, n_kv, tq, past_len, pt_ref, *refs):
    npg = 2 * bps
    kp = refs[0:npg]
    vp = refs[npg:2 * npg]
    qbd_ref, slope_ref, trow_ref, knew_ref, vnew_ref, o_ref, st_scr, km_scr, kn_scr, vn_scr = refs[2 * npg:]
    bs = MOBA_BLOCK
    hd = ATTN_HEAD_DIM
    rows = qbd_ref.shape[1]
    kvw = qbd_ref.shape[2]
    page = kp[0].shape[-1]
    rpk = rows // n_kv
    s_id = pl.program_id(1)
    n_steps = pl.num_programs(1)
    nblk = past_len // bs

    @pl.when(s_id == 0)
    def _():
        km_scr[...] = jnp.zeros(km_scr.shape, F32)

    qf = qbd_ref[0]
    qb = (qf * (hd ** -0.5)).astype(BF16)
    slope = slope_ref[...]
    lane = lax.broadcasted_iota(jnp.int32, (rows, LANES), 1)
    lane_k = lax.broadcasted_iota(jnp.int32, (kvw, LANES), 1)
    sc0 = slope * lane.astype(F32)
    sc1 = sc0 + slope * float(page)

    for bi in range(bps):
        j = s_id * bps + bi
        kt0 = kp[2 * bi][0].reshape(kvw, page)
        kt1 = kp[2 * bi + 1][0].reshape(kvw, page)
        vt0 = vp[2 * bi][0].reshape(kvw, page)
        vt1 = vp[2 * bi + 1][0].reshape(kvw, page)
        ksum = jnp.sum(kt0 + kt1, axis=1, keepdims=True) * (1.0 / bs)
        km_scr[...] = jnp.where(lane_k == j, ksum, km_scr[...])
        s0 = _dot(qb, kt0.astype(BF16)) + sc0
        s1 = _dot(qb, kt1.astype(BF16)) + sc1
        m = jnp.maximum(jnp.max(s0, axis=-1, keepdims=True), jnp.max(s1, axis=-1, keepdims=True))
        p0 = jnp.exp(s0 - m)
        p1 = jnp.exp(s1 - m)
        l = jnp.sum(p0, axis=-1, keepdims=True) + jnp.sum(p1, axis=-1, keepdims=True)
        o = _dot_nt(p0.astype(BF16), vt0.astype(BF16)) + _dot_nt(p1.astype(BF16), vt1.astype(BF16))
        od = _diag_extract(o, n_kv, rpk)
        st_scr[j] = jnp.where(lane < hd, od, jnp.where(lane == hd, m, l))

    @pl.when(s_id == n_steps - 1)
    def _():
        trow = trow_ref[...]
        gate = _dot_precise(qf, km_scr[...])
        gate = jnp.where(lane < nblk, gate, NEG_INF)
        sel, _ = _topk_mask(gate, lane, MOBA_TOPK)
        selneg = jnp.where(sel, 0.0, NEG_INF)

        kn_scr[...] = jnp.zeros(kn_scr.shape, F32)
        vn_scr[...] = jnp.zeros(vn_scr.shape, F32)
        kn_scr[0:tq, :] = knew_ref[...]
        vn_scr[0:tq, :] = vnew_ref[...]
        lanef = lane.astype(F32)
        so = _dot_nt(qb, kn_scr[...].astype(BF16)) - slope * (trow - lanef)
        so = jnp.where(jnp.logical_and(lanef <= trow, lane < tq), so, NEG_INF)
        m_run = jnp.max(so, axis=-1, keepdims=True)
        po = jnp.exp(so - m_run)
        den = jnp.sum(po, axis=-1, keepdims=True)
        num = _diag_extract(_dot(po.astype(BF16), vn_scr[...].astype(BF16)), n_kv, rpk)

        slope1 = slope[:, 0:1]
        base = -slope1 * (float(past_len) + trow[:, 0:1])

        def body(j, carry):
            m_run, num, den = carry
            tile = st_scr[j]
            selcol = jnp.max(jnp.where(lane == j, selneg, NEG_INF), axis=-1, keepdims=True)
            mj = tile[:, hd:hd + 1] + (base + slope1 * (j * bs).astype(F32)) + selcol
            lj = tile[:, hd + 1:hd + 2]
            m_new = jnp.maximum(m_run, mj)
            a = jnp.exp(m_run - m_new)
            b = jnp.exp(mj - m_new)
            return m_new, a * num + b * tile, a * den + b * lj

        m_run, num, den = lax.fori_loop(0, nblk, body, (m_run, num, den))
        o_ref[0] = num / den


def _attn_sample(cache_kt, cache_vt, layer, page_table, qbd, slope_rows, t_rows, k_new, v_new, tq, n_kv, past_len):
    nbat, rows, kvw = qbd.shape
    page = cache_kt.shape[4]
    hd = cache_kt.shape[3]
    bs = MOBA_BLOCK
    nblk = past_len // bs
    bps = 4
    while nblk % bps:
        bps //= 2
    n_steps = nblk // bps
    assert bs == 2 * page and nblk <= LANES
    npg = 2 * bps

    def page_spec(r):
        return pl.BlockSpec((None, 1, n_kv, hd, page), lambda b, s, pt: (layer, pt[b, s * npg + r], 0, 0, 0))

    in_specs = ([page_spec(r) for r in range(npg)] + [page_spec(r) for r in range(npg)]
                + [pl.BlockSpec((1, rows, kvw), lambda b, s, pt: (b, 0, 0)),
                   pl.BlockSpec((rows, LANES), lambda b, s, pt: (0, 0)),
                   pl.BlockSpec((rows, LANES), lambda b, s, pt: (0, 0)),
                   pl.BlockSpec((tq, kvw), lambda b, s, pt: (b, 0)),
                   pl.BlockSpec((tq, kvw), lambda b, s, pt: (b, 0))])
    grid_spec = pltpu.PrefetchScalarGridSpec(
        num_scalar_prefetch=1,
        grid=(nbat, n_steps),
        in_specs=in_specs,
        out_specs=pl.BlockSpec((1, rows, LANES), lambda b, s, pt: (b, 0, 0)),
        scratch_shapes=[pltpu.VMEM((nblk, rows, LANES), F32),
                        pltpu.VMEM((kvw, LANES), F32),
                        pltpu.VMEM((LANES, kvw), F32),
                        pltpu.VMEM((LANES, kvw), F32)])
    return pl.pallas_call(
        functools.partial(_attn_sample_kernel, bps, n_kv, tq, past_len),
        out_shape=jax.ShapeDtypeStruct((nbat, rows, LANES), F32),
        grid_spec=grid_spec,
        compiler_params=_cparams(2),
        name="attn_sample",
    )(page_table, *([cache_kt] * npg), *([cache_vt] * npg), qbd, slope_rows, t_rows, k_new, v_new)


def _outproj_kernel(alpha, n_exp, ys_ref, ya_ref, gs_ref, ga_ref, x_ref, gm_ref, scf_ref, shf_ref,
                    wos_ref, woa_ref, wout_ref, g1_ref, b1_ref, wr_ref, br_ref,
                    x1_ref, hf_ref, gates_ref):
    merged = (jax.nn.sigmoid(gs_ref[...]) * _dot(ys_ref[...].astype(BF16), wos_ref[...])
              + jax.nn.sigmoid(ga_ref[...]) * _dot(ya_ref[...], woa_ref[...]))
    upd = _dot(merged.astype(BF16), wout_ref[...])
    x1 = _layer_norm(alpha * x_ref[...] + gm_ref[0] * upd, g1_ref[...], b1_ref[...])
    x1_ref[...] = x1
    hf = x1 * scf_ref[0] + shf_ref[0]
    hf_ref[...] = hf.astype(BF16)
    logits = _dot_precise(hf, wr_ref[...]) + br_ref[...]
    lane = lax.broadcasted_iota(jnp.int32, logits.shape, 1)
    logits = jnp.where(lane < n_exp, logits, NEG_INF)
    g = logits
    picks, vals = [], []
    for _ in range(TOP_K):
        m = jnp.max(g, axis=-1, keepdims=True)
        idx = jnp.min(jnp.where(g == m, lane, LANES), axis=-1, keepdims=True)
        pick = lane == idx
        g = jnp.where(pick, NEG_INF, g)
        picks.append(pick)
        vals.append(m)
    es = [jnp.exp(v - vals[0]) for v in vals]
    inv = 1.0 / (es[0] + es[1] + es[2] + es[3])
    gates = jnp.zeros(logits.shape, F32)
    for pick, e in zip(picks, es):
        gates = gates + jnp.where(pick, e * inv, 0.0)
    gates_ref[...] = gates


def _outproj(alpha, n_exp, ys, ya, proj, x, gm, scf1, shf, wos, woa, wout, g1, b1, wr, br, tm, tiles_per_group,
             gs_blk, ga_blk):
    n, d = x.shape
    r = gm.shape[1]
    grp = lambda i: (i // tiles_per_group, 0, 0)
    const = lambda i: (0, 0)
    return pl.pallas_call(
        functools.partial(_outproj_kernel, alpha, n_exp),
        out_shape=(jax.ShapeDtypeStruct((n, d), F32),
                   jax.ShapeDtypeStruct((n, d), BF16),
                   jax.ShapeDtypeStruct((n, LANES), F32)),
        grid=(n // tm,),
        in_specs=[pl.BlockSpec((tm, ys.shape[1]), lambda i: (i, 0)),
                  pl.BlockSpec((tm, ya.shape[1]), lambda i: (i, 0)),
                  pl.BlockSpec((tm, d), lambda i: (i, gs_blk)),
                  pl.BlockSpec((tm, d), lambda i: (i, ga_blk)),
                  pl.BlockSpec((tm, d), lambda i: (i, 0)),
                  pl.BlockSpec((1, r, d), grp),
                  pl.BlockSpec((1, r, d), grp),
                  pl.BlockSpec((1, r, d), grp),
                  pl.BlockSpec(wos.shape, const),
                  pl.BlockSpec(woa.shape, const),
                  pl.BlockSpec(wout.shape, const),
                  pl.BlockSpec((1, d), const),
                  pl.BlockSpec((1, d), const),
                  pl.BlockSpec((d, LANES), const),
                  pl.BlockSpec((1, LANES), const)],
        out_specs=(pl.BlockSpec((tm, d), lambda i: (i, 0)),
                   pl.BlockSpec((tm, d), lambda i: (i, 0)),
                   pl.BlockSpec((tm, LANES), lambda i: (i, 0))),
        compiler_params=_cparams(1),
        name="outproj",
    )(ys, ya, proj, proj, x, gm, scf1, shf, wos, woa, wout, g1, b1, wr, br)


def _moe_kernel(alpha, d_ff, hf_ref, gates_ref, x1_ref, gf_ref, wgu_ref, bgu_ref, wdn_ref, bdn_ref,
                g2_ref, b2_ref, o_ref, acc_scr):
    e = pl.program_id(1)

    @pl.when(e == 0)
    def _():
        acc_scr[...] = jnp.zeros(acc_scr.shape, F32)

    gu = _dot(hf_ref[...], wgu_ref[0]) + bgu_ref[0]
    gg = jnp.minimum(gu[:, :d_ff], SWIGLU_LIMIT)
    uu = jnp.clip(gu[:, d_ff:], -SWIGLU_LIMIT, SWIGLU_LIMIT)
    act = (uu + 1.0) * (gg * jax.nn.sigmoid(SWIGLU_ALPHA * gg))
    y = _dot(act.astype(BF16), wdn_ref[0]) + bdn_ref[0]
    gates = gates_ref[...]
    lane = lax.broadcasted_iota(jnp.int32, gates.shape, 1)
    gcol = jnp.sum(jnp.where(lane == e, gates, 0.0), axis=-1, keepdims=True)
    acc_scr[...] += gcol * y

    @pl.when(e == pl.num_programs(1) - 1)
    def _():
        o_ref[...] = _layer_norm(alpha * x1_ref[...] + gf_ref[0] * acc_scr[...], g2_ref[...], b2_ref[...])


def _moe(alpha, hf, gates, x1, gf, wgu, bgu, wdn, bdn, g2, b2, tm, tiles_per_group):
    n, d = x1.shape
    n_exp, _, two_ff = wgu.shape
    d_ff = two_ff // 2
    r = gf.shape[1]
    return pl.pallas_call(
        functools.partial(_moe_kernel, alpha, d_ff),
        out_shape=jax.ShapeDtypeStruct((n, d), F32),
        grid=(n // tm, n_exp),
        in_specs=[pl.BlockSpec((tm, d), lambda i, e: (i, 0)),
                  pl.BlockSpec((tm, LANES), lambda i, e: (i, 0)),
                  pl.BlockSpec((tm, d), lambda i, e: (i, 0)),
                  pl.BlockSpec((1, r, d), lambda i, e: (i // tiles_per_group, 0, 0)),
                  pl.BlockSpec((1, d, two_ff), lambda i, e: (e, 0, 0)),
                  pl.BlockSpec((1, 1, two_ff), lambda i, e: (e, 0, 0)),
                  pl.BlockSpec((1, d_ff, d), lambda i, e: (e, 0, 0)),
                  pl.BlockSpec((1, 1, d), lambda i, e: (e, 0, 0)),
                  pl.BlockSpec((1, d), lambda i, e: (0, 0)),
                  pl.BlockSpec((1, d), lambda i, e: (0, 0))],
        out_specs=pl.BlockSpec((tm, d), lambda i, e: (i, 0)),
        scratch_shapes=[pltpu.VMEM((tm, d), F32)],
        compiler_params=_cparams(2),
        name="moe",
    )(hf, gates, x1, gf, wgu, bgu, wdn, bdn, g2, b2)


def _pad_lanes(v, fill=0.0):
    return jnp.pad(v, (0, LANES - v.shape[0]), constant_values=fill).reshape(1, LANES)


def kernel(x_prompt, x_sample, c_prompt, c_sample, cache_k, cache_v, state_ssm, state_conv, page_table, w_ada, b_ada, w_in, conv_w, conv_b, dt_bias, a_log, d_skip, ssm_norm_g, w_o_ssm, w_o_attn, w_out, ln1_g, ln1_b, ln2_g, ln2_b, w_router, b_router, w_gu, b_gu, w_down, b_down):
    depth, d, _ = w_ada.shape
    bp, tp, _ = x_prompt.shape
    bsm, ts, _ = x_sample.shape
    n_heads = dt_bias.shape[1]
    d_inner = n_heads * SSM_HEAD_DIM
    conv_ch = conv_w.shape[2]
    kvw = cache_k.shape[3] * cache_k.shape[4]
    n_kv = cache_k.shape[3]
    aw = w_o_attn.shape[1]
    n_q = aw // ATTN_HEAD_DIM
    n_exp = w_router.shape[2]
    page = cache_k.shape[2]
    past_len = page_table.shape[1] * page
    alpha = float((2.0 * depth) ** 0.25)
    assert d_inner == 2 * d and aw == d and kvw == d // 2 and n_q == 2 * n_kv
    assert conv_ch == d_inner + 2 * SSM_GROUPS * SSM_STATE and conv_ch == 3 * d
    assert tp % MOBA_BLOCK == 0 and tp % SSD_CHUNK == 0 and ts == SUBLANES and n_heads <= LANES

    o_z, o_xbc, o_dt = 0, d_inner, d_inner + conv_ch
    o_q = o_dt + n_heads
    o_k, o_v = o_q + aw, o_q + aw + kvw
    o_gs, o_ga = o_v + kvw, o_v + kvw + d
    c_z, c_q, c_gs, c_ga = 0, d_inner, d_inner + d, d_inner + 2 * d
    c_k = d_inner + 3 * d
    c_v = c_k + kvw
    c_xbc = c_v + kvw
    assert c_xbc % conv_ch == 0

    slopes = 2.0 ** (-8.0 * (jnp.arange(n_q, dtype=F32) + 1.0) / n_q)

    n_c = bsm + bp
    n_c_pad = -(-n_c // SUBLANES) * SUBLANES
    c_all = jnp.concatenate([c_sample, c_prompt, jnp.zeros((n_c_pad - n_c, d), F32)], axis=0)
    mod = _ada(c_all, w_ada, b_ada)

    grp = n_q // n_kv
    rows = n_kv * grp * ts
    r_idx = jnp.arange(rows)
    r_head = (r_idx // (grp * ts)) * grp + (r_idx // ts) % grp
    slope_rows = jnp.broadcast_to(slopes[r_head][:, None], (rows, LANES))
    t_rows = jnp.broadcast_to((r_idx % ts).astype(F32)[:, None], (rows, LANES))
    eye_kv = jnp.eye(n_kv, dtype=F32)

    cache_kt = jnp.transpose(cache_k, (0, 1, 3, 4, 2))
    cache_vt = jnp.transpose(cache_v, (0, 1, 3, 4, 2))

    xp = x_prompt.reshape(bp * tp, d)
    xs = x_sample.reshape(bsm * ts, d)
    tm_p = 1024 if tp % 1024 == 0 else MOBA_BLOCK
    tm_o = 512 if tp % 512 == 0 else MOBA_BLOCK
    n_s = bsm * ts

    outs = {k: [] for k in ("kp", "vp", "hp", "cvp", "ks", "vs", "hs", "cvs")}
    for l in range(depth):
        wl = w_in[l]
        w_main = jnp.concatenate([wl[:, o_z:o_z + d_inner], wl[:, o_q:o_q + aw], wl[:, o_gs:o_gs + d],
                                  wl[:, o_ga:o_ga + d], wl[:, o_k:o_k + kvw], wl[:, o_v:o_v + kvw],
                                  wl[:, o_xbc:o_xbc + conv_ch]], axis=1).astype(BF16)
        w_dt = jnp.pad(wl[:, o_dt:o_dt + n_heads], ((0, 0), (0, LANES - n_heads))).astype(BF16)
        w_dtT = wl[:, o_dt:o_dt + n_heads].T.astype(BF16)
        dtb = _pad_lanes(dt_bias[l])
        dtbT = jnp.broadcast_to(dt_bias[l][:, None], (n_heads, SSD_CHUNK))
        a_neg = -jnp.exp(a_log[l])
        a_row = _pad_lanes(a_neg)
        aT = jnp.broadcast_to(a_neg[:, None], (n_heads, SSD_CHUNK))
        dskip = jnp.repeat(d_skip[l], SSM_HEAD_DIM).reshape(1, d_inner)
        ng = ssm_norm_g[l].reshape(1, d_inner)
        cw = conv_w[l]
        cb = conv_b[l].reshape(1, conv_ch)
        wos = w_o_ssm[l].astype(BF16)
        woa = w_o_attn[l].astype(BF16)
        wout = w_out[l].astype(BF16)
        g1, b1 = ln1_g[l].reshape(1, d), ln1_b[l].reshape(1, d)
        g2, b2 = ln2_g[l].reshape(1, d), ln2_b[l].reshape(1, d)
        wr = jnp.pad(w_router[l], ((0, 0), (0, LANES - n_exp)))
        br = _pad_lanes(b_router[l])
        wgu = w_gu[l].astype(BF16)
        bgu = b_gu[l].reshape(n_exp, 1, -1)
        wdn = w_down[l].astype(BF16)
        bdn = b_down[l].reshape(n_exp, 1, d)

        mod_l = mod[l]
        sh_m, sc_m, g_m, sh_f, sc_f, g_f = [mod_l[:, i * d:(i + 1) * d] for i in range(6)]

        def group_mod(v, is_prompt):
            if is_prompt:
                return v[bsm:bsm + bp].reshape(bp, 1, d)
            return jnp.repeat(v[:bsm], ts, axis=0).reshape(1, n_s, d)

        def run(x, is_prompt, h0T, cprev8):
            bsz, t = (bp, tp) if is_prompt else (bsm, ts)
            n = bsz * t
            tm = tm_p if is_prompt else n
            tmo = tm_o if is_prompt else n
            tpg = (t // tm) if is_prompt else 1
            tpgo = (t // tmo) if is_prompt else 1
            gm = lambda v: group_mod(v, is_prompt)
            proj, dt, dtT = _inproj(x, gm(1.0 + sc_m), gm(sh_m), w_main, w_dt, w_dtT, tm, tpg)
            dtT_seq = dtT.reshape(n_heads, bsz, t).transpose(1, 0, 2)
            lr = SSD_CHUNK if is_prompt else t
            y_s, h_newT = _ssd(proj, dt, dtT_seq, h0T, cprev8, cw, cb, dtb, dtbT, a_row, aT, dskip, ng,
                               bsz, t, lr, c_z // d_inner, c_xbc // conv_ch, d_inner, conv_ch)
            k_new = proj[:, c_k:c_k + kvw]
            v_new = proj[:, c_v:c_v + kvw]
            if is_prompt:
                y_a = _attn_prompt(proj, slopes, bsz, t, n_kv, c_q, c_k, c_v)
            else:
                q5 = proj[:, c_q:c_q + aw].reshape(bsz, t, n_kv, grp, ATTN_HEAD_DIM)
                q5 = q5.transpose(0, 2, 3, 1, 4).reshape(bsz, n_kv, grp * t, ATTN_HEAD_DIM)
                qbd = (q5[:, :, :, None, :] * eye_kv[None, :, None, :, None]).reshape(bsz, rows, kvw)
                o = _attn_sample(cache_kt, cache_vt, l, page_table, qbd, slope_rows, t_rows, k_new, v_new,
                                 t, n_kv, past_len)
                o = o[:, :, :ATTN_HEAD_DIM].reshape(bsz, n_kv, grp, t, ATTN_HEAD_DIM)
                y_a = o.transpose(0, 3, 1, 2, 4).reshape(n, aw).astype(BF16)
            x1, hf, gates = _outproj(alpha, n_exp, y_s, y_a, proj, x, gm(g_m), gm(1.0 + sc_f), gm(sh_f),
                                     wos, woa, wout, g1, b1, wr, br, tmo, tpgo, c_gs // d, c_ga // d)
            x2 = _moe(alpha, hf, gates, x1, gm(g_f), wgu, bgu, wdn, bdn, g2, b2, tmo, tpgo)
            conv_new = proj[:, c_xbc:c_xbc + conv_ch].reshape(bsz, t, conv_ch)[:, t - (CONV_W - 1):]
            h_new = h_newT.transpose(0, 2, 1).reshape(bsz, n_heads, SSM_HEAD_DIM, SSM_STATE)
            hd4 = (bsz, t, n_kv, ATTN_HEAD_DIM)
            return x2, k_new.reshape(hd4), v_new.reshape(hd4), h_new, conv_new

        h0_p = jnp.zeros((bp, SSM_STATE, d_inner), F32)
        cv0_p = jnp.zeros((bp, SUBLANES, conv_ch), F32)
        xp, k1, v1, h1, cv1 = run(xp, True, h0_p, cv0_p)
        outs["kp"].append(k1); outs["vp"].append(v1); outs["hp"].append(h1); outs["cvp"].append(cv1)

        h0_s = state_ssm[l].reshape(bsm, d_inner, SSM_STATE).transpose(0, 2, 1)
        cv0_s = jnp.pad(state_conv[l], ((0, 0), (SUBLANES - (CONV_W - 1), 0), (0, 0)))
        xs, k1, v1, h1, cv1 = run(xs, False, h0_s, cv0_s)
        outs["ks"].append(k1); outs["vs"].append(v1); outs["hs"].append(h1); outs["cvs"].append(cv1)

    st = lambda k: jnp.stack(outs[k])
    return (xp.reshape(bp, tp, d), xs.reshape(bsm, ts, d), st("kp"), st("vp"), st("hp"), st("cvp"),
            st("ks"), st("vs"), st("hs"), st("cvs"))
```

```python
import functools
import math

import jax
import jax.numpy as jnp
from jax import lax
from jax.experimental import pallas as pl
from jax.experimental.pallas import tpu as pltpu
from jax.experimental.pallas import tpu_sc as plsc

F32 = jnp.float32
BF16 = jnp.bfloat16
NEG_INF = float("-inf")

LANES = 128
SUBLANES = 8
VMEM_LIMIT = 56 * 1024 * 1024
SC_CORES = 2
SC_SUBCORES = 16
SC_GATHER_ROWS = 64
MOE_TILE = 512

SSM_HEAD_DIM = 64
SSM_GROUPS = 4
SSM_STATE = 128
CONV_W = 4
SSD_CHUNK = 128
RMS_EPS = 1e-5
ATTN_HEAD_DIM = 64
MOBA_BLOCK = 256
MOBA_TOPK = 3
TOP_K = 4
SWIGLU_LIMIT = 7.0
SWIGLU_ALPHA = 1.702
LN_EPS = 1e-5


def _cparams(n_axes):
    return pltpu.CompilerParams(dimension_semantics=("arbitrary",) * n_axes,
                                vmem_limit_bytes=VMEM_LIMIT)


def _dot(a, b):
    return jnp.dot(a, b, preferred_element_type=F32)


def _dot_nt(a, b):
    return lax.dot_general(a, b, (((1,), (1,)), ((), ())), preferred_element_type=F32)


def _split2(a):
    hi = a.astype(BF16)
    lo = (a - hi.astype(F32)).astype(BF16)
    return hi, lo


def _split3(a):
    hi = a.astype(BF16)
    r = a - hi.astype(F32)
    mid = r.astype(BF16)
    lo = (r - mid.astype(F32)).astype(BF16)
    return hi, mid, lo


def _dot_nt_precise(a, b):
    ah, al = _split2(a)
    bh, bl = _split2(b)
    return _dot_nt(ah, bh) + (_dot_nt(ah, bl) + _dot_nt(al, bh))


def _dot_precise(a, b):
    ah, al = _split2(a)
    bh, bl = _split2(b)
    return _dot(ah, bh) + (_dot(ah, bl) + _dot(al, bh))


def _silu(x):
    return x * jax.nn.sigmoid(x)


def _softplus(x):
    return jnp.maximum(x, 0.0) + jnp.log1p(jnp.exp(-jnp.abs(x)))


def _layer_norm(x, g, b):
    mu = jnp.mean(x, axis=-1, keepdims=True)
    xc = x - mu
    var = jnp.mean(xc * xc, axis=-1, keepdims=True)
    return xc * lax.rsqrt(var + LN_EPS) * g + b


def _topk_mask(score, lane, k):
    width = score.shape[-1]
    sel = jnp.zeros(score.shape, dtype=jnp.bool_)
    vals = []
    g = score
    for _ in range(k):
        m = jnp.max(g, axis=-1, keepdims=True)
        idx = jnp.min(jnp.where(g == m, lane, width), axis=-1, keepdims=True)
        pick = jnp.logical_and(lane == idx, m > NEG_INF)
        sel = jnp.logical_or(sel, pick)
        g = jnp.where(pick, NEG_INF, g)
        vals.append(m)
    return sel, vals


def _ada_kernel(c_ref, w_ref, b_ref, o_ref):
    s = _silu(c_ref[...]).astype(BF16)
    o_ref[0] = _dot(s, w_ref[0].astype(BF16)) + b_ref[0]


def _ada(c_all, w_ada, b_ada):
    depth, d, n6 = w_ada.shape
    rows = c_all.shape[0]
    tn = 1024
    return pl.pallas_call(
        _ada_kernel,
        out_shape=jax.ShapeDtypeStruct((depth, rows, n6), F32),
        grid=(depth, n6 // tn),
        in_specs=[pl.BlockSpec((rows, d), lambda l, j: (0, 0)),
                  pl.BlockSpec((1, d, tn), lambda l, j: (l, 0, j)),
                  pl.BlockSpec((1, 1, tn), lambda l, j: (l, 0, j))],
        out_specs=pl.BlockSpec((1, rows, tn), lambda l, j: (l, 0, j)),
        compiler_params=_cparams(2),
        name="ada",
    )(c_all, w_ada, b_ada.reshape(depth, 1, n6))


def _inproj_kernel(x_ref, sc_ref, sh_ref, w_ref, wdt_ref, wdtT_ref, o_ref, dt_ref, dtT_ref, h_scr):
    @pl.when(pl.program_id(1) == 0)
    def _():
        hb = (x_ref[...] * sc_ref[0] + sh_ref[0]).astype(BF16)
        h_scr[...] = hb
        dt_ref[...] = _dot(hb, wdt_ref[...])
        dtT_ref[...] = _dot_nt(wdtT_ref[...], hb)

    o_ref[...] = _dot(h_scr[...], w_ref[...])


def _inproj(x, sc1, sh, w_main, w_dt, w_dtT, tm, tiles_per_group):
    n, d = x.shape
    n_main = w_main.shape[1]
    tn = 1024
    r = sc1.shape[1]
    nh = w_dtT.shape[0]
    grp = lambda i, j: (i // tiles_per_group, 0, 0)
    return pl.pallas_call(
        _inproj_kernel,
        out_shape=(jax.ShapeDtypeStruct((n, n_main), F32),
                   jax.ShapeDtypeStruct((n, LANES), F32),
                   jax.ShapeDtypeStruct((nh, n), F32)),
        grid=(n // tm, n_main // tn),
        in_specs=[pl.BlockSpec((tm, d), lambda i, j: (i, 0)),
                  pl.BlockSpec((1, r, d), grp),
                  pl.BlockSpec((1, r, d), grp),
                  pl.BlockSpec((d, tn), lambda i, j: (0, j)),
                  pl.BlockSpec((d, LANES), lambda i, j: (0, 0)),
                  pl.BlockSpec((nh, d), lambda i, j: (0, 0))],
        out_specs=(pl.BlockSpec((tm, tn), lambda i, j: (i, j)),
                   pl.BlockSpec((tm, LANES), lambda i, j: (i, 0)),
                   pl.BlockSpec((nh, tm), lambda i, j: (0, i))),
        scratch_shapes=[pltpu.VMEM((tm, d), BF16)],
        compiler_params=_cparams(2),
        name="inproj",
    )(x, sc1, sh, w_main, w_dt, w_dtT)


def _ssd_kernel(lr, d_inner, xbc_ref, z_ref, dt_ref, dtT_ref, h0_ref, cprev_ref, cw_ref, cb_ref,
                dtb_ref, dtbT_ref, a_ref, aT_ref, dskip_ref, ng_ref,
                y_ref, hout_ref, xp_scr, h_scr, y_scr, dt_scr, dtT_scr):
    L = SSD_CHUNK
    gs = SSM_GROUPS * SSM_STATE
    n_heads = d_inner // SSM_HEAD_DIM
    pairs_per_group = n_heads // SSM_GROUPS // 2
    c = pl.program_id(1)

    @pl.when(c == 0)
    def _():
        xp_scr[0:SUBLANES, :] = cprev_ref[0]
        h_scr[...] = h0_ref[0]

    if lr < L:
        xp_scr[SUBLANES + lr:SUBLANES + L, :] = jnp.zeros((L - lr, xp_scr.shape[1]), F32)
    xp_scr[SUBLANES:SUBLANES + lr, :] = xbc_ref[...]

    acc = cb_ref[...] + cw_ref[0:1, :] * xp_scr[SUBLANES - 3:SUBLANES - 3 + L, :]
    for j in range(1, CONV_W):
        acc = acc + cw_ref[j:j + 1, :] * xp_scr[SUBLANES - 3 + j:SUBLANES - 3 + j + L, :]
    tail = xp_scr[lr:lr + SUBLANES, :]
    xp_scr[0:SUBLANES, :] = tail
    xc = _silu(acc)

    if lr < L:
        dt_scr[...] = jnp.zeros(dt_scr.shape, F32)
        dt_scr[0:lr, :] = dt_ref[...]
        dtT_scr[...] = jnp.zeros(dtT_scr.shape, F32)
        dtT_scr[:, 0:lr] = dtT_ref[0]
        dt_raw = dt_scr[...]
        dtT_raw = dtT_scr[...]
    else:
        dt_raw = dt_ref[...]
        dtT_raw = dtT_ref[0]
    row = lax.broadcasted_iota(jnp.int32, (L, LANES), 0)
    col = lax.broadcasted_iota(jnp.int32, (n_heads, L), 1)
    dtv = jnp.where(row < lr, _softplus(dt_raw + dtb_ref[...]), 0.0)
    dtvT = jnp.where(col < lr, _softplus(dtT_raw + dtbT_ref[...]), 0.0)
    dA = dtv * a_ref[...]
    dAT = dtvT * aT_ref[...]

    ri = lax.broadcasted_iota(jnp.int32, (L, L), 0)
    ci = lax.broadcasted_iota(jnp.int32, (L, L), 1)
    causal = ri >= ci
    tri = jnp.where(causal, 1.0, 0.0).astype(BF16)
    triT = jnp.where(ri <= ci, 1.0, 0.0).astype(BF16)
    d1, d2, d3 = _split3(dA)
    la = _dot(tri, d1) + (_dot(tri, d2) + _dot(tri, d3))
    e1, e2, e3 = _split3(dAT)
    laT = _dot(e1, triT) + (_dot(e2, triT) + _dot(e3, triT))

    la_last = la[L - 1:L, :]
    to_end = jnp.exp(la_last - la)
    ela = jnp.exp(la)
    cdec = jnp.exp(la_last)

    lane = lax.broadcasted_iota(jnp.int32, (L, LANES), 1)
    lo_half = lane < SSM_HEAD_DIM
    lane1 = lax.broadcasted_iota(jnp.int32, (1, LANES), 1)
    eye = jnp.where(ri == ci, 1.0, 0.0).astype(BF16)

    def pair_bcast(arr, p):
        rows = arr.shape[0]
        a0 = jnp.broadcast_to(arr[:, 2 * p:2 * p + 1], (rows, LANES))
        a1 = jnp.broadcast_to(arr[:, 2 * p + 1:2 * p + 2], (rows, LANES))
        return jnp.where(lo_half[:rows], a0, a1)

    for g in range(SSM_GROUPS):
        bg = xc[:, d_inner + g * SSM_STATE:d_inner + (g + 1) * SSM_STATE].astype(BF16)
        cg = xc[:, d_inner + gs + g * SSM_STATE:d_inner + gs + (g + 1) * SSM_STATE].astype(BF16)
        cb = _dot_nt(cg, bg)
        bgT = _dot_nt(eye, bg).astype(BF16)
        for pp in range(pairs_per_group):
            p = g * pairs_per_group + pp
            sl = slice(p * LANES, (p + 1) * LANES)
            xs_p = xc[:, sl]
            xdt_p = xs_p * pair_bcast(dtv, p)
            ydiag = None
            for a in range(2):
                h = 2 * p + a
                seg = jnp.broadcast_to(la[:, h:h + 1], (L, L)) - laT[h:h + 1, :]
                decay = jnp.exp(jnp.where(causal, seg, NEG_INF))
                m = (cb * decay).astype(BF16)
                half = lo_half if a == 0 else jnp.logical_not(lo_half)
                part = _dot(m, jnp.where(half, xdt_p, 0.0).astype(BF16))
                ydiag = part if ydiag is None else ydiag + part
            hin = h_scr[:, sl]
            yoff = _dot(cg, hin.astype(BF16)) * pair_bcast(ela, p)
            y_scr[:, sl] = ydiag + yoff + dskip_ref[:, sl] * xs_p
            s_new = _dot(bgT, (xdt_p * pair_bcast(to_end, p)).astype(BF16))
            cd_p = jnp.where(lane1 < SSM_HEAD_DIM,
                             jnp.broadcast_to(cdec[:, 2 * p:2 * p + 1], (1, LANES)),
                             jnp.broadcast_to(cdec[:, 2 * p + 1:2 * p + 2], (1, LANES)))
            h_scr[:, sl] = hin * cd_p + s_new

    gw = d_inner // SSM_GROUPS
    for g in range(SSM_GROUPS):
        sl = slice(g * gw, (g + 1) * gw)
        yg = y_scr[0:lr, sl] * _silu(z_ref[:, sl])
        ms = jnp.mean(yg * yg, axis=-1, keepdims=True)
        y_ref[:, sl] = (yg * lax.rsqrt(ms + RMS_EPS) * ng_ref[:, sl]).astype(y_ref.dtype)

    @pl.when(c == pl.num_programs(1) - 1)
    def _():
        hout_ref[0] = h_scr[...]


def _ssd(proj, dt, dtT_seq, h0T, cprev8, cw, cb, dtb, dtbT, a, aT, dskip, ng, bsz, t, lr,
         z_blk, xbc_blk, d_inner, conv_ch):
    n = proj.shape[0]
    nc = t // lr
    n_heads = d_inner // SSM_HEAD_DIM
    L = SSD_CHUNK
    const2 = lambda b, c: (0, 0)
    return pl.pallas_call(
        functools.partial(_ssd_kernel, lr, d_inner),
        out_shape=(jax.ShapeDtypeStruct((n, d_inner), BF16 if lr % 16 == 0 else F32),
                   jax.ShapeDtypeStruct((bsz, SSM_STATE, d_inner), F32)),
        grid=(bsz, nc),
        in_specs=[pl.BlockSpec((lr, conv_ch), lambda b, c: (b * nc + c, xbc_blk)),
                  pl.BlockSpec((lr, d_inner), lambda b, c: (b * nc + c, z_blk)),
                  pl.BlockSpec((lr, LANES), lambda b, c: (b * nc + c, 0)),
                  pl.BlockSpec((1, n_heads, lr), lambda b, c: (b, 0, c)),
                  pl.BlockSpec((1, SSM_STATE, d_inner), lambda b, c: (b, 0, 0)),
                  pl.BlockSpec((1, SUBLANES, conv_ch), lambda b, c: (b, 0, 0)),
                  pl.BlockSpec((CONV_W, conv_ch), const2),
                  pl.BlockSpec((1, conv_ch), const2),
                  pl.BlockSpec((1, LANES), const2),
                  pl.BlockSpec((n_heads, L), const2),
                  pl.BlockSpec((1, LANES), const2),
                  pl.BlockSpec((n_heads, L), const2),
                  pl.BlockSpec((1, d_inner), const2),
                  pl.BlockSpec((1, d_inner), const2)],
        out_specs=(pl.BlockSpec((lr, d_inner), lambda b, c: (b * nc + c, 0)),
                   pl.BlockSpec((1, SSM_STATE, d_inner), lambda b, c: (b, 0, 0))),
        scratch_shapes=[pltpu.VMEM((SUBLANES + L, conv_ch), F32),
                        pltpu.VMEM((SSM_STATE, d_inner), F32),
                        pltpu.VMEM((L, d_inner), F32),
                        pltpu.VMEM((L, LANES), F32),
                        pltpu.VMEM((n_heads, L), F32)],
        compiler_params=_cparams(2),
        name="ssd",
    )(proj, proj, dt, dtT_seq, h0T, cprev8, cw, cb, dtb, dtbT, a, aT, dskip, ng)


def _topk_mask_rows(score, row, k):
    big = score.shape[0]
    sel = jnp.zeros(score.shape, dtype=jnp.bool_)
    g = score
    for _ in range(k):
        m = jnp.max(g, axis=0, keepdims=True)
        idx = jnp.min(jnp.where(g == m, row, big), axis=0, keepdims=True)
        pick = jnp.logical_and(row == idx, m > NEG_INF)
        sel = jnp.logical_or(sel, pick)
        g = jnp.where(pick, NEG_INF, g)
    return sel


def _attn_prompt_kernel(nb, slopes_ref, q_ref, k_ref, v_ref, o_ref,
                        kb_scr, vt_scr, km_scr, sel_scr, bias_scr, qt_scr, m_scr, l_scr, acc_scr):
    bs = MOBA_BLOCK
    hd = ATTN_HEAD_DIM
    pr = pl.program_id(1)
    qi = pl.program_id(2)
    nbp = km_scr.shape[0]
    n_ch = 4

    @pl.when(qi == 0)
    def _():
        km_scr[...] = jnp.zeros(km_scr.shape, F32)
        for i in range(nb):
            kblk = k_ref[i * bs:(i + 1) * bs, :]
            kb_scr[i] = kblk.astype(BF16)
            vt_scr[i] = v_ref[i * bs:(i + 1) * bs, :].T.astype(BF16)
            km_scr[i:i + 1, :] = jnp.sum(kblk, axis=0, keepdims=True) * (1.0 / bs)

    lane = lax.broadcasted_iota(jnp.int32, (bs, LANES), 1)
    qts = []
    for par in range(2):
        qp = q_ref[:, par * LANES:(par + 1) * LANES]
        qr = pltpu.roll(qp, hd, 1)
        mine = (lane >= hd) if par == 1 else (lane < hd)
        q_g0 = jnp.where(mine, qp if par == 0 else qr, 0.0)
        q_g1 = jnp.where(mine, qr if par == 0 else qp, 0.0)
        qts += [q_g0.T, q_g1.T]

    rowb = lax.broadcasted_iota(jnp.int32, (nbp, bs), 0)
    rk = lax.broadcasted_iota(jnp.int32, (bs, bs), 0)
    cq = lax.broadcasted_iota(jnp.int32, (bs, bs), 1)
    rel = (cq - rk).astype(F32)
    km = km_scr[...]
    slopes = [slopes_ref[n_ch * pr + c] for c in range(n_ch)]
    kd = kb_scr[qi]
    for c in range(n_ch):
        par = c // 2
        gate = _dot_precise(km, qts[c])
        gate = jnp.where(rowb < qi, gate, NEG_INF)
        sel_scr[c] = jnp.where(_topk_mask_rows(gate, rowb, MOBA_TOPK), 0.0, NEG_INF)
        bias = -slopes[c] * rel
        bias_scr[c] = bias
        qtb = (qts[c] * (hd ** -0.5)).astype(BF16)
        qt_scr[c] = qtb
        s = _dot(kd, qtb) + bias
        s = jnp.where(rel >= 0.0, s, NEG_INF)
        m0 = jnp.max(s, axis=0, keepdims=True)
        p = jnp.exp(s - m0)
        m_scr[c] = m0
        l_scr[c] = jnp.sum(p, axis=0, keepdims=True)
        acc_scr[c] = _dot(vt_scr[qi, par * hd:(par + 1) * hd, :], p.astype(BF16))

    def body(j, carry):
        kj = kb_scr[j]
        dist0 = jnp.asarray((qi - j) * bs, dtype=F32)
        ss = [_dot(kj, qt_scr[c]) for c in range(n_ch)]
        for c in range(n_ch):
            par = c // 2
            colb = sel_scr[c, pl.ds(j, 1), :] - slopes[c] * dist0
            sj = ss[c] + bias_scr[c] + colb
            m_old = m_scr[c]
            m_new = jnp.maximum(m_old, jnp.max(sj, axis=0, keepdims=True))
            alpha = jnp.exp(m_old - m_new)
            pj = jnp.exp(sj - m_new)
            l_scr[c] = alpha * l_scr[c] + jnp.sum(pj, axis=0, keepdims=True)
            pv = _dot(vt_scr[j, par * hd:(par + 1) * hd, :], pj.astype(BF16))
            acc_scr[c] = alpha * acc_scr[c] + pv
            m_scr[c] = m_new
        return carry

    lax.fori_loop(0, qi, body, 0)

    for par in range(2):
        o_g0 = acc_scr[2 * par] / l_scr[2 * par]
        o_g1 = acc_scr[2 * par + 1] / l_scr[2 * par + 1]
        o_ref[:, par * LANES:(par + 1) * LANES] = jnp.concatenate([o_g0, o_g1], axis=0).T.astype(o_ref.dtype)


def _attn_prompt(proj, slopes, bsz, t, n_kv, q_col, k_col, v_col):
    n = proj.shape[0]
    bs = MOBA_BLOCK
    nb = t // bs
    nbp = -(-nb // 16) * 16
    n_pairs = n_kv // 2
    qw = 2 * LANES
    hd = ATTN_HEAD_DIM
    grid_spec = pltpu.PrefetchScalarGridSpec(
        num_scalar_prefetch=1,
        grid=(bsz, n_pairs, nb),
        in_specs=[pl.BlockSpec((bs, qw), lambda b, h, i, s: (b * nb + i, q_col // qw + h)),
                  pl.BlockSpec((t, LANES), lambda b, h, i, s: (b, k_col // LANES + h)),
                  pl.BlockSpec((t, LANES), lambda b, h, i, s: (b, v_col // LANES + h))],
        out_specs=pl.BlockSpec((bs, qw), lambda b, h, i, s: (b * nb + i, h)),
        scratch_shapes=[pltpu.VMEM((nb, bs, LANES), BF16),
                        pltpu.VMEM((nb, LANES, bs), BF16),
                        pltpu.VMEM((nbp, LANES), F32),
                        pltpu.VMEM((4, nbp, bs), F32),
                        pltpu.VMEM((4, bs, bs), F32),
                        pltpu.VMEM((4, LANES, bs), BF16),
                        pltpu.VMEM((4, 1, bs), F32),
                        pltpu.VMEM((4, 1, bs), F32),
                        pltpu.VMEM((4, hd, bs), F32)])
    return pl.pallas_call(
        functools.partial(_attn_prompt_kernel, nb),
        out_shape=jax.ShapeDtypeStruct((n, n_kv * LANES), BF16),
        grid_spec=grid_spec,
        compiler_params=_cparams(3),
        name="attn_prompt",
    )(slopes, proj, proj, proj)


def _diag_extract(o, n_kv, rows_per_kv):
    hd = ATTN_HEAD_DIM
    tiles = []
    for k in range(n_kv):
        tl = o[k * rows_per_kv:(k + 1) * rows_per_kv, (k // 2) * LANES:(k // 2 + 1) * LANES]
        if k % 2 == 1:
            tl = pltpu.roll(tl, hd, 1)
        tiles.append(tl)
    return jnp.concatenate(tiles, axis=0)


def _attn_sample_kernel(bps, n_kv, tq, past_len, pt_ref, *refs):
    npg = 2 * bps
    kp = refs[0:npg]
    vp = refs[npg:2 * npg]
    qbd_ref, slope_ref, trow_ref, knew_ref, vnew_ref, o_ref, st_scr, km_scr, kn_scr, vn_scr = refs[2 * npg:]
    bs = MOBA_BLOCK
    hd = ATTN_HEAD_DIM
    rows = qbd_ref.shape[1]
    kvw = qbd_ref.shape[2]
    page = kp[0].shape[-1]
    rpk = rows // n_kv
    s_id = pl.program_id(1)
    n_steps = pl.num_programs(1)
    nblk = past_len // bs

    @pl.when(s_id == 0)
    def _():
        km_scr[...] = jnp.zeros(km_scr.shape, F32)

    qf = qbd_ref[0]
    qb = (qf * (hd ** -0.5)).astype(BF16)
    slope = slope_ref[...]
    lane = lax.broadcasted_iota(jnp.int32, (rows, LANES), 1)
    lane_k = lax.broadcasted_iota(jnp.int32, (kvw, LANES), 1)
    sc0 = slope * lane.astype(F32)
    sc1 = sc0 + slope * float(page)

    for bi in range(bps):
        j = s_id * bps + bi
        kt0 = kp[2 * bi][0].reshape(kvw, page)
        kt1 = kp[2 * bi + 1][0].reshape(kvw, page)
        vt0 = vp[2 * bi][0].reshape(kvw, page)
        vt1 = vp[2 * bi + 1][0].reshape(kvw, page)
        ksum = jnp.sum(kt0 + kt1, axis=1, keepdims=True) * (1.0 / bs)
        km_scr[...] = jnp.where(lane_k == j, ksum, km_scr[...])
        s0 = _dot(qb, kt0.astype(BF16)) + sc0
        s1 = _dot(qb, kt1.astype(BF16)) + sc1
        m = jnp.maximum(jnp.max(s0, axis=-1, keepdims=True), jnp.max(s1, axis=-1, keepdims=True))
        p0 = jnp.exp(s0 - m)
        p1 = jnp.exp(s1 - m)
        l = jnp.sum(p0, axis=-1, keepdims=True) + jnp.sum(p1, axis=-1, keepdims=True)
        o = _dot_nt(p0.astype(BF16), vt0.astype(BF16)) + _dot_nt(p1.astype(BF16), vt1.astype(BF16))
        od = _diag_extract(o, n_kv, rpk)
        st_scr[j] = jnp.where(lane < hd, od, jnp.where(lane == hd, m, l))

    @pl.when(s_id == n_steps - 1)
    def _():
        trow = trow_ref[...]
        gate = _dot_precise(qf, km_scr[...])
        gate = jnp.where(lane < nblk, gate, NEG_INF)
        sel, _ = _topk_mask(gate, lane, MOBA_TOPK)
        selneg = jnp.where(sel, 0.0, NEG_INF)

        kn_scr[...] = jnp.zeros(kn_scr.shape, F32)
        vn_scr[...] = jnp.zeros(vn_scr.shape, F32)
        kn_scr[0:tq, :] = knew_ref[...]
        vn_scr[0:tq, :] = vnew_ref[...]
        lanef = lane.astype(F32)
        so = _dot_nt(qb, kn_scr[...].astype(BF16)) - slope * (trow - lanef)
        so = jnp.where(jnp.logical_and(lanef <= trow, lane < tq), so, NEG_INF)
        m_run = jnp.max(so, axis=-1, keepdims=True)
        po = jnp.exp(so - m_run)
        den = jnp.sum(po, axis=-1, keepdims=True)
        num = _diag_extract(_dot(po.astype(BF16), vn_scr[...].astype(BF16)), n_kv, rpk)

        slope1 = slope[:, 0:1]
        base = -slope1 * (float(past_len) + trow[:, 0:1])

        def body(j, carry):
            m_run, num, den = carry
            tile = st_scr[j]
            selcol = jnp.max(jnp.where(lane == j, selneg, NEG_INF), axis=-1, keepdims=True)
            mj = tile[:, hd:hd + 1] + (base + slope1 * jnp.asarray(j * bs, dtype=F32)) + selcol
            lj = tile[:, hd + 1:hd + 2]
            m_new = jnp.maximum(m_run, mj)
            a = jnp.exp(m_run - m_new)
            b = jnp.exp(mj - m_new)
            return m_new, a * num + b * tile, a * den + b * lj

        m_run, num, den = lax.fori_loop(0, nblk, body, (m_run, num, den))
        o_ref[0] = num / den


def _attn_sample(cache_kt, cache_vt, layer, page_table, qbd, slope_rows, t_rows, k_new, v_new, tq, n_kv, past_len):
    nbat, rows, kvw = qbd.shape
    page = cache_kt.shape[4]
    hd = cache_kt.shape[3]
    bs = MOBA_BLOCK
    nblk = past_len // bs
    bps = 4
    while nblk % bps:
        bps //= 2
    n_steps = nblk // bps
    assert bs == 2 * page and nblk <= LANES
    npg = 2 * bps

    def page_spec(r):
        return pl.BlockSpec((None, 1, n_kv, hd, page), lambda b, s, pt: (layer, pt[b, s * npg + r], 0, 0, 0))

    in_specs = ([page_spec(r) for r in range(npg)] + [page_spec(r) for r in range(npg)]
                + [pl.BlockSpec((1, rows, kvw), lambda b, s, pt: (b, 0, 0)),
                   pl.BlockSpec((rows, LANES), lambda b, s, pt: (0, 0)),
                   pl.BlockSpec((rows, LANES), lambda b, s, pt: (0, 0)),
                   pl.BlockSpec((tq, kvw), lambda b, s, pt: (b, 0)),
                   pl.BlockSpec((tq, kvw), lambda b, s, pt: (b, 0))])
    grid_spec = pltpu.PrefetchScalarGridSpec(
        num_scalar_prefetch=1,
        grid=(nbat, n_steps),
        in_specs=in_specs,
        out_specs=pl.BlockSpec((1, rows, LANES), lambda b, s, pt: (b, 0, 0)),
        scratch_shapes=[pltpu.VMEM((nblk, rows, LANES), F32),
                        pltpu.VMEM((kvw, LANES), F32),
                        pltpu.VMEM((LANES, kvw), F32),
                        pltpu.VMEM((LANES, kvw), F32)])
    return pl.pallas_call(
        functools.partial(_attn_sample_kernel, bps, n_kv, tq, past_len),
        out_shape=jax.ShapeDtypeStruct((nbat, rows, LANES), F32),
        grid_spec=grid_spec,
        compiler_params=_cparams(2),
        name="attn_sample",
    )(page_table, *([cache_kt] * npg), *([cache_vt] * npg), qbd, slope_rows, t_rows, k_new, v_new)


def _outproj_kernel(alpha, n_exp, ys_ref, ya_ref, gs_ref, ga_ref, x_ref, gm_ref, scf_ref, shf_ref,
                    wos_ref, woa_ref, wout_ref, g1_ref, b1_ref, wr_ref, br_ref,
                    x1_ref, hf_ref, meta_ref, cnt_ref, cnt_scr):
    tm = x_ref.shape[0]
    merged = (jax.nn.sigmoid(gs_ref[...]) * _dot(ys_ref[...].astype(BF16), wos_ref[...])
              + jax.nn.sigmoid(ga_ref[...]) * _dot(ya_ref[...], woa_ref[...]))
    upd = _dot(merged.astype(BF16), wout_ref[...])
    x1 = _layer_norm(alpha * x_ref[...] + gm_ref[0] * upd, g1_ref[...], b1_ref[...])
    x1_ref[...] = x1
    hf = x1 * scf_ref[0] + shf_ref[0]
    hf_ref[...] = hf
    logits = _dot_precise(hf, wr_ref[...]) + br_ref[...]
    lane = lax.broadcasted_iota(jnp.int32, logits.shape, 1)
    logits = jnp.where(lane < n_exp, logits, NEG_INF)
    g = logits
    picks, vals, ids = [], [], []
    for _ in range(TOP_K):
        m = jnp.max(g, axis=-1, keepdims=True)
        idx = jnp.min(jnp.where(g == m, lane, LANES), axis=-1, keepdims=True)
        pick = lane == idx
        g = jnp.where(pick, NEG_INF, g)
        picks.append(pick)
        vals.append(m)
        ids.append(idx)
    es = [jnp.exp(v - vals[0]) for v in vals]
    inv = 1.0 / (es[0] + es[1] + es[2] + es[3])

    @pl.when(pl.program_id(0) == 0)
    def _():
        cnt_scr[...] = jnp.zeros(cnt_scr.shape, F32)

    onehot = jnp.zeros(logits.shape, F32)
    for pick in picks:
        onehot = onehot + jnp.where(pick, 1.0, 0.0)
    ri = lax.broadcasted_iota(jnp.int32, (tm, tm), 0)
    ci = lax.broadcasted_iota(jnp.int32, (tm, tm), 1)
    ltri = jnp.where(ri > ci, 1.0, 0.0).astype(BF16)
    rank_mat = _dot(ltri, onehot.astype(BF16)) + cnt_scr[0:1, :]
    cnt_new = cnt_scr[0:1, :] + jnp.sum(onehot, axis=0, keepdims=True)
    cnt_scr[...] = jnp.broadcast_to(cnt_new, cnt_scr.shape)
    cnt_ref[...] = jnp.broadcast_to(cnt_new, cnt_ref.shape)
    meta = jnp.zeros(logits.shape, F32)
    for k in range(TOP_K):
        rank_k = jnp.sum(jnp.where(picks[k], rank_mat, 0.0), axis=-1, keepdims=True)
        meta = jnp.where(lane == k, ids[k].astype(F32), meta)
        meta = jnp.where(lane == TOP_K + k, rank_k, meta)
        meta = jnp.where(lane == 2 * TOP_K + k, es[k] * inv, meta)
    meta_ref[...] = meta


def _outproj(alpha, n_exp, ys, ya, proj, x, gm, scf1, shf, wos, woa, wout, g1, b1, wr, br, tm, tiles_per_group,
             gs_blk, ga_blk):
    n, d = x.shape
    r = gm.shape[1]
    grp = lambda i: (i // tiles_per_group, 0, 0)
    const = lambda i: (0, 0)
    return pl.pallas_call(
        functools.partial(_outproj_kernel, alpha, n_exp),
        out_shape=(jax.ShapeDtypeStruct((n, d), F32),
                   jax.ShapeDtypeStruct((n, d), F32),
                   jax.ShapeDtypeStruct((n, LANES), F32),
                   jax.ShapeDtypeStruct((SUBLANES, LANES), F32)),
        grid=(n // tm,),
        in_specs=[pl.BlockSpec((tm, ys.shape[1]), lambda i: (i, 0)),
                  pl.BlockSpec((tm, ya.shape[1]), lambda i: (i, 0)),
                  pl.BlockSpec((tm, d), lambda i: (i, gs_blk)),
                  pl.BlockSpec((tm, d), lambda i: (i, ga_blk)),
                  pl.BlockSpec((tm, d), lambda i: (i, 0)),
                  pl.BlockSpec((1, r, d), grp),
                  pl.BlockSpec((1, r, d), grp),
                  pl.BlockSpec((1, r, d), grp),
                  pl.BlockSpec(wos.shape, const),
                  pl.BlockSpec(woa.shape, const),
                  pl.BlockSpec(wout.shape, const),
                  pl.BlockSpec((1, d), const),
                  pl.BlockSpec((1, d), const),
                  pl.BlockSpec((d, LANES), const),
                  pl.BlockSpec((1, LANES), const)],
        out_specs=(pl.BlockSpec((tm, d), lambda i: (i, 0)),
                   pl.BlockSpec((tm, d), lambda i: (i, 0)),
                   pl.BlockSpec((tm, LANES), lambda i: (i, 0)),
                   pl.BlockSpec((SUBLANES, LANES), const)),
        scratch_shapes=[pltpu.VMEM((SUBLANES, LANES), F32)],
        compiler_params=_cparams(1),
        name="outproj",
    )(ys, ya, proj, proj, x, gm, scf1, shf, wos, woa, wout, g1, b1, wr, br)


def _sc_gather(table, idx):
    _, dcol = table.shape
    b = idx.shape[0]
    nw = SC_CORES * SC_SUBCORES
    ch = SC_GATHER_ROWS
    assert b % (nw * ch) == 0
    b_per_w = b // nw
    n_ch = b_per_w // ch
    mesh = plsc.VectorSubcoreMesh(core_axis_name="c", subcore_axis_name="s")

    @functools.partial(
        pl.kernel, mesh=mesh,
        out_type=jax.ShapeDtypeStruct((b, dcol), table.dtype),
        scratch_types=[pltpu.VMEM((ch,), jnp.int32),
                       pltpu.VMEM((ch, dcol), table.dtype),
                       pltpu.SemaphoreType.DMA],
        name="sc_gather",
    )
    def gather(table_hbm, idx_hbm, out_hbm, idx_v, rows_v, sem):
        wid = lax.axis_index("s") * SC_CORES + lax.axis_index("c")
        base = wid * b_per_w

        @pl.loop(0, n_ch)
        def _(c):
            off = pl.multiple_of(base + c * ch, ch)
            pltpu.sync_copy(idx_hbm.at[pl.ds(off, ch)], idx_v)
            pltpu.async_copy(table_hbm.at[idx_v], rows_v, sem).wait()
            pltpu.sync_copy(rows_v, out_hbm.at[pl.ds(off, ch)])

    return gather(table, idx)


def _moe_experts_kernel(d_ff, te_ref, x_ref, wgu_ref, bgu_ref, wdn_ref, bdn_ref, o_ref, wgu_scr, wdn_scr):
    i = pl.program_id(0)
    prev = te_ref[jnp.maximum(i - 1, 0)]

    @pl.when(jnp.logical_or(i == 0, te_ref[i] != prev))
    def _():
        wgu_scr[...] = wgu_ref[0].astype(BF16)
        wdn_scr[...] = wdn_ref[0].astype(BF16)

    gu = _dot(x_ref[...].astype(BF16), wgu_scr[...]) + bgu_ref[0]
    gg = jnp.minimum(gu[:, :d_ff], SWIGLU_LIMIT)
    uu = jnp.clip(gu[:, d_ff:], -SWIGLU_LIMIT, SWIGLU_LIMIT)
    act = (uu + 1.0) * (gg * jax.nn.sigmoid(SWIGLU_ALPHA * gg))
    o_ref[...] = _dot(act.astype(BF16), wdn_scr[...]) + bdn_ref[0]


def _moe_experts(xs, tile_expert, layer, w_gu, b_gu4, w_down, b_down4, tme):
    p, d = xs.shape
    two_ff = w_gu.shape[3]
    d_ff = two_ff // 2
    grid_spec = pltpu.PrefetchScalarGridSpec(
        num_scalar_prefetch=1,
        grid=(p // tme,),
        in_specs=[pl.BlockSpec((tme, d), lambda i, te: (i, 0)),
                  pl.BlockSpec((None, 1, d, two_ff), lambda i, te: (layer, te[i], 0, 0)),
                  pl.BlockSpec((None, 1, 1, two_ff), lambda i, te: (layer, te[i], 0, 0)),
                  pl.BlockSpec((None, 1, d_ff, d), lambda i, te: (layer, te[i], 0, 0)),
                  pl.BlockSpec((None, 1, 1, d), lambda i, te: (layer, te[i], 0, 0))],
        out_specs=pl.BlockSpec((tme, d), lambda i, te: (i, 0)),
        scratch_shapes=[pltpu.VMEM((d, two_ff), BF16),
                        pltpu.VMEM((d_ff, d), BF16)])
    return pl.pallas_call(
        functools.partial(_moe_experts_kernel, d_ff),
        out_shape=jax.ShapeDtypeStruct((p, d), F32),
        grid_spec=grid_spec,
        compiler_params=_cparams(1),
        name="moe_experts",
    )(tile_expert, xs, w_gu, b_gu4, w_down, b_down4)


def _moe_combine_kernel(alpha, yk_ref, meta_ref, x1_ref, gf_ref, g2_ref, b2_ref, o_ref):
    d = x1_ref.shape[1]
    meta = meta_ref[...]
    acc = meta[:, 2 * TOP_K:2 * TOP_K + 1] * yk_ref[:, 0:d]
    for k in range(1, TOP_K):
        acc = acc + meta[:, 2 * TOP_K + k:2 * TOP_K + k + 1] * yk_ref[:, k * d:(k + 1) * d]
    o_ref[...] = _layer_norm(alpha * x1_ref[...] + gf_ref[0] * acc, g2_ref[...], b2_ref[...])


def _moe_combine(alpha, yk, row0, meta, x1, gf, g2, b2, tm, tiles_per_group):
    n, d = x1.shape
    r = gf.shape[1]
    blk0 = row0 // tm
    assert row0 % tm == 0
    return pl.pallas_call(
        functools.partial(_moe_combine_kernel, alpha),
        out_shape=jax.ShapeDtypeStruct((n, d), F32),
        grid=(n // tm,),
        in_specs=[pl.BlockSpec((tm, TOP_K * d), lambda i: (blk0 + i, 0)),
                  pl.BlockSpec((tm, LANES), lambda i: (i, 0)),
                  pl.BlockSpec((tm, d), lambda i: (i, 0)),
                  pl.BlockSpec((1, r, d), lambda i: (i // tiles_per_group, 0, 0)),
                  pl.BlockSpec((1, d), lambda i: (0, 0)),
                  pl.BlockSpec((1, d), lambda i: (0, 0))],
        out_specs=pl.BlockSpec((tm, d), lambda i: (i, 0)),
        compiler_params=_cparams(1),
        name="moe_combine",
    )(yk, meta, x1, gf, g2, b2)


def _pad_lanes(v, fill=0.0):
    return jnp.pad(v, (0, LANES - v.shape[0]), constant_values=fill).reshape(1, LANES)


def kernel(x_prompt, x_sample, c_prompt, c_sample, cache_k, cache_v, state_ssm, state_conv, page_table, w_ada, b_ada, w_in, conv_w, conv_b, dt_bias, a_log, d_skip, ssm_norm_g, w_o_ssm, w_o_attn, w_out, ln1_g, ln1_b, ln2_g, ln2_b, w_router, b_router, w_gu, b_gu, w_down, b_down):
    depth, d, _ = w_ada.shape
    bp, tp, _ = x_prompt.shape
    bsm, ts, _ = x_sample.shape
    n_heads = dt_bias.shape[1]
    d_inner = n_heads * SSM_HEAD_DIM
    conv_ch = conv_w.shape[2]
    kvw = cache_k.shape[3] * cache_k.shape[4]
    n_kv = cache_k.shape[3]
    aw = w_o_attn.shape[1]
    n_q = aw // ATTN_HEAD_DIM
    n_exp = w_router.shape[2]
    page = cache_k.shape[2]
    past_len = page_table.shape[1] * page
    alpha = float((2.0 * depth) ** 0.25)
    assert d_inner == 2 * d and aw == d and kvw == d // 2 and n_q == 2 * n_kv
    assert conv_ch == d_inner + 2 * SSM_GROUPS * SSM_STATE and conv_ch == 3 * d
    assert tp % MOBA_BLOCK == 0 and tp % SSD_CHUNK == 0 and ts == SUBLANES and n_heads <= LANES

    o_z, o_xbc, o_dt = 0, d_inner, d_inner + conv_ch
    o_q = o_dt + n_heads
    o_k, o_v = o_q + aw, o_q + aw + kvw
    o_gs, o_ga = o_v + kvw, o_v + kvw + d
    c_z, c_q, c_gs, c_ga = 0, d_inner, d_inner + d, d_inner + 2 * d
    c_k = d_inner + 3 * d
    c_v = c_k + kvw
    c_xbc = c_v + kvw
    assert c_xbc % conv_ch == 0

    slopes = 2.0 ** (-8.0 * (jnp.arange(n_q, dtype=F32) + 1.0) / n_q)

    n_c = bsm + bp
    n_c_pad = -(-n_c // SUBLANES) * SUBLANES
    c_all = jnp.concatenate([c_sample, c_prompt, jnp.zeros((n_c_pad - n_c, d), F32)], axis=0)
    mod = _ada(c_all, w_ada, b_ada)

    grp = n_q // n_kv
    rows = n_kv * grp * ts
    r_idx = jnp.arange(rows)
    r_head = (r_idx // (grp * ts)) * grp + (r_idx // ts) % grp
    slope_rows = jnp.broadcast_to(slopes[r_head][:, None], (rows, LANES))
    t_rows = jnp.broadcast_to((r_idx % ts).astype(F32)[:, None], (rows, LANES))
    eye_kv = jnp.eye(n_kv, dtype=F32)

    cache_kt = jnp.transpose(cache_k, (0, 1, 3, 4, 2))
    cache_vt = jnp.transpose(cache_v, (0, 1, 3, 4, 2))

    xp = x_prompt.reshape(bp * tp, d)
    xs = x_sample.reshape(bsm * ts, d)
    tm_p = 1024 if tp % 1024 == 0 else MOBA_BLOCK
    tm_o = 512 if tp % 512 == 0 else MOBA_BLOCK
    n_s = bsm * ts

    outs = {k: [] for k in ("kp", "vp", "hp", "cvp", "ks", "vs", "hs", "cvs")}
    for l in range(depth):
        wl = w_in[l]
        w_main = jnp.concatenate([wl[:, o_z:o_z + d_inner], wl[:, o_q:o_q + aw], wl[:, o_gs:o_gs + d],
                                  wl[:, o_ga:o_ga + d], wl[:, o_k:o_k + kvw], wl[:, o_v:o_v + kvw],
                                  wl[:, o_xbc:o_xbc + conv_ch]], axis=1).astype(BF16)
        w_dt = jnp.pad(wl[:, o_dt:o_dt + n_heads], ((0, 0), (0, LANES - n_heads))).astype(BF16)
        w_dtT = wl[:, o_dt:o_dt + n_heads].T.astype(BF16)
        dtb = _pad_lanes(dt_bias[l])
        dtbT = jnp.broadcast_to(dt_bias[l][:, None], (n_heads, SSD_CHUNK))
        a_neg = -jnp.exp(a_log[l])
        a_row = _pad_lanes(a_neg)
        aT = jnp.broadcast_to(a_neg[:, None], (n_heads, SSD_CHUNK))
        dskip = jnp.repeat(d_skip[l], SSM_HEAD_DIM).reshape(1, d_inner)
        ng = ssm_norm_g[l].reshape(1, d_inner)
        cw = conv_w[l]
        cb = conv_b[l].reshape(1, conv_ch)
        wos = w_o_ssm[l].astype(BF16)
        woa = w_o_attn[l].astype(BF16)
        wout = w_out[l].astype(BF16)
        g1, b1 = ln1_g[l].reshape(1, d), ln1_b[l].reshape(1, d)
        g2, b2 = ln2_g[l].reshape(1, d), ln2_b[l].reshape(1, d)
        wr = jnp.pad(w_router[l], ((0, 0), (0, LANES - n_exp)))
        br = _pad_lanes(b_router[l])

        mod_l = mod[l]
        sh_m, sc_m, g_m, sh_f, sc_f, g_f = [mod_l[:, i * d:(i + 1) * d] for i in range(6)]

        def group_mod(v, is_prompt):
            if is_prompt:
                return v[bsm:bsm + bp].reshape(bp, 1, d)
            return jnp.repeat(v[:bsm], ts, axis=0).reshape(1, n_s, d)

        def run(x, is_prompt, h0T, cprev8):
            bsz, t = (bp, tp) if is_prompt else (bsm, ts)
            n = bsz * t
            tm = tm_p if is_prompt else n
            tmo = tm_o if is_prompt else n
            tpg = (t // tm) if is_prompt else 1
            tpgo = (t // tmo) if is_prompt else 1
            gm = lambda v: group_mod(v, is_prompt)
            proj, dt, dtT = _inproj(x, gm(1.0 + sc_m), gm(sh_m), w_main, w_dt, w_dtT, tm, tpg)
            dtT_seq = dtT.reshape(n_heads, bsz, t).transpose(1, 0, 2)
            lr = SSD_CHUNK if is_prompt else t
            y_s, h_newT = _ssd(proj, dt, dtT_seq, h0T, cprev8, cw, cb, dtb, dtbT, a_row, aT, dskip, ng,
                               bsz, t, lr, c_z // d_inner, c_xbc // conv_ch, d_inner, conv_ch)
            k_new = proj[:, c_k:c_k + kvw]
            v_new = proj[:, c_v:c_v + kvw]
            if is_prompt:
                y_a = _attn_prompt(proj, slopes, bsz, t, n_kv, c_q, c_k, c_v)
            else:
                q5 = proj[:, c_q:c_q + aw].reshape(bsz, t, n_kv, grp, ATTN_HEAD_DIM)
                q5 = q5.transpose(0, 2, 3, 1, 4).reshape(bsz, n_kv, grp * t, ATTN_HEAD_DIM)
                qbd = (q5[:, :, :, None, :] * eye_kv[None, :, None, :, None]).reshape(bsz, rows, kvw)
                o = _attn_sample(cache_kt, cache_vt, l, page_table, qbd, slope_rows, t_rows, k_new, v_new,
                                 t, n_kv, past_len)
                o = o[:, :, :ATTN_HEAD_DIM].reshape(bsz, n_kv, grp, t, ATTN_HEAD_DIM)
                y_a = o.transpose(0, 3, 1, 2, 4).reshape(n, aw).astype(BF16)
            x1, hf, meta, cnt = _outproj(alpha, n_exp, y_s, y_a, proj, x, gm(g_m), gm(1.0 + sc_f), gm(sh_f),
                                         wos, woa, wout, g1, b1, wr, br, tmo, tpgo, c_gs // d, c_ga // d)
            conv_new = proj[:, c_xbc:c_xbc + conv_ch].reshape(bsz, t, conv_ch)[:, t - (CONV_W - 1):]
            h_new = h_newT.transpose(0, 2, 1).reshape(bsz, n_heads, SSM_HEAD_DIM, SSM_STATE)
            hd4 = (bsz, t, n_kv, ATTN_HEAD_DIM)
            return (x1, hf, meta, cnt[0, :n_exp].astype(jnp.int32), gm(g_f), tmo, tpgo,
                    k_new.reshape(hd4), v_new.reshape(hd4), h_new, conv_new)

        h0_p = jnp.zeros((bp, SSM_STATE, d_inner), F32)
        cv0_p = jnp.zeros((bp, SUBLANES, conv_ch), F32)
        x1_p, hf_p, meta_p, cnt_p, gf_p, tmo_p, tpgo_p, k1, v1, h1, cv1 = run(xp, True, h0_p, cv0_p)
        outs["kp"].append(k1); outs["vp"].append(v1); outs["hp"].append(h1); outs["cvp"].append(cv1)

        h0_s = state_ssm[l].reshape(bsm, d_inner, SSM_STATE).transpose(0, 2, 1)
        cv0_s = jnp.pad(state_conv[l], ((0, 0), (SUBLANES - (CONV_W - 1), 0), (0, 0)))
        x1_s, hf_s, meta_s, cnt_s, gf_s, tmo_s, tpgo_s, k1, v1, h1, cv1 = run(xs, False, h0_s, cv0_s)
        outs["ks"].append(k1); outs["vs"].append(v1); outs["hs"].append(h1); outs["cvs"].append(cv1)

        n_p, n_all = bp * tp, bp * tp + n_s
        meta_all = jnp.concatenate([meta_p, meta_s], axis=0)
        eid = meta_all[:, 0:TOP_K].astype(jnp.int32)
        rank = meta_all[:, TOP_K:2 * TOP_K].astype(jnp.int32)
        e_iota = jnp.arange(n_exp, dtype=jnp.int32)
        counts = cnt_p + cnt_s
        padded = -(-counts // MOE_TILE) * MOE_TILE
        ends = jnp.cumsum(padded)
        is_e = eid[:, :, None] == e_iota
        start = (ends - padded)[None, None, :] + jnp.where(jnp.arange(n_all)[:, None, None] >= n_p,
                                                           cnt_p[None, None, :], 0)
        dest = jnp.sum(jnp.where(is_e, start, 0), axis=-1) + rank
        gran = SC_CORES * SC_SUBCORES * SC_GATHER_ROWS
        p_rows = -(-(n_all * TOP_K + n_exp * MOE_TILE) // gran) * gran
        tok = jnp.broadcast_to(jnp.arange(n_all, dtype=jnp.int32)[:, None], (n_all, TOP_K))
        src_tok = jnp.zeros((p_rows,), jnp.int32).at[dest.reshape(-1)].set(
            tok.reshape(-1), unique_indices=True, mode="promise_in_bounds")
        tile_start = jnp.arange(p_rows // MOE_TILE, dtype=jnp.int32) * MOE_TILE
        tile_expert = jnp.minimum(jnp.sum(tile_start[:, None] >= ends[None, :], axis=-1), n_exp - 1).astype(jnp.int32)

        hf_all = jnp.concatenate([hf_p, hf_s], axis=0)
        xs_sorted = _sc_gather(hf_all, src_tok)
        ys_sorted = _moe_experts(xs_sorted, tile_expert, l, w_gu, b_gu.reshape(depth, n_exp, 1, -1),
                                 w_down, b_down.reshape(depth, n_exp, 1, d), MOE_TILE)
        n_back = -(-(n_all * TOP_K) // gran) * gran
        dest_flat = jnp.concatenate([dest.reshape(-1), jnp.zeros((n_back - n_all * TOP_K,), jnp.int32)])
        yk = _sc_gather(ys_sorted, dest_flat).reshape(n_back // TOP_K, TOP_K * d)
        xp = _moe_combine(alpha, yk, 0, meta_p, x1_p, gf_p, g2, b2, MOBA_BLOCK, tp // MOBA_BLOCK)
        xs = _moe_combine(alpha, yk, n_p, meta_s, x1_s, gf_s, g2, b2, n_s, 1)

    st = lambda k: jnp.stack(outs[k])
    return (xp.reshape(bp, tp, d), xs.reshape(bsm, ts, d), st("kp"), st("vp"), st("hp"), st("cvp"),
            st("ks"), st("vs"), st("hs"), st("cvs"))
```

```python
import functools
import math

import jax
import jax.numpy as jnp
from jax import lax
from jax.experimental import pallas as pl
from jax.experimental.pallas import tpu as pltpu
from jax.experimental.pallas import tpu_sc as plsc

F32 = jnp.float32
BF16 = jnp.bfloat16
NEG_INF = float("-inf")

LANES = 128
SUBLANES = 8
VMEM_LIMIT = 56 * 1024 * 1024
SC_CORES = 2
SC_SUBCORES = 16
SC_GATHER_ROWS = 32
MOE_TILE = 512
MOE_TILE_SMALL = 64

SSM_HEAD_DIM = 64
SSM_GROUPS = 4
SSM_STATE = 128
CONV_W = 4
SSD_CHUNK = 128
RMS_EPS = 1e-5
ATTN_HEAD_DIM = 64
MOBA_BLOCK = 256
MOBA_TOPK = 3
TOP_K = 4
SWIGLU_LIMIT = 7.0
SWIGLU_ALPHA = 1.702
LN_EPS = 1e-5


def _cparams(n_axes):
    return pltpu.CompilerParams(dimension_semantics=("arbitrary",) * n_axes,
                                vmem_limit_bytes=VMEM_LIMIT)


def _dot(a, b):
    return jnp.dot(a, b, preferred_element_type=F32)


def _dot_nt(a, b):
    return lax.dot_general(a, b, (((1,), (1,)), ((), ())), preferred_element_type=F32)


def _split2(a):
    hi = a.astype(BF16)
    lo = (a - hi.astype(F32)).astype(BF16)
    return hi, lo


def _split3(a):
    hi = a.astype(BF16)
    r = a - hi.astype(F32)
    mid = r.astype(BF16)
    lo = (r - mid.astype(F32)).astype(BF16)
    return hi, mid, lo


def _dot_nt_precise(a, b):
    ah, al = _split2(a)
    bh, bl = _split2(b)
    return _dot_nt(ah, bh) + (_dot_nt(ah, bl) + _dot_nt(al, bh))


def _dot_precise(a, b):
    ah, al = _split2(a)
    bh, bl = _split2(b)
    return _dot(ah, bh) + (_dot(ah, bl) + _dot(al, bh))


def _silu(x):
    return x * jax.nn.sigmoid(x)


def _softplus(x):
    return jnp.maximum(x, 0.0) + jnp.log1p(jnp.exp(-jnp.abs(x)))


def _layer_norm(x, g, b):
    mu = jnp.mean(x, axis=-1, keepdims=True)
    xc = x - mu
    var = jnp.mean(xc * xc, axis=-1, keepdims=True)
    return xc * lax.rsqrt(var + LN_EPS) * g + b


def _topk_mask(score, lane, k):
    width = score.shape[-1]
    sel = jnp.zeros(score.shape, dtype=jnp.bool_)
    vals = []
    g = score
    for _ in range(k):
        m = jnp.max(g, axis=-1, keepdims=True)
        idx = jnp.min(jnp.where(g == m, lane, width), axis=-1, keepdims=True)
        pick = jnp.logical_and(lane == idx, m > NEG_INF)
        sel = jnp.logical_or(sel, pick)
        g = jnp.where(pick, NEG_INF, g)
        vals.append(m)
    return sel, vals


def _ada_kernel(c_ref, w_ref, b_ref, o_ref):
    s = _silu(c_ref[...]).astype(BF16)
    o_ref[0] = _dot(s, w_ref[0].astype(BF16)) + b_ref[0]


def _ada(c_all, w_ada, b_ada):
    depth, d, n6 = w_ada.shape
    rows = c_all.shape[0]
    tn = 1024
    return pl.pallas_call(
        _ada_kernel,
        out_shape=jax.ShapeDtypeStruct((depth, rows, n6), F32),
        grid=(depth, n6 // tn),
        in_specs=[pl.BlockSpec((rows, d), lambda l, j: (0, 0)),
                  pl.BlockSpec((1, d, tn), lambda l, j: (l, 0, j)),
                  pl.BlockSpec((1, 1, tn), lambda l, j: (l, 0, j))],
        out_specs=pl.BlockSpec((1, rows, tn), lambda l, j: (l, 0, j)),
        compiler_params=_cparams(2),
        name="ada",
    )(c_all, w_ada, b_ada.reshape(depth, 1, n6))


def _inproj_kernel(x_ref, sc_ref, sh_ref, w_ref, wdt_ref, wdtT_ref, o_ref, dt_ref, dtT_ref, h_scr):
    @pl.when(pl.program_id(1) == 0)
    def _():
        hb = (x_ref[...] * sc_ref[0] + sh_ref[0]).astype(BF16)
        h_scr[...] = hb
        dt_ref[...] = _dot(hb, wdt_ref[...])
        dtT_ref[...] = _dot_nt(wdtT_ref[...], hb)

    o_ref[...] = _dot(h_scr[...], w_ref[...])


def _inproj(x, sc1, sh, w_main, w_dt, w_dtT, tm, tiles_per_group):
    n, d = x.shape
    n_main = w_main.shape[1]
    tn = 1024
    r = sc1.shape[1]
    nh = w_dtT.shape[0]
    grp = lambda i, j: (i // tiles_per_group, 0, 0)
    return pl.pallas_call(
        _inproj_kernel,
        out_shape=(jax.ShapeDtypeStruct((n, n_main), F32),
                   jax.ShapeDtypeStruct((n, LANES), F32),
                   jax.ShapeDtypeStruct((nh, n), F32)),
        grid=(n // tm, n_main // tn),
        in_specs=[pl.BlockSpec((tm, d), lambda i, j: (i, 0)),
                  pl.BlockSpec((1, r, d), grp),
                  pl.BlockSpec((1, r, d), grp),
                  pl.BlockSpec((d, tn), lambda i, j: (0, j)),
                  pl.BlockSpec((d, LANES), lambda i, j: (0, 0)),
                  pl.BlockSpec((nh, d), lambda i, j: (0, 0))],
        out_specs=(pl.BlockSpec((tm, tn), lambda i, j: (i, j)),
                   pl.BlockSpec((tm, LANES), lambda i, j: (i, 0)),
                   pl.BlockSpec((nh, tm), lambda i, j: (0, i))),
        scratch_shapes=[pltpu.VMEM((tm, d), BF16)],
        compiler_params=_cparams(2),
        name="inproj",
        cost_estimate=pl.CostEstimate(flops=2 * n * d * (n_main + LANES + nh), transcendentals=0,
                                      bytes_accessed=4 * n * (d + n_main) + 2 * d * n_main * (n // tm)),
    )(x, sc1, sh, w_main, w_dt, w_dtT)


def _ssd_kernel(lr, d_inner, xbc_ref, z_ref, dt_ref, dtT_ref, h0_ref, cprev_ref, cw_ref, cb_ref,
                dtb_ref, dtbT_ref, a_ref, aT_ref, dskip_ref, ng_ref,
                y_ref, hout_ref, xp_scr, h_scr, y_scr, dt_scr, dtT_scr):
    L = SSD_CHUNK
    gs = SSM_GROUPS * SSM_STATE
    n_heads = d_inner // SSM_HEAD_DIM
    pairs_per_group = n_heads // SSM_GROUPS // 2
    c = pl.program_id(1)

    @pl.when(c == 0)
    def _():
        xp_scr[0:SUBLANES, :] = cprev_ref[0]
        h_scr[...] = h0_ref[0]

    if lr < L:
        xp_scr[SUBLANES + lr:SUBLANES + L, :] = jnp.zeros((L - lr, xp_scr.shape[1]), F32)
    xp_scr[SUBLANES:SUBLANES + lr, :] = xbc_ref[...]

    acc = cb_ref[...] + cw_ref[0:1, :] * xp_scr[SUBLANES - 3:SUBLANES - 3 + L, :]
    for j in range(1, CONV_W):
        acc = acc + cw_ref[j:j + 1, :] * xp_scr[SUBLANES - 3 + j:SUBLANES - 3 + j + L, :]
    tail = xp_scr[lr:lr + SUBLANES, :]
    xp_scr[0:SUBLANES, :] = tail
    xc = _silu(acc)

    if lr < L:
        dt_scr[...] = jnp.zeros(dt_scr.shape, F32)
        dt_scr[0:lr, :] = dt_ref[...]
        dtT_scr[...] = jnp.zeros(dtT_scr.shape, F32)
        dtT_scr[:, 0:lr] = dtT_ref[0]
        dt_raw = dt_scr[...]
        dtT_raw = dtT_scr[...]
    else:
        dt_raw = dt_ref[...]
        dtT_raw = dtT_ref[0]
    row = lax.broadcasted_iota(jnp.int32, (L, LANES), 0)
    col = lax.broadcasted_iota(jnp.int32, (n_heads, L), 1)
    dtv = jnp.where(row < lr, _softplus(dt_raw + dtb_ref[...]), 0.0)
    dtvT = jnp.where(col < lr, _softplus(dtT_raw + dtbT_ref[...]), 0.0)
    dA = dtv * a_ref[...]
    dAT = dtvT * aT_ref[...]

    ri = lax.broadcasted_iota(jnp.int32, (L, L), 0)
    ci = lax.broadcasted_iota(jnp.int32, (L, L), 1)
    causal = ri >= ci
    tri = jnp.where(causal, 1.0, 0.0).astype(BF16)
    triT = jnp.where(ri <= ci, 1.0, 0.0).astype(BF16)
    d1, d2, d3 = _split3(dA)
    la = _dot(tri, d1) + (_dot(tri, d2) + _dot(tri, d3))
    e1, e2, e3 = _split3(dAT)
    laT = _dot(e1, triT) + (_dot(e2, triT) + _dot(e3, triT))

    la_last = la[L - 1:L, :]
    to_end = jnp.exp(la_last - la)
    ela = jnp.exp(la)
    cdec = jnp.exp(la_last)

    lane = lax.broadcasted_iota(jnp.int32, (L, LANES), 1)
    lo_half = lane < SSM_HEAD_DIM
    lane1 = lax.broadcasted_iota(jnp.int32, (1, LANES), 1)
    eye = jnp.where(ri == ci, 1.0, 0.0).astype(BF16)

    def pair_bcast(arr, p):
        rows = arr.shape[0]
        a0 = jnp.broadcast_to(arr[:, 2 * p:2 * p + 1], (rows, LANES))
        a1 = jnp.broadcast_to(arr[:, 2 * p + 1:2 * p + 2], (rows, LANES))
        return jnp.where(lo_half[:rows], a0, a1)

    for g in range(SSM_GROUPS):
        bg = xc[:, d_inner + g * SSM_STATE:d_inner + (g + 1) * SSM_STATE].astype(BF16)
        cg = xc[:, d_inner + gs + g * SSM_STATE:d_inner + gs + (g + 1) * SSM_STATE].astype(BF16)
        cb = _dot_nt(cg, bg)
        bgT = _dot_nt(eye, bg).astype(BF16)
        for pp in range(pairs_per_group):
            p = g * pairs_per_group + pp
            sl = slice(p * LANES, (p + 1) * LANES)
            xs_p = xc[:, sl]
            xdt_p = xs_p * pair_bcast(dtv, p)
            ydiag = None
            for a in range(2):
                h = 2 * p + a
                seg = jnp.broadcast_to(la[:, h:h + 1], (L, L)) - laT[h:h + 1, :]
                decay = jnp.exp(jnp.where(causal, seg, NEG_INF))
                m = (cb * decay).astype(BF16)
                half = lo_half if a == 0 else jnp.logical_not(lo_half)
                part = _dot(m, jnp.where(half, xdt_p, 0.0).astype(BF16))
                ydiag = part if ydiag is None else ydiag + part
            hin = h_scr[:, sl]
            yoff = _dot(cg, hin.astype(BF16)) * pair_bcast(ela, p)
            y_scr[:, sl] = ydiag + yoff + dskip_ref[:, sl] * xs_p
            s_new = _dot(bgT, (xdt_p * pair_bcast(to_end, p)).astype(BF16))
            cd_p = jnp.where(lane1 < SSM_HEAD_DIM,
                             jnp.broadcast_to(cdec[:, 2 * p:2 * p + 1], (1, LANES)),
                             jnp.broadcast_to(cdec[:, 2 * p + 1:2 * p + 2], (1, LANES)))
            h_scr[:, sl] = hin * cd_p + s_new

    gw = d_inner // SSM_GROUPS
    for g in range(SSM_GROUPS):
        sl = slice(g * gw, (g + 1) * gw)
        yg = y_scr[0:lr, sl] * _silu(z_ref[:, sl])
        ms = jnp.mean(yg * yg, axis=-1, keepdims=True)
        y_ref[:, sl] = (yg * lax.rsqrt(ms + RMS_EPS) * ng_ref[:, sl]).astype(y_ref.dtype)

    @pl.when(c == pl.num_programs(1) - 1)
    def _():
        hout_ref[0] = h_scr[...]


def _ssd(proj, dt, dtT_seq, h0T, cprev8, cw, cb, dtb, dtbT, a, aT, dskip, ng, bsz, t, lr,
         z_blk, xbc_blk, d_inner, conv_ch):
    n = proj.shape[0]
    nc = t // lr
    n_heads = d_inner // SSM_HEAD_DIM
    L = SSD_CHUNK
    const2 = lambda b, c: (0, 0)
    return pl.pallas_call(
        functools.partial(_ssd_kernel, lr, d_inner),
        out_shape=(jax.ShapeDtypeStruct((n, d_inner), BF16 if lr % 16 == 0 else F32),
                   jax.ShapeDtypeStruct((bsz, SSM_STATE, d_inner), F32)),
        grid=(bsz, nc),
        in_specs=[pl.BlockSpec((lr, conv_ch), lambda b, c: (b * nc + c, xbc_blk)),
                  pl.BlockSpec((lr, d_inner), lambda b, c: (b * nc + c, z_blk)),
                  pl.BlockSpec((lr, LANES), lambda b, c: (b * nc + c, 0)),
                  pl.BlockSpec((1, n_heads, lr), lambda b, c: (b, 0, c)),
                  pl.BlockSpec((1, SSM_STATE, d_inner), lambda b, c: (b, 0, 0)),
                  pl.BlockSpec((1, SUBLANES, conv_ch), lambda b, c: (b, 0, 0)),
                  pl.BlockSpec((CONV_W, conv_ch), const2),
                  pl.BlockSpec((1, conv_ch), const2),
                  pl.BlockSpec((1, LANES), const2),
                  pl.BlockSpec((n_heads, L), const2),
                  pl.BlockSpec((1, LANES), const2),
                  pl.BlockSpec((n_heads, L), const2),
                  pl.BlockSpec((1, d_inner), const2),
                  pl.BlockSpec((1, d_inner), const2)],
        out_specs=(pl.BlockSpec((lr, d_inner), lambda b, c: (b * nc + c, 0)),
                   pl.BlockSpec((1, SSM_STATE, d_inner), lambda b, c: (b, 0, 0))),
        scratch_shapes=[pltpu.VMEM((SUBLANES + L, conv_ch), F32),
                        pltpu.VMEM((SSM_STATE, d_inner), F32),
                        pltpu.VMEM((L, d_inner), F32),
                        pltpu.VMEM((L, LANES), F32),
                        pltpu.VMEM((n_heads, L), F32)],
        compiler_params=_cparams(2),
        name="ssd",
        cost_estimate=pl.CostEstimate(
            flops=2 * bsz * nc * L * (SSM_GROUPS * L * SSM_STATE + d_inner * (L + 2 * SSM_STATE)),
            transcendentals=bsz * nc * L * (n_heads * L + conv_ch + d_inner),
            bytes_accessed=4 * n * (conv_ch + d_inner) + 2 * n * d_inner + 8 * bsz * SSM_STATE * d_inner),
    )(proj, proj, dt, dtT_seq, h0T, cprev8, cw, cb, dtb, dtbT, a, aT, dskip, ng)


def _topk_mask_rows(score, row, k):
    big = score.shape[0]
    sel = jnp.zeros(score.shape, dtype=jnp.bool_)
    g = score
    for _ in range(k):
        m = jnp.max(g, axis=0, keepdims=True)
        idx = jnp.min(jnp.where(g == m, row, big), axis=0, keepdims=True)
        pick = jnp.logical_and(row == idx, m > NEG_INF)
        sel = jnp.logical_or(sel, pick)
        g = jnp.where(pick, NEG_INF, g)
    return sel


def _attn_prompt_kernel(nb, slopes_ref, q_ref, k_ref, v_ref, o_ref,
                        kb_scr, vt_scr, km_scr, sel_scr, bias_scr, qt_scr, m_scr, l_scr, acc_scr):
    bs = MOBA_BLOCK
    hd = ATTN_HEAD_DIM
    pr = pl.program_id(1)
    qi = pl.program_id(2)
    nbp = km_scr.shape[0]
    n_ch = 4

    @pl.when(qi == 0)
    def _():
        km_scr[...] = jnp.zeros(km_scr.shape, F32)
        for i in range(nb):
            kblk = k_ref[i * bs:(i + 1) * bs, :]
            kb_scr[i] = kblk.astype(BF16)
            vt_scr[i] = v_ref[i * bs:(i + 1) * bs, :].T.astype(BF16)
            km_scr[i:i + 1, :] = jnp.sum(kblk, axis=0, keepdims=True) * (1.0 / bs)

    lane = lax.broadcasted_iota(jnp.int32, (bs, LANES), 1)
    qts = []
    for par in range(2):
        qp = q_ref[:, par * LANES:(par + 1) * LANES]
        qr = pltpu.roll(qp, hd, 1)
        mine = (lane >= hd) if par == 1 else (lane < hd)
        q_g0 = jnp.where(mine, qp if par == 0 else qr, 0.0)
        q_g1 = jnp.where(mine, qr if par == 0 else qp, 0.0)
        qts += [q_g0.T, q_g1.T]

    rowb = lax.broadcasted_iota(jnp.int32, (nbp, bs), 0)
    rk = lax.broadcasted_iota(jnp.int32, (bs, bs), 0)
    cq = lax.broadcasted_iota(jnp.int32, (bs, bs), 1)
    rel = (cq - rk).astype(F32)
    km = km_scr[...]
    slopes = [slopes_ref[n_ch * pr + c] for c in range(n_ch)]
    kd = kb_scr[qi]
    for c in range(n_ch):
        par = c // 2
        gate = _dot_precise(km, qts[c])
        gate = jnp.where(rowb < qi, gate, NEG_INF)
        sel_scr[c] = jnp.where(_topk_mask_rows(gate, rowb, MOBA_TOPK), 0.0, NEG_INF)
        bias = -slopes[c] * rel
        bias_scr[c] = bias
        qtb = (qts[c] * (hd ** -0.5)).astype(BF16)
        qt_scr[c] = qtb
        s = _dot(kd, qtb) + bias
        s = jnp.where(rel >= 0.0, s, NEG_INF)
        m0 = jnp.max(s, axis=0, keepdims=True)
        p = jnp.exp(s - m0)
        m_scr[c] = m0
        l_scr[c] = jnp.sum(p, axis=0, keepdims=True)
        acc_scr[c] = _dot(vt_scr[qi, par * hd:(par + 1) * hd, :], p.astype(BF16))

    def body(j, carry):
        kj = kb_scr[j]
        dist0 = jnp.asarray((qi - j) * bs, dtype=F32)
        ss = [_dot(kj, qt_scr[c]) for c in range(n_ch)]
        for c in range(n_ch):
            par = c // 2
            colb = sel_scr[c, pl.ds(j, 1), :] - slopes[c] * dist0
            sj = ss[c] + bias_scr[c] + colb
            m_old = m_scr[c]
            m_new = jnp.maximum(m_old, jnp.max(sj, axis=0, keepdims=True))
            alpha = jnp.exp(m_old - m_new)
            pj = jnp.exp(sj - m_new)
            l_scr[c] = alpha * l_scr[c] + jnp.sum(pj, axis=0, keepdims=True)
            pv = _dot(vt_scr[j, par * hd:(par + 1) * hd, :], pj.astype(BF16))
            acc_scr[c] = alpha * acc_scr[c] + pv
            m_scr[c] = m_new
        return carry

    lax.fori_loop(0, qi, body, 0)

    for par in range(2):
        o_g0 = acc_scr[2 * par] / l_scr[2 * par]
        o_g1 = acc_scr[2 * par + 1] / l_scr[2 * par + 1]
        o_ref[:, par * LANES:(par + 1) * LANES] = jnp.concatenate([o_g0, o_g1], axis=0).T.astype(o_ref.dtype)


def _attn_prompt(proj, slopes, bsz, t, n_kv, q_col, k_col, v_col):
    n = proj.shape[0]
    bs = MOBA_BLOCK
    nb = t // bs
    nbp = -(-nb // 16) * 16
    n_pairs = n_kv // 2
    qw = 2 * LANES
    hd = ATTN_HEAD_DIM
    grid_spec = pltpu.PrefetchScalarGridSpec(
        num_scalar_prefetch=1,
        grid=(bsz, n_pairs, nb),
        in_specs=[pl.BlockSpec((bs, qw), lambda b, h, i, s: (b * nb + i, q_col // qw + h)),
                  pl.BlockSpec((t, LANES), lambda b, h, i, s: (b, k_col // LANES + h)),
                  pl.BlockSpec((t, LANES), lambda b, h, i, s: (b, v_col // LANES + h))],
        out_specs=pl.BlockSpec((bs, qw), lambda b, h, i, s: (b * nb + i, h)),
        scratch_shapes=[pltpu.VMEM((nb, bs, LANES), BF16),
                        pltpu.VMEM((nb, LANES, bs), BF16),
                        pltpu.VMEM((nbp, LANES), F32),
                        pltpu.VMEM((4, nbp, bs), F32),
                        pltpu.VMEM((4, bs, bs), F32),
                        pltpu.VMEM((4, LANES, bs), BF16),
                        pltpu.VMEM((4, 1, bs), F32),
                        pltpu.VMEM((4, 1, bs), F32),
                        pltpu.VMEM((4, hd, bs), F32)])
    return pl.pallas_call(
        functools.partial(_attn_prompt_kernel, nb),
        out_shape=jax.ShapeDtypeStruct((n, n_kv * LANES), BF16),
        grid_spec=grid_spec,
        compiler_params=_cparams(3),
        name="attn_prompt",
        cost_estimate=pl.CostEstimate(flops=2 * bsz * n_kv * 2 * (nb * (nb + 1) // 2) * bs * bs * 2 * LANES,
                                      transcendentals=bsz * n_kv * 2 * (nb * (nb + 1) // 2) * bs * bs,
                                      bytes_accessed=n * (4 * (qw * n_pairs + 2 * LANES * n_pairs) + 2 * qw * n_pairs)),
    )(slopes, proj, proj, proj)


def _diag_extract(o, n_kv, rows_per_kv):
    hd = ATTN_HEAD_DIM
    tiles = []
    for k in range(n_kv):
        tl = o[k * rows_per_kv:(k + 1) * rows_per_kv, (k // 2) * LANES:(k // 2 + 1) * LANES]
        if k % 2 == 1:
            tl = pltpu.roll(tl, hd, 1)
        tiles.append(tl)
    return jnp.concatenate(tiles, axis=0)


def _attn_sample_kernel(bps, n_kv, tq, past_len, pt_ref, *refs):
    npg = 2 * bps
    kp = refs[0:npg]
    vp = refs[npg:2 * npg]
    qbd_ref, slope_ref, trow_ref, knew_ref, vnew_ref, o_ref, st_scr, km_scr, kn_scr, vn_scr = refs[2 * npg:]
    bs = MOBA_BLOCK
    hd = ATTN_HEAD_DIM
    rows = qbd_ref.shape[1]
    kvw = qbd_ref.shape[2]
    page = kp[0].shape[-1]
    rpk = rows // n_kv
    s_id = pl.program_id(1)
    n_steps = pl.num_programs(1)
    nblk = past_len // bs

    @pl.when(s_id == 0)
    def _():
        km_scr[...] = jnp.zeros(km_scr.shape, F32)

    qf = qbd_ref[0]
    qb = (qf * (hd ** -0.5)).astype(BF16)
    slope = slope_ref[...]
    lane = lax.broadcasted_iota(jnp.int32, (rows, LANES), 1)
    lane_k = lax.broadcasted_iota(jnp.int32, (kvw, LANES), 1)
    sc0 = slope * lane.astype(F32)
    sc1 = sc0 + slope * float(page)

    for bi in range(bps):
        j = s_id * bps + bi
        kt0 = kp[2 * bi][0].reshape(kvw, page)
        kt1 = kp[2 * bi + 1][0].reshape(kvw, page)
        vt0 = vp[2 * bi][0].reshape(kvw, page)
        vt1 = vp[2 * bi + 1][0].reshape(kvw, page)
        ksum = jnp.sum(kt0 + kt1, axis=1, keepdims=True) * (1.0 / bs)
        km_scr[...] = jnp.where(lane_k == j, ksum, km_scr[...])
        s0 = _dot(qb, kt0.astype(BF16)) + sc0
        s1 = _dot(qb, kt1.astype(BF16)) + sc1
        m = jnp.maximum(jnp.max(s0, axis=-1, keepdims=True), jnp.max(s1, axis=-1, keepdims=True))
        p0 = jnp.exp(s0 - m)
        p1 = jnp.exp(s1 - m)
        l = jnp.sum(p0, axis=-1, keepdims=True) + jnp.sum(p1, axis=-1, keepdims=True)
        o = _dot_nt(p0.astype(BF16), vt0.astype(BF16)) + _dot_nt(p1.astype(BF16), vt1.astype(BF16))
        od = _diag_extract(o, n_kv, rpk)
        st_scr[j] = jnp.where(lane < hd, od, jnp.where(lane == hd, m, l))

    @pl.when(s_id == n_steps - 1)
    def _():
        trow = trow_ref[...]
        gate = _dot_precise(qf, km_scr[...])
        gate = jnp.where(lane < nblk, gate, NEG_INF)
        sel, _ = _topk_mask(gate, lane, MOBA_TOPK)
        selneg = jnp.where(sel, 0.0, NEG_INF)

        kn_scr[...] = jnp.zeros(kn_scr.shape, F32)
        vn_scr[...] = jnp.zeros(vn_scr.shape, F32)
        kn_scr[0:tq, :] = knew_ref[...]
        vn_scr[0:tq, :] = vnew_ref[...]
        lanef = lane.astype(F32)
        so = _dot_nt(qb, kn_scr[...].astype(BF16)) - slope * (trow - lanef)
        so = jnp.where(jnp.logical_and(lanef <= trow, lane < tq), so, NEG_INF)
        m_run = jnp.max(so, axis=-1, keepdims=True)
        po = jnp.exp(so - m_run)
        den = jnp.sum(po, axis=-1, keepdims=True)
        num = _diag_extract(_dot(po.astype(BF16), vn_scr[...].astype(BF16)), n_kv, rpk)

        slope1 = slope[:, 0:1]
        base = -slope1 * (float(past_len) + trow[:, 0:1])

        def body(j, carry):
            m_run, num, den = carry
            tile = st_scr[j]
            selcol = jnp.max(jnp.where(lane == j, selneg, NEG_INF), axis=-1, keepdims=True)
            mj = tile[:, hd:hd + 1] + (base + slope1 * jnp.asarray(j * bs, dtype=F32)) + selcol
            lj = tile[:, hd + 1:hd + 2]
            m_new = jnp.maximum(m_run, mj)
            a = jnp.exp(m_run - m_new)
            b = jnp.exp(mj - m_new)
            return m_new, a * num + b * tile, a * den + b * lj

        m_run, num, den = lax.fori_loop(0, nblk, body, (m_run, num, den))
        o_ref[0] = num / den


def _attn_sample(cache_kt, cache_vt, layer, page_table, qbd, slope_rows, t_rows, k_new, v_new, tq, n_kv, past_len):
    nbat, rows, kvw = qbd.shape
    page = cache_kt.shape[4]
    hd = cache_kt.shape[3]
    bs = MOBA_BLOCK
    nblk = past_len // bs
    bps = 4
    while nblk % bps:
        bps //= 2
    n_steps = nblk // bps
    assert bs == 2 * page and nblk <= LANES
    npg = 2 * bps

    def page_spec(r):
        return pl.BlockSpec((None, 1, n_kv, hd, page), lambda b, s, pt: (layer, pt[b, s * npg + r], 0, 0, 0))

    in_specs = ([page_spec(r) for r in range(npg)] + [page_spec(r) for r in range(npg)]
                + [pl.BlockSpec((1, rows, kvw), lambda b, s, pt: (b, 0, 0)),
                   pl.BlockSpec((rows, LANES), lambda b, s, pt: (0, 0)),
                   pl.BlockSpec((rows, LANES), lambda b, s, pt: (0, 0)),
                   pl.BlockSpec((tq, kvw), lambda b, s, pt: (b, 0)),
                   pl.BlockSpec((tq, kvw), lambda b, s, pt: (b, 0))])
    grid_spec = pltpu.PrefetchScalarGridSpec(
        num_scalar_prefetch=1,
        grid=(nbat, n_steps),
        in_specs=in_specs,
        out_specs=pl.BlockSpec((1, rows, LANES), lambda b, s, pt: (b, 0, 0)),
        scratch_shapes=[pltpu.VMEM((nblk, rows, LANES), F32),
                        pltpu.VMEM((kvw, LANES), F32),
                        pltpu.VMEM((LANES, kvw), F32),
                        pltpu.VMEM((LANES, kvw), F32)])
    return pl.pallas_call(
        functools.partial(_attn_sample_kernel, bps, n_kv, tq, past_len),
        out_shape=jax.ShapeDtypeStruct((nbat, rows, LANES), F32),
        grid_spec=grid_spec,
        compiler_params=_cparams(2),
        name="attn_sample",
        cost_estimate=pl.CostEstimate(flops=2 * 2 * nbat * rows * past_len * kvw,
                                      transcendentals=nbat * rows * past_len,
                                      bytes_accessed=2 * 4 * nbat * past_len * kvw),
    )(page_table, *([cache_kt] * npg), *([cache_vt] * npg), qbd, slope_rows, t_rows, k_new, v_new)


def _outproj_kernel(alpha, n_exp, ys_ref, ya_ref, gs_ref, ga_ref, x_ref, gm_ref, scf_ref, shf_ref,
                    wos_ref, woa_ref, wout_ref, g1_ref, b1_ref, wr_ref, br_ref,
                    x1_ref, hf_ref, meta_ref, cnt_ref, cnt_scr):
    tm = x_ref.shape[0]
    merged = (jax.nn.sigmoid(gs_ref[...]) * _dot(ys_ref[...].astype(BF16), wos_ref[...])
              + jax.nn.sigmoid(ga_ref[...]) * _dot(ya_ref[...], woa_ref[...]))
    upd = _dot(merged.astype(BF16), wout_ref[...])
    x1 = _layer_norm(alpha * x_ref[...] + gm_ref[0] * upd, g1_ref[...], b1_ref[...])
    x1_ref[...] = x1
    hf = x1 * scf_ref[0] + shf_ref[0]
    hf_ref[...] = hf
    logits = _dot_precise(hf, wr_ref[...]) + br_ref[...]
    lane = lax.broadcasted_iota(jnp.int32, logits.shape, 1)
    logits = jnp.where(lane < n_exp, logits, NEG_INF)
    g = logits
    picks, vals, ids = [], [], []
    for _ in range(TOP_K):
        m = jnp.max(g, axis=-1, keepdims=True)
        idx = jnp.min(jnp.where(g == m, lane, LANES), axis=-1, keepdims=True)
        pick = lane == idx
        g = jnp.where(pick, NEG_INF, g)
        picks.append(pick)
        vals.append(m)
        ids.append(idx)
    es = [jnp.exp(v - vals[0]) for v in vals]
    inv = 1.0 / (es[0] + es[1] + es[2] + es[3])

    @pl.when(pl.program_id(0) == 0)
    def _():
        cnt_scr[...] = jnp.zeros(cnt_scr.shape, F32)

    onehot = jnp.zeros(logits.shape, F32)
    for pick in picks:
        onehot = onehot + jnp.where(pick, 1.0, 0.0)
    ri = lax.broadcasted_iota(jnp.int32, (tm, tm), 0)
    ci = lax.broadcasted_iota(jnp.int32, (tm, tm), 1)
    ltri = jnp.where(ri > ci, 1.0, 0.0).astype(BF16)
    rank_mat = _dot(ltri, onehot.astype(BF16)) + cnt_scr[0:1, :]
    cnt_new = cnt_scr[0:1, :] + jnp.sum(onehot, axis=0, keepdims=True)
    cnt_scr[...] = jnp.broadcast_to(cnt_new, cnt_scr.shape)
    cnt_ref[...] = jnp.broadcast_to(cnt_new, cnt_ref.shape)
    meta = jnp.zeros(logits.shape, F32)
    for k in range(TOP_K):
        rank_k = jnp.sum(jnp.where(picks[k], rank_mat, 0.0), axis=-1, keepdims=True)
        meta = jnp.where(lane == k, ids[k].astype(F32), meta)
        meta = jnp.where(lane == TOP_K + k, rank_k, meta)
        meta = jnp.where(lane == 2 * TOP_K + k, es[k] * inv, meta)
    meta_ref[...] = meta


def _outproj(alpha, n_exp, ys, ya, proj, x, gm, scf1, shf, wos, woa, wout, g1, b1, wr, br, tm, tiles_per_group,
             gs_blk, ga_blk):
    n, d = x.shape
    r = gm.shape[1]
    grp = lambda i: (i // tiles_per_group, 0, 0)
    const = lambda i: (0, 0)
    return pl.pallas_call(
        functools.partial(_outproj_kernel, alpha, n_exp),
        out_shape=(jax.ShapeDtypeStruct((n, d), F32),
                   jax.ShapeDtypeStruct((n, d), F32),
                   jax.ShapeDtypeStruct((n, LANES), F32),
                   jax.ShapeDtypeStruct((SUBLANES, LANES), F32)),
        grid=(n // tm,),
        in_specs=[pl.BlockSpec((tm, ys.shape[1]), lambda i: (i, 0)),
                  pl.BlockSpec((tm, ya.shape[1]), lambda i: (i, 0)),
                  pl.BlockSpec((tm, d), lambda i: (i, gs_blk)),
                  pl.BlockSpec((tm, d), lambda i: (i, ga_blk)),
                  pl.BlockSpec((tm, d), lambda i: (i, 0)),
                  pl.BlockSpec((1, r, d), grp),
                  pl.BlockSpec((1, r, d), grp),
                  pl.BlockSpec((1, r, d), grp),
                  pl.BlockSpec(wos.shape, const),
                  pl.BlockSpec(woa.shape, const),
                  pl.BlockSpec(wout.shape, const),
                  pl.BlockSpec((1, d), const),
                  pl.BlockSpec((1, d), const),
                  pl.BlockSpec((d, LANES), const),
                  pl.BlockSpec((1, LANES), const)],
        out_specs=(pl.BlockSpec((tm, d), lambda i: (i, 0)),
                   pl.BlockSpec((tm, d), lambda i: (i, 0)),
                   pl.BlockSpec((tm, LANES), lambda i: (i, 0)),
                   pl.BlockSpec((SUBLANES, LANES), const)),
        scratch_shapes=[pltpu.VMEM((SUBLANES, LANES), F32)],
        compiler_params=_cparams(1),
        name="outproj",
        cost_estimate=pl.CostEstimate(flops=2 * n * d * (ys.shape[1] + ya.shape[1] + d + 3 * LANES + tm),
                                      transcendentals=2 * n * d,
                                      bytes_accessed=n * (2 * ys.shape[1] + 2 * ya.shape[1] + 4 * 5 * d)),
    )(ys, ya, proj, proj, x, gm, scf1, shf, wos, woa, wout, g1, b1, wr, br)


def _sc_gather(table, idx):
    _, dcol = table.shape
    b = idx.shape[0]
    nw = SC_CORES * SC_SUBCORES
    ch = SC_GATHER_ROWS
    assert b % (nw * 2 * ch) == 0
    b_per_w = b // nw
    n_ch = b_per_w // ch
    mesh = plsc.VectorSubcoreMesh(core_axis_name="c", subcore_axis_name="s")

    @functools.partial(
        pl.kernel, mesh=mesh,
        out_type=jax.ShapeDtypeStruct((b, dcol), table.dtype),
        scratch_types=[pltpu.VMEM((ch,), jnp.int32), pltpu.VMEM((ch,), jnp.int32),
                       pltpu.VMEM((ch, dcol), table.dtype), pltpu.VMEM((ch, dcol), table.dtype),
                       pltpu.SemaphoreType.DMA, pltpu.SemaphoreType.DMA],
        name="sc_gather",
        cost_estimate=pl.CostEstimate(flops=0, transcendentals=0,
                                      bytes_accessed=2 * b * dcol * table.dtype.itemsize + 4 * b),
    )
    def gather(table_hbm, idx_hbm, out_hbm, idx0, idx1, rows0, rows1, sem0, sem1):
        idx_v, rows_v, sems = (idx0, idx1), (rows0, rows1), (sem0, sem1)
        wid = lax.axis_index("s") * SC_CORES + lax.axis_index("c")
        base = wid * b_per_w

        def start(c, slot):
            off = pl.multiple_of(base + c * ch, ch)
            pltpu.sync_copy(idx_hbm.at[pl.ds(off, ch)], idx_v[slot])
            pltpu.async_copy(table_hbm.at[idx_v[slot]], rows_v[slot], sems[slot])

        def finish(c, slot):
            pltpu.make_async_copy(table_hbm.at[idx_v[slot]], rows_v[slot], sems[slot]).wait()
            off = pl.multiple_of(base + c * ch, ch)
            pltpu.sync_copy(rows_v[slot], out_hbm.at[pl.ds(off, ch)])

        start(0, 0)

        @pl.loop(0, n_ch, step=2)
        def _(c):
            start(c + 1, 1)
            finish(c, 0)

            @pl.when(c + 2 < n_ch)
            def _():
                start(c + 2, 0)

            finish(c + 1, 1)

    return gather(table, idx)


def _moe_experts_kernel(d_ff, te_ref, x_ref, wgu_ref, bgu_ref, wdn_ref, bdn_ref, o_ref, wgu_scr, wdn_scr):
    i = pl.program_id(0)
    prev = te_ref[jnp.maximum(i - 1, 0)]

    @pl.when(jnp.logical_or(i == 0, te_ref[i] != prev))
    def _():
        wgu_scr[...] = wgu_ref[0].astype(BF16)
        wdn_scr[...] = wdn_ref[0].astype(BF16)

    gu = _dot(x_ref[...].astype(BF16), wgu_scr[...]) + bgu_ref[0]
    gg = jnp.minimum(gu[:, :d_ff], SWIGLU_LIMIT)
    uu = jnp.clip(gu[:, d_ff:], -SWIGLU_LIMIT, SWIGLU_LIMIT)
    act = (uu + 1.0) * (gg * jax.nn.sigmoid(SWIGLU_ALPHA * gg))
    o_ref[...] = _dot(act.astype(BF16), wdn_scr[...]) + bdn_ref[0]


def _moe_experts(xs, tile_expert, layer, w_gu, b_gu4, w_down, b_down4, tme):
    p, d = xs.shape
    two_ff = w_gu.shape[3]
    d_ff = two_ff // 2
    grid_spec = pltpu.PrefetchScalarGridSpec(
        num_scalar_prefetch=1,
        grid=(p // tme,),
        in_specs=[pl.BlockSpec((tme, d), lambda i, te: (i, 0)),
                  pl.BlockSpec((None, 1, d, two_ff), lambda i, te: (layer, te[i], 0, 0)),
                  pl.BlockSpec((None, 1, 1, two_ff), lambda i, te: (layer, te[i], 0, 0)),
                  pl.BlockSpec((None, 1, d_ff, d), lambda i, te: (layer, te[i], 0, 0)),
                  pl.BlockSpec((None, 1, 1, d), lambda i, te: (layer, te[i], 0, 0))],
        out_specs=pl.BlockSpec((tme, d), lambda i, te: (i, 0)),
        scratch_shapes=[pltpu.VMEM((d, two_ff), BF16),
                        pltpu.VMEM((d_ff, d), BF16)])
    return pl.pallas_call(
        functools.partial(_moe_experts_kernel, d_ff),
        out_shape=jax.ShapeDtypeStruct((p, d), F32),
        grid_spec=grid_spec,
        compiler_params=_cparams(1),
        name="moe_experts",
        cost_estimate=pl.CostEstimate(flops=2 * p * d * 3 * d_ff, transcendentals=p * d_ff,
                                      bytes_accessed=8 * p * d + 4 * w_gu.shape[1] * 3 * d * d_ff),
    )(tile_expert, xs, w_gu, b_gu4, w_down, b_down4)


def _moe_combine_kernel(alpha, *refs):
    yk_refs = refs[:TOP_K]
    meta_ref, x1_ref, gf_ref, g2_ref, b2_ref, o_ref = refs[TOP_K:]
    meta = meta_ref[...]
    acc = meta[:, 2 * TOP_K:2 * TOP_K + 1] * yk_refs[0][...]
    for k in range(1, TOP_K):
        acc = acc + meta[:, 2 * TOP_K + k:2 * TOP_K + k + 1] * yk_refs[k][...]
    o_ref[...] = _layer_norm(alpha * x1_ref[...] + gf_ref[0] * acc, g2_ref[...], b2_ref[...])


def _moe_combine(alpha, yk, n_pad, meta, x1, gf, g2, b2, tm, tiles_per_group):
    n, d = x1.shape
    r = gf.shape[1]
    bpk = n_pad // tm
    assert n_pad % tm == 0

    def choice_spec(k):
        return pl.BlockSpec((tm, d), lambda i: (k * bpk + i, 0))

    return pl.pallas_call(
        functools.partial(_moe_combine_kernel, alpha),
        out_shape=jax.ShapeDtypeStruct((n, d), F32),
        grid=(n // tm,),
        in_specs=[choice_spec(k) for k in range(TOP_K)] + [
                  pl.BlockSpec((tm, LANES), lambda i: (i, 0)),
                  pl.BlockSpec((tm, d), lambda i: (i, 0)),
                  pl.BlockSpec((1, r, d), lambda i: (i // tiles_per_group, 0, 0)),
                  pl.BlockSpec((1, d), lambda i: (0, 0)),
                  pl.BlockSpec((1, d), lambda i: (0, 0))],
        out_specs=pl.BlockSpec((tm, d), lambda i: (i, 0)),
        compiler_params=_cparams(1),
        name="moe_combine",
        cost_estimate=pl.CostEstimate(flops=2 * TOP_K * n * d, transcendentals=n,
                                      bytes_accessed=4 * n * d * (TOP_K + 2)),
    )(*([yk] * TOP_K), meta, x1, gf, g2, b2)


def _moe_route(meta, counts, tile, tm):
    n = meta.shape[0]
    n_exp = counts.shape[0]
    gran = SC_CORES * SC_SUBCORES * 2 * SC_GATHER_ROWS
    unit = gran * tile // math.gcd(gran, tile)
    eid = meta[:, 0:TOP_K].astype(jnp.int32)
    rank = meta[:, TOP_K:2 * TOP_K].astype(jnp.int32)
    padded = -(-counts // tile) * tile
    ends = jnp.cumsum(padded)
    is_e = eid[:, :, None] == jnp.arange(n_exp, dtype=jnp.int32)
    dest = jnp.sum(jnp.where(is_e, (ends - padded)[None, None, :], 0), axis=-1) + rank
    p_rows = -(-(n * TOP_K + n_exp * tile) // unit) * unit
    tok = jnp.broadcast_to(jnp.arange(n, dtype=jnp.int32)[:, None], (n, TOP_K))
    src_tok = jnp.zeros((p_rows,), jnp.int32).at[dest.reshape(-1)].set(
        tok.reshape(-1), unique_indices=True, mode="promise_in_bounds")
    tile_start = jnp.arange(p_rows // tile, dtype=jnp.int32) * tile
    tile_expert = jnp.minimum(jnp.sum(tile_start[:, None] >= ends[None, :], axis=-1), n_exp - 1).astype(jnp.int32)
    quant = max(gran // TOP_K, tm)
    n_pad = -(-n // quant) * quant
    dest_km = jnp.pad(dest.T, ((0, 0), (0, n_pad - n))).reshape(-1)
    return dict(src_tok=src_tok, tile_expert=tile_expert, dest_km=dest_km, n_pad=n_pad, tile=tile)


def _moe_group(alpha, layer, route, hf, meta, x1, gf, g2, b2, w_gu, b_gu4, w_down, b_down4, tm, tiles_per_group):
    xs_sorted = _sc_gather(hf, route["src_tok"])
    ys_sorted = _moe_experts(xs_sorted, route["tile_expert"], layer, w_gu, b_gu4, w_down, b_down4, route["tile"])
    yk = _sc_gather(ys_sorted, route["dest_km"])
    return yk, functools.partial(_moe_combine, alpha, n_pad=route["n_pad"], meta=meta, x1=x1, gf=gf, g2=g2, b2=b2,
                                 tm=tm, tiles_per_group=tiles_per_group)


def _pad_lanes(v, fill=0.0):
    return jnp.pad(v, (0, LANES - v.shape[0]), constant_values=fill).reshape(1, LANES)


def kernel(x_prompt, x_sample, c_prompt, c_sample, cache_k, cache_v, state_ssm, state_conv, page_table, w_ada, b_ada, w_in, conv_w, conv_b, dt_bias, a_log, d_skip, ssm_norm_g, w_o_ssm, w_o_attn, w_out, ln1_g, ln1_b, ln2_g, ln2_b, w_router, b_router, w_gu, b_gu, w_down, b_down):
    depth, d, _ = w_ada.shape
    bp, tp, _ = x_prompt.shape
    bsm, ts, _ = x_sample.shape
    n_heads = dt_bias.shape[1]
    d_inner = n_heads * SSM_HEAD_DIM
    conv_ch = conv_w.shape[2]
    kvw = cache_k.shape[3] * cache_k.shape[4]
    n_kv = cache_k.shape[3]
    aw = w_o_attn.shape[1]
    n_q = aw // ATTN_HEAD_DIM
    n_exp = w_router.shape[2]
    page = cache_k.shape[2]
    past_len = page_table.shape[1] * page
    alpha = float((2.0 * depth) ** 0.25)
    assert d_inner == 2 * d and aw == d and kvw == d // 2 and n_q == 2 * n_kv
    assert conv_ch == d_inner + 2 * SSM_GROUPS * SSM_STATE and conv_ch == 3 * d
    assert tp % MOBA_BLOCK == 0 and tp % SSD_CHUNK == 0 and ts == SUBLANES and n_heads <= LANES

    o_z, o_xbc, o_dt = 0, d_inner, d_inner + conv_ch
    o_q = o_dt + n_heads
    o_k, o_v = o_q + aw, o_q + aw + kvw
    o_gs, o_ga = o_v + kvw, o_v + kvw + d
    c_z, c_q, c_gs, c_ga = 0, d_inner, d_inner + d, d_inner + 2 * d
    c_k = d_inner + 3 * d
    c_v = c_k + kvw
    c_xbc = c_v + kvw
    assert c_xbc % conv_ch == 0

    slopes = 2.0 ** (-8.0 * (jnp.arange(n_q, dtype=F32) + 1.0) / n_q)

    n_c = bsm + bp
    n_c_pad = -(-n_c // SUBLANES) * SUBLANES
    c_all = jnp.concatenate([c_sample, c_prompt, jnp.zeros((n_c_pad - n_c, d), F32)], axis=0)
    mod = _ada(c_all, w_ada, b_ada)

    grp = n_q // n_kv
    rows = n_kv * grp * ts
    r_idx = jnp.arange(rows)
    r_head = (r_idx // (grp * ts)) * grp + (r_idx // ts) % grp
    slope_rows = jnp.broadcast_to(slopes[r_head][:, None], (rows, LANES))
    t_rows = jnp.broadcast_to((r_idx % ts).astype(F32)[:, None], (rows, LANES))
    eye_kv = jnp.eye(n_kv, dtype=F32)

    cache_kt = jnp.transpose(cache_k, (0, 1, 3, 4, 2))
    cache_vt = jnp.transpose(cache_v, (0, 1, 3, 4, 2))

    xp = x_prompt.reshape(bp * tp, d)
    xs = x_sample.reshape(bsm * ts, d)
    tm_p = 1024 if tp % 1024 == 0 else MOBA_BLOCK
    tm_o = 512 if tp % 512 == 0 else MOBA_BLOCK
    n_s = bsm * ts

    def make_layer(l):
        wl = w_in[l]
        w_main = jnp.concatenate([wl[:, o_z:o_z + d_inner], wl[:, o_q:o_q + aw], wl[:, o_gs:o_gs + d],
                                  wl[:, o_ga:o_ga + d], wl[:, o_k:o_k + kvw], wl[:, o_v:o_v + kvw],
                                  wl[:, o_xbc:o_xbc + conv_ch]], axis=1).astype(BF16)
        w_dt = jnp.pad(wl[:, o_dt:o_dt + n_heads], ((0, 0), (0, LANES - n_heads))).astype(BF16)
        w_dtT = wl[:, o_dt:o_dt + n_heads].T.astype(BF16)
        dtb = _pad_lanes(dt_bias[l])
        dtbT = jnp.broadcast_to(dt_bias[l][:, None], (n_heads, SSD_CHUNK))
        a_neg = -jnp.exp(a_log[l])
        a_row = _pad_lanes(a_neg)
        aT = jnp.broadcast_to(a_neg[:, None], (n_heads, SSD_CHUNK))
        dskip = jnp.repeat(d_skip[l], SSM_HEAD_DIM).reshape(1, d_inner)
        ng = ssm_norm_g[l].reshape(1, d_inner)
        cw = conv_w[l]
        cb = conv_b[l].reshape(1, conv_ch)
        wos = w_o_ssm[l].astype(BF16)
        woa = w_o_attn[l].astype(BF16)
        wout = w_out[l].astype(BF16)
        g1, b1 = ln1_g[l].reshape(1, d), ln1_b[l].reshape(1, d)
        g2, b2 = ln2_g[l].reshape(1, d), ln2_b[l].reshape(1, d)
        wr = jnp.pad(w_router[l], ((0, 0), (0, LANES - n_exp)))
        br = _pad_lanes(b_router[l])

        mod_l = mod[l]
        sh_m, sc_m, g_m, sh_f, sc_f, g_f = [mod_l[:, i * d:(i + 1) * d] for i in range(6)]

        def group_mod(v, is_prompt):
            if is_prompt:
                return v[bsm:bsm + bp].reshape(bp, 1, d)
            return jnp.repeat(v[:bsm], ts, axis=0).reshape(1, n_s, d)

        def run(x, is_prompt, h0T, cprev8):
            bsz, t = (bp, tp) if is_prompt else (bsm, ts)
            n = bsz * t
            tm = tm_p if is_prompt else n
            tmo = tm_o if is_prompt else n
            tpg = (t // tm) if is_prompt else 1
            tpgo = (t // tmo) if is_prompt else 1
            gm = lambda v: group_mod(v, is_prompt)
            proj, dt, dtT = _inproj(x, gm(1.0 + sc_m), gm(sh_m), w_main, w_dt, w_dtT, tm, tpg)
            dtT_seq = dtT.reshape(n_heads, bsz, t).transpose(1, 0, 2)
            lr = SSD_CHUNK if is_prompt else t
            y_s, h_newT = _ssd(proj, dt, dtT_seq, h0T, cprev8, cw, cb, dtb, dtbT, a_row, aT, dskip, ng,
                               bsz, t, lr, c_z // d_inner, c_xbc // conv_ch, d_inner, conv_ch)
            k_new = proj[:, c_k:c_k + kvw]
            v_new = proj[:, c_v:c_v + kvw]
            if is_prompt:
                y_a = _attn_prompt(proj, slopes, bsz, t, n_kv, c_q, c_k, c_v)
            else:
                q5 = proj[:, c_q:c_q + aw].reshape(bsz, t, n_kv, grp, ATTN_HEAD_DIM)
                q5 = q5.transpose(0, 2, 3, 1, 4).reshape(bsz, n_kv, grp * t, ATTN_HEAD_DIM)
                qbd = (q5[:, :, :, None, :] * eye_kv[None, :, None, :, None]).reshape(bsz, rows, kvw)
                o = _attn_sample(cache_kt, cache_vt, l, page_table, qbd, slope_rows, t_rows, k_new, v_new,
                                 t, n_kv, past_len)
                o = o[:, :, :ATTN_HEAD_DIM].reshape(bsz, n_kv, grp, t, ATTN_HEAD_DIM)
                y_a = o.transpose(0, 3, 1, 2, 4).reshape(n, aw).astype(BF16)
            x1, hf, meta, cnt = _outproj(alpha, n_exp, y_s, y_a, proj, x, gm(g_m), gm(1.0 + sc_f), gm(sh_f),
                                         wos, woa, wout, g1, b1, wr, br, tmo, tpgo, c_gs // d, c_ga // d)
            conv_new = proj[:, c_xbc:c_xbc + conv_ch].reshape(bsz, t, conv_ch)[:, t - (CONV_W - 1):]
            h_new = h_newT.transpose(0, 2, 1).reshape(bsz, n_heads, SSM_HEAD_DIM, SSM_STATE)
            hd4 = (bsz, t, n_kv, ATTN_HEAD_DIM)
            return dict(x1=x1, hf=hf, meta=meta, cnt=cnt[0, :n_exp].astype(jnp.int32), gf=gm(g_f),
                        k=k_new.reshape(hd4), v=v_new.reshape(hd4), h=h_new, conv=conv_new)

        return run, g2, b2

    def sample_front(l, x):
        h0_s = state_ssm[l].reshape(bsm, d_inner, SSM_STATE).transpose(0, 2, 1)
        cv0_s = jnp.pad(state_conv[l], ((0, 0), (SUBLANES - (CONV_W - 1), 0), (0, 0)))
        return layers[l][0](x, False, h0_s, cv0_s)

    layers = [make_layer(l) for l in range(depth)]
    b_gu4 = b_gu.reshape(depth, n_exp, 1, -1)
    b_down4 = b_down.reshape(depth, n_exp, 1, d)
    h0_p = jnp.zeros((bp, SSM_STATE, d_inner), F32)
    cv0_p = jnp.zeros((bp, SUBLANES, conv_ch), F32)
    outs = {k: [] for k in ("kp", "vp", "hp", "cvp", "ks", "vs", "hs", "cvs")}
    fs = None
    for l in range(depth):
        run, g2, b2 = layers[l]
        fp = run(xp, True, h0_p, cv0_p)
        route_p = _moe_route(fp["meta"], fp["cnt"], MOE_TILE, MOBA_BLOCK)
        fs = sample_front(l, xs)
        outs["kp"].append(fp["k"]); outs["vp"].append(fp["v"]); outs["hp"].append(fp["h"]); outs["cvp"].append(fp["conv"])
        outs["ks"].append(fs["k"]); outs["vs"].append(fs["v"]); outs["hs"].append(fs["h"]); outs["cvs"].append(fs["conv"])

        yk_p, combine_p = _moe_group(alpha, l, route_p, fp["hf"], fp["meta"], fp["x1"], fp["gf"], g2, b2,
                                     w_gu, b_gu4, w_down, b_down4, MOBA_BLOCK, tp // MOBA_BLOCK)
        route_s = _moe_route(fs["meta"], fs["cnt"], MOE_TILE_SMALL, n_s)
        yk_s, combine_s = _moe_group(alpha, l, route_s, fs["hf"], fs["meta"], fs["x1"], fs["gf"], g2, b2,
                                     w_gu, b_gu4, w_down, b_down4, n_s, 1)
        xs = combine_s(yk_s)
        xp = combine_p(yk_p)

    st = lambda k: jnp.stack(outs[k])
    return (xp.reshape(bp, tp, d), xs.reshape(bsm, ts, d), st("kp"), st("vp"), st("hp"), st("cvp"),
            st("ks"), st("vs"), st("hs"), st("cvs"))
```

```python
import functools
import math

import jax
import jax.numpy as jnp
from jax import lax
from jax.experimental import pallas as pl
from jax.experimental.pallas import tpu as pltpu
from jax.experimental.pallas import tpu_sc as plsc

F32 = jnp.float32
BF16 = jnp.bfloat16
NEG_INF = float("-inf")

LANES = 128
SUBLANES = 8
VMEM_LIMIT = 56 * 1024 * 1024
SC_CORES = 2
SC_SUBCORES = 16
SC_GATHER_ROWS = 32
MOE_TILE = 512
MOE_TILE_SMALL = 64

SSM_HEAD_DIM = 64
SSM_GROUPS = 4
SSM_STATE = 128
CONV_W = 4
SSD_CHUNK = 128
RMS_EPS = 1e-5
ATTN_HEAD_DIM = 64
MOBA_BLOCK = 256
MOBA_TOPK = 3
TOP_K = 4
SWIGLU_LIMIT = 7.0
SWIGLU_ALPHA = 1.702
LN_EPS = 1e-5


def _cparams(n_axes):
    return pltpu.CompilerParams(dimension_semantics=("arbitrary",) * n_axes,
                                vmem_limit_bytes=VMEM_LIMIT)


def _dot(a, b):
    return jnp.dot(a, b, preferred_element_type=F32)


def _dot_nt(a, b):
    return lax.dot_general(a, b, (((1,), (1,)), ((), ())), preferred_element_type=F32)


def _split2(a):
    hi = a.astype(BF16)
    lo = (a - hi.astype(F32)).astype(BF16)
    return hi, lo


def _split3(a):
    hi = a.astype(BF16)
    r = a - hi.astype(F32)
    mid = r.astype(BF16)
    lo = (r - mid.astype(F32)).astype(BF16)
    return hi, mid, lo


def _dot_nt_precise(a, b):
    ah, al = _split2(a)
    bh, bl = _split2(b)
    return _dot_nt(ah, bh) + (_dot_nt(ah, bl) + _dot_nt(al, bh))


def _dot_precise(a, b):
    ah, al = _split2(a)
    bh, bl = _split2(b)
    return _dot(ah, bh) + (_dot(ah, bl) + _dot(al, bh))


def _silu(x):
    return x * jax.nn.sigmoid(x)


def _softplus(x):
    return jnp.maximum(x, 0.0) + jnp.log1p(jnp.exp(-jnp.abs(x)))


def _layer_norm(x, g, b):
    mu = jnp.mean(x, axis=-1, keepdims=True)
    xc = x - mu
    var = jnp.mean(xc * xc, axis=-1, keepdims=True)
    return xc * lax.rsqrt(var + LN_EPS) * g + b


def _topk_mask(score, lane, k):
    width = score.shape[-1]
    sel = jnp.zeros(score.shape, dtype=jnp.bool_)
    vals = []
    g = score
    for _ in range(k):
        m = jnp.max(g, axis=-1, keepdims=True)
        idx = jnp.min(jnp.where(g == m, lane, width), axis=-1, keepdims=True)
        pick = jnp.logical_and(lane == idx, m > NEG_INF)
        sel = jnp.logical_or(sel, pick)
        g = jnp.where(pick, NEG_INF, g)
        vals.append(m)
    return sel, vals


def _ada_kernel(c_ref, w_ref, b_ref, o_ref):
    s = _silu(c_ref[...]).astype(BF16)
    o_ref[0] = _dot(s, w_ref[0].astype(BF16)) + b_ref[0]


def _ada(c_all, w_ada, b_ada):
    depth, d, n6 = w_ada.shape
    rows = c_all.shape[0]
    tn = 1024
    return pl.pallas_call(
        _ada_kernel,
        out_shape=jax.ShapeDtypeStruct((depth, rows, n6), F32),
        grid=(depth, n6 // tn),
        in_specs=[pl.BlockSpec((rows, d), lambda l, j: (0, 0)),
                  pl.BlockSpec((1, d, tn), lambda l, j: (l, 0, j)),
                  pl.BlockSpec((1, 1, tn), lambda l, j: (l, 0, j))],
        out_specs=pl.BlockSpec((1, rows, tn), lambda l, j: (l, 0, j)),
        compiler_params=_cparams(2),
        name="ada",
    )(c_all, w_ada, b_ada.reshape(depth, 1, n6))


def _inproj_kernel(j_kv, kvw, x_ref, sc_ref, sh_ref, w_ref, wdt_ref, wdtT_ref, o_ref, dt_ref, dtT_ref, k_ref, v_ref,
                   h_scr):
    @pl.when(pl.program_id(1) == 0)
    def _():
        hb = (x_ref[...] * sc_ref[0] + sh_ref[0]).astype(BF16)
        h_scr[...] = hb
        dt_ref[...] = _dot(hb, wdt_ref[...])
        dtT_ref[...] = _dot_nt(wdtT_ref[...], hb)

    res = _dot(h_scr[...], w_ref[...])
    o_ref[...] = res

    @pl.when(pl.program_id(1) == j_kv)
    def _():
        k_ref[...] = res[:, 0:kvw]
        v_ref[...] = res[:, kvw:2 * kvw]


def _inproj(x, sc1, sh, w_main, w_dt, w_dtT, tm, tiles_per_group, k_col, kvw):
    n, d = x.shape
    n_main = w_main.shape[1]
    tn = 1024
    r = sc1.shape[1]
    nh = w_dtT.shape[0]
    assert k_col % tn == 0 and 2 * kvw <= tn
    grp = lambda i, j: (i // tiles_per_group, 0, 0)
    return pl.pallas_call(
        functools.partial(_inproj_kernel, k_col // tn, kvw),
        out_shape=(jax.ShapeDtypeStruct((n, n_main), F32),
                   jax.ShapeDtypeStruct((n, LANES), F32),
                   jax.ShapeDtypeStruct((nh, n), F32),
                   jax.ShapeDtypeStruct((n, kvw), F32),
                   jax.ShapeDtypeStruct((n, kvw), F32)),
        grid=(n // tm, n_main // tn),
        in_specs=[pl.BlockSpec((tm, d), lambda i, j: (i, 0)),
                  pl.BlockSpec((1, r, d), grp),
                  pl.BlockSpec((1, r, d), grp),
                  pl.BlockSpec((d, tn), lambda i, j: (0, j)),
                  pl.BlockSpec((d, LANES), lambda i, j: (0, 0)),
                  pl.BlockSpec((nh, d), lambda i, j: (0, 0))],
        out_specs=(pl.BlockSpec((tm, tn), lambda i, j: (i, j)),
                   pl.BlockSpec((tm, LANES), lambda i, j: (i, 0)),
                   pl.BlockSpec((nh, tm), lambda i, j: (0, i)),
                   pl.BlockSpec((tm, kvw), lambda i, j: (i, 0)),
                   pl.BlockSpec((tm, kvw), lambda i, j: (i, 0))),
        scratch_shapes=[pltpu.VMEM((tm, d), BF16)],
        compiler_params=_cparams(2),
        name="inproj",
        cost_estimate=pl.CostEstimate(flops=2 * n * d * (n_main + LANES + nh), transcendentals=0,
                                      bytes_accessed=4 * n * (d + n_main) + 2 * d * n_main * (n // tm)),
    )(x, sc1, sh, w_main, w_dt, w_dtT)


def _ssd_kernel(lr, d_inner, xbc_ref, z_ref, dt_ref, dtT_ref, h0_ref, cprev_ref, cw_ref, cb_ref,
                dtb_ref, dtbT_ref, a_ref, aT_ref, dskip_ref, ng_ref,
                y_ref, hout_ref, xp_scr, h_scr, y_scr, dt_scr, dtT_scr):
    L = SSD_CHUNK
    gs = SSM_GROUPS * SSM_STATE
    n_heads = d_inner // SSM_HEAD_DIM
    pairs_per_group = n_heads // SSM_GROUPS // 2
    c = pl.program_id(1)

    @pl.when(c == 0)
    def _():
        xp_scr[0:SUBLANES, :] = cprev_ref[0]
        h_scr[...] = h0_ref[0]

    if lr < L:
        xp_scr[SUBLANES + lr:SUBLANES + L, :] = jnp.zeros((L - lr, xp_scr.shape[1]), F32)
    xp_scr[SUBLANES:SUBLANES + lr, :] = xbc_ref[...]

    acc = cb_ref[...] + cw_ref[0:1, :] * xp_scr[SUBLANES - 3:SUBLANES - 3 + L, :]
    for j in range(1, CONV_W):
        acc = acc + cw_ref[j:j + 1, :] * xp_scr[SUBLANES - 3 + j:SUBLANES - 3 + j + L, :]
    tail = xp_scr[lr:lr + SUBLANES, :]
    xp_scr[0:SUBLANES, :] = tail
    xc = _silu(acc)

    if lr < L:
        dt_scr[...] = jnp.zeros(dt_scr.shape, F32)
        dt_scr[0:lr, :] = dt_ref[...]
        dtT_scr[...] = jnp.zeros(dtT_scr.shape, F32)
        dtT_scr[:, 0:lr] = dtT_ref[0]
        dt_raw = dt_scr[...]
        dtT_raw = dtT_scr[...]
    else:
        dt_raw = dt_ref[...]
        dtT_raw = dtT_ref[0]
    row = lax.broadcasted_iota(jnp.int32, (L, LANES), 0)
    col = lax.broadcasted_iota(jnp.int32, (n_heads, L), 1)
    dtv = jnp.where(row < lr, _softplus(dt_raw + dtb_ref[...]), 0.0)
    dtvT = jnp.where(col < lr, _softplus(dtT_raw + dtbT_ref[...]), 0.0)
    dA = dtv * a_ref[...]
    dAT = dtvT * aT_ref[...]

    ri = lax.broadcasted_iota(jnp.int32, (L, L), 0)
    ci = lax.broadcasted_iota(jnp.int32, (L, L), 1)
    causal = ri >= ci
    tri = jnp.where(causal, 1.0, 0.0).astype(BF16)
    triT = jnp.where(ri <= ci, 1.0, 0.0).astype(BF16)
    d1, d2, d3 = _split3(dA)
    la = _dot(tri, d1) + (_dot(tri, d2) + _dot(tri, d3))
    e1, e2, e3 = _split3(dAT)
    laT = _dot(e1, triT) + (_dot(e2, triT) + _dot(e3, triT))

    la_last = la[L - 1:L, :]
    to_end = jnp.exp(la_last - la)
    ela = jnp.exp(la)
    cdec = jnp.exp(la_last)

    lane = lax.broadcasted_iota(jnp.int32, (L, LANES), 1)
    lo_half = lane < SSM_HEAD_DIM
    lane1 = lax.broadcasted_iota(jnp.int32, (1, LANES), 1)
    eye = jnp.where(ri == ci, 1.0, 0.0).astype(BF16)

    def pair_bcast(arr, p):
        rows = arr.shape[0]
        a0 = jnp.broadcast_to(arr[:, 2 * p:2 * p + 1], (rows, LANES))
        a1 = jnp.broadcast_to(arr[:, 2 * p + 1:2 * p + 2], (rows, LANES))
        return jnp.where(lo_half[:rows], a0, a1)

    for g in range(SSM_GROUPS):
        bg = xc[:, d_inner + g * SSM_STATE:d_inner + (g + 1) * SSM_STATE].astype(BF16)
        cg = xc[:, d_inner + gs + g * SSM_STATE:d_inner + gs + (g + 1) * SSM_STATE].astype(BF16)
        cb = _dot_nt(cg, bg)
        bgT = _dot_nt(eye, bg).astype(BF16)
        for pp in range(pairs_per_group):
            p = g * pairs_per_group + pp
            sl = slice(p * LANES, (p + 1) * LANES)
            xs_p = xc[:, sl]
            xdt_p = xs_p * pair_bcast(dtv, p)
            ydiag = None
            for a in range(2):
                h = 2 * p + a
                seg = jnp.broadcast_to(la[:, h:h + 1], (L, L)) - laT[h:h + 1, :]
                decay = jnp.exp(jnp.where(causal, seg, NEG_INF))
                m = (cb * decay).astype(BF16)
                half = lo_half if a == 0 else jnp.logical_not(lo_half)
                part = _dot(m, jnp.where(half, xdt_p, 0.0).astype(BF16))
                ydiag = part if ydiag is None else ydiag + part
            hin = h_scr[:, sl]
            yoff = _dot(cg, hin.astype(BF16)) * pair_bcast(ela, p)
            y_scr[:, sl] = ydiag + yoff + dskip_ref[:, sl] * xs_p
            s_new = _dot(bgT, (xdt_p * pair_bcast(to_end, p)).astype(BF16))
            cd_p = jnp.where(lane1 < SSM_HEAD_DIM,
                             jnp.broadcast_to(cdec[:, 2 * p:2 * p + 1], (1, LANES)),
                             jnp.broadcast_to(cdec[:, 2 * p + 1:2 * p + 2], (1, LANES)))
            h_scr[:, sl] = hin * cd_p + s_new

    gw = d_inner // SSM_GROUPS
    for g in range(SSM_GROUPS):
        sl = slice(g * gw, (g + 1) * gw)
        yg = y_scr[0:lr, sl] * _silu(z_ref[:, sl])
        ms = jnp.mean(yg * yg, axis=-1, keepdims=True)
        y_ref[:, sl] = (yg * lax.rsqrt(ms + RMS_EPS) * ng_ref[:, sl]).astype(y_ref.dtype)

    @pl.when(c == pl.num_programs(1) - 1)
    def _():
        hout_ref[0] = h_scr[...]


def _ssd(proj, dt, dtT_seq, h0T, cprev8, cw, cb, dtb, dtbT, a, aT, dskip, ng, bsz, t, lr,
         z_blk, xbc_blk, d_inner, conv_ch):
    n = proj.shape[0]
    nc = t // lr
    n_heads = d_inner // SSM_HEAD_DIM
    L = SSD_CHUNK
    const2 = lambda b, c: (0, 0)
    return pl.pallas_call(
        functools.partial(_ssd_kernel, lr, d_inner),
        out_shape=(jax.ShapeDtypeStruct((n, d_inner), BF16 if lr % 16 == 0 else F32),
                   jax.ShapeDtypeStruct((bsz, SSM_STATE, d_inner), F32)),
        grid=(bsz, nc),
        in_specs=[pl.BlockSpec((lr, conv_ch), lambda b, c: (b * nc + c, xbc_blk)),
                  pl.BlockSpec((lr, d_inner), lambda b, c: (b * nc + c, z_blk)),
                  pl.BlockSpec((lr, LANES), lambda b, c: (b * nc + c, 0)),
                  pl.BlockSpec((1, n_heads, lr), lambda b, c: (b, 0, c)),
                  pl.BlockSpec((1, SSM_STATE, d_inner), lambda b, c: (b, 0, 0)),
                  pl.BlockSpec((1, SUBLANES, conv_ch), lambda b, c: (b, 0, 0)),
                  pl.BlockSpec((CONV_W, conv_ch), const2),
                  pl.BlockSpec((1, conv_ch), const2),
                  pl.BlockSpec((1, LANES), const2),
                  pl.BlockSpec((n_heads, L), const2),
                  pl.BlockSpec((1, LANES), const2),
                  pl.BlockSpec((n_heads, L), const2),
                  pl.BlockSpec((1, d_inner), const2),
                  pl.BlockSpec((1, d_inner), const2)],
        out_specs=(pl.BlockSpec((lr, d_inner), lambda b, c: (b * nc + c, 0)),
                   pl.BlockSpec((1, SSM_STATE, d_inner), lambda b, c: (b, 0, 0))),
        scratch_shapes=[pltpu.VMEM((SUBLANES + L, conv_ch), F32),
                        pltpu.VMEM((SSM_STATE, d_inner), F32),
                        pltpu.VMEM((L, d_inner), F32),
                        pltpu.VMEM((L, LANES), F32),
                        pltpu.VMEM((n_heads, L), F32)],
        compiler_params=_cparams(2),
        name="ssd",
        cost_estimate=pl.CostEstimate(
            flops=2 * bsz * nc * L * (SSM_GROUPS * L * SSM_STATE + d_inner * (L + 2 * SSM_STATE)),
            transcendentals=bsz * nc * L * (n_heads * L + conv_ch + d_inner),
            bytes_accessed=4 * n * (conv_ch + d_inner) + 2 * n * d_inner + 8 * bsz * SSM_STATE * d_inner),
    )(proj, proj, dt, dtT_seq, h0T, cprev8, cw, cb, dtb, dtbT, a, aT, dskip, ng)


def _topk_mask_rows(score, row, k):
    big = score.shape[0]
    sel = jnp.zeros(score.shape, dtype=jnp.bool_)
    g = score
    for _ in range(k):
        m = jnp.max(g, axis=0, keepdims=True)
        idx = jnp.min(jnp.where(g == m, row, big), axis=0, keepdims=True)
        pick = jnp.logical_and(row == idx, m > NEG_INF)
        sel = jnp.logical_or(sel, pick)
        g = jnp.where(pick, NEG_INF, g)
    return sel


def _attn_prompt_kernel(nb, slopes_ref, q_ref, k_ref, v_ref, o_ref,
                        kb_scr, vt_scr, km_scr, sel_scr, bias_scr, qt_scr, m_scr, l_scr, acc_scr, s_scr, al_scr):
    bs = MOBA_BLOCK
    hd = ATTN_HEAD_DIM
    pr = pl.program_id(1)
    qi = pl.program_id(2)
    nbp = km_scr.shape[0]
    n_ch = 4

    @pl.when(qi == 0)
    def _():
        km_scr[...] = jnp.zeros(km_scr.shape, F32)
        for i in range(nb):
            kblk = k_ref[i * bs:(i + 1) * bs, :]
            kb_scr[i] = kblk.astype(BF16)
            vt_scr[i] = v_ref[i * bs:(i + 1) * bs, :].T.astype(BF16)
            km_scr[i:i + 1, :] = jnp.sum(kblk, axis=0, keepdims=True) * (1.0 / bs)

    lane = lax.broadcasted_iota(jnp.int32, (bs, LANES), 1)
    qts = []
    for par in range(2):
        qp = q_ref[:, par * LANES:(par + 1) * LANES]
        qr = pltpu.roll(qp, hd, 1)
        mine = (lane >= hd) if par == 1 else (lane < hd)
        q_g0 = jnp.where(mine, qp if par == 0 else qr, 0.0)
        q_g1 = jnp.where(mine, qr if par == 0 else qp, 0.0)
        qts += [q_g0.T, q_g1.T]

    rowb = lax.broadcasted_iota(jnp.int32, (nbp, bs), 0)
    rk = lax.broadcasted_iota(jnp.int32, (bs, bs), 0)
    cq = lax.broadcasted_iota(jnp.int32, (bs, bs), 1)
    rel = (cq - rk).astype(F32)
    km = km_scr[...]
    slopes = [slopes_ref[n_ch * pr + c] for c in range(n_ch)]
    kd = kb_scr[qi]
    for c in range(n_ch):
        par = c // 2
        gate = _dot_precise(km, qts[c])
        gate = jnp.where(rowb < qi, gate, NEG_INF)
        sel_scr[c] = jnp.where(_topk_mask_rows(gate, rowb, MOBA_TOPK), 0.0, NEG_INF)
        bias = -slopes[c] * rel
        bias_scr[c] = bias
        qtb = (qts[c] * (hd ** -0.5)).astype(BF16)
        qt_scr[c] = qtb
        s = _dot(kd, qtb) + bias
        s = jnp.where(rel >= 0.0, s, NEG_INF)
        m0 = jnp.max(s, axis=0, keepdims=True)
        p = jnp.exp(s - m0)
        m_scr[c] = m0
        l_scr[c] = jnp.sum(p, axis=0, keepdims=True)
        acc_scr[c] = _dot(vt_scr[qi, par * hd:(par + 1) * hd, :], p.astype(BF16))

    def scores(j, slot):
        kj = kb_scr[j]
        dist0 = jnp.asarray((qi - j) * bs, dtype=F32)
        for c in range(n_ch):
            colb = sel_scr[c, pl.ds(j, 1), :] - slopes[c] * dist0
            sj = _dot(kj, qt_scr[c]) + bias_scr[c] + colb
            m_old = m_scr[c]
            m_new = jnp.maximum(m_old, jnp.max(sj, axis=0, keepdims=True))
            al_scr[slot, c] = jnp.exp(m_old - m_new)
            m_scr[c] = m_new
            s_scr[slot, c] = sj - m_new

    def accumulate(j, slot):
        for c in range(n_ch):
            par = c // 2
            pj = jnp.exp(s_scr[slot, c])
            alpha = al_scr[slot, c]
            l_scr[c] = alpha * l_scr[c] + jnp.sum(pj, axis=0, keepdims=True)
            pv = _dot(vt_scr[j, par * hd:(par + 1) * hd, :], pj.astype(BF16))
            acc_scr[c] = alpha * acc_scr[c] + pv

    @pl.when(qi > 0)
    def _():
        scores(0, 0)

        def body(j, carry):
            scores(j + 1, (j + 1) % 2)
            accumulate(j, j % 2)
            return carry

        lax.fori_loop(0, qi - 1, body, 0)
        accumulate(qi - 1, (qi - 1) % 2)

    for par in range(2):
        o_g0 = acc_scr[2 * par] / l_scr[2 * par]
        o_g1 = acc_scr[2 * par + 1] / l_scr[2 * par + 1]
        o_ref[:, par * LANES:(par + 1) * LANES] = jnp.concatenate([o_g0, o_g1], axis=0).T.astype(o_ref.dtype)


def _attn_prompt(proj, slopes, bsz, t, n_kv, q_col, k_col, v_col):
    n = proj.shape[0]
    bs = MOBA_BLOCK
    nb = t // bs
    nbp = -(-nb // 16) * 16
    n_pairs = n_kv // 2
    qw = 2 * LANES
    hd = ATTN_HEAD_DIM
    grid_spec = pltpu.PrefetchScalarGridSpec(
        num_scalar_prefetch=1,
        grid=(bsz, n_pairs, nb),
        in_specs=[pl.BlockSpec((bs, qw), lambda b, h, i, s: (b * nb + i, q_col // qw + h)),
                  pl.BlockSpec((t, LANES), lambda b, h, i, s: (b, k_col // LANES + h)),
                  pl.BlockSpec((t, LANES), lambda b, h, i, s: (b, v_col // LANES + h))],
        out_specs=pl.BlockSpec((bs, qw), lambda b, h, i, s: (b * nb + i, h)),
        scratch_shapes=[pltpu.VMEM((nb, bs, LANES), BF16),
                        pltpu.VMEM((nb, LANES, bs), BF16),
                        pltpu.VMEM((nbp, LANES), F32),
                        pltpu.VMEM((4, nbp, bs), F32),
                        pltpu.VMEM((4, bs, bs), F32),
                        pltpu.VMEM((4, LANES, bs), BF16),
                        pltpu.VMEM((4, 1, bs), F32),
                        pltpu.VMEM((4, 1, bs), F32),
                        pltpu.VMEM((4, hd, bs), F32),
                        pltpu.VMEM((2, 4, bs, bs), F32),
                        pltpu.VMEM((2, 4, 1, bs), F32)])
    return pl.pallas_call(
        functools.partial(_attn_prompt_kernel, nb),
        out_shape=jax.ShapeDtypeStruct((n, n_kv * LANES), BF16),
        grid_spec=grid_spec,
        compiler_params=_cparams(3),
        name="attn_prompt",
        cost_estimate=pl.CostEstimate(flops=2 * bsz * n_kv * 2 * (nb * (nb + 1) // 2) * bs * bs * 2 * LANES,
                                      transcendentals=bsz * n_kv * 2 * (nb * (nb + 1) // 2) * bs * bs,
                                      bytes_accessed=n * (4 * (qw * n_pairs + 2 * LANES * n_pairs) + 2 * qw * n_pairs)),
    )(slopes, proj, proj, proj)


def _diag_extract(o, n_kv, rows_per_kv):
    hd = ATTN_HEAD_DIM
    tiles = []
    for k in range(n_kv):
        tl = o[k * rows_per_kv:(k + 1) * rows_per_kv, (k // 2) * LANES:(k // 2 + 1) * LANES]
        if k % 2 == 1:
            tl = pltpu.roll(tl, hd, 1)
        tiles.append(tl)
    return jnp.concatenate(tiles, axis=0)


def _attn_sample_kernel(bps, n_kv, tq, past_len, pt_ref, *refs):
    npg = 2 * bps
    kp = refs[0:npg]
    vp = refs[npg:2 * npg]
    qbd_ref, slope_ref, trow_ref, knew_ref, vnew_ref, o_ref, st_scr, km_scr, kn_scr, vn_scr = refs[2 * npg:]
    bs = MOBA_BLOCK
    hd = ATTN_HEAD_DIM
    rows = qbd_ref.shape[1]
    kvw = qbd_ref.shape[2]
    page = kp[0].shape[-1]
    rpk = rows // n_kv
    s_id = pl.program_id(1)
    n_steps = pl.num_programs(1)
    nblk = past_len // bs

    @pl.when(s_id == 0)
    def _():
        km_scr[...] = jnp.zeros(km_scr.shape, F32)

    qf = qbd_ref[0]
    qb = (qf * (hd ** -0.5)).astype(BF16)
    slope = slope_ref[...]
    lane = lax.broadcasted_iota(jnp.int32, (rows, LANES), 1)
    lane_k = lax.broadcasted_iota(jnp.int32, (kvw, LANES), 1)
    sc0 = slope * lane.astype(F32)
    sc1 = sc0 + slope * float(page)

    for bi in range(bps):
        j = s_id * bps + bi
        kt0 = kp[2 * bi][0].reshape(kvw, page)
        kt1 = kp[2 * bi + 1][0].reshape(kvw, page)
        vt0 = vp[2 * bi][0].reshape(kvw, page)
        vt1 = vp[2 * bi + 1][0].reshape(kvw, page)
        ksum = jnp.sum(kt0 + kt1, axis=1, keepdims=True) * (1.0 / bs)
        km_scr[...] = jnp.where(lane_k == j, ksum, km_scr[...])
        s0 = _dot(qb, kt0.astype(BF16)) + sc0
        s1 = _dot(qb, kt1.astype(BF16)) + sc1
        m = jnp.maximum(jnp.max(s0, axis=-1, keepdims=True), jnp.max(s1, axis=-1, keepdims=True))
        p0 = jnp.exp(s0 - m)
        p1 = jnp.exp(s1 - m)
        l = jnp.sum(p0, axis=-1, keepdims=True) + jnp.sum(p1, axis=-1, keepdims=True)
        o = _dot_nt(p0.astype(BF16), vt0.astype(BF16)) + _dot_nt(p1.astype(BF16), vt1.astype(BF16))
        od = _diag_extract(o, n_kv, rpk)
        st_scr[j] = jnp.where(lane < hd, od, jnp.where(lane == hd, m, l))

    @pl.when(s_id == n_steps - 1)
    def _():
        trow = trow_ref[...]
        gate = _dot_precise(qf, km_scr[...])
        gate = jnp.where(lane < nblk, gate, NEG_INF)
        sel, _ = _topk_mask(gate, lane, MOBA_TOPK)
        selneg = jnp.where(sel, 0.0, NEG_INF)

        kn_scr[...] = jnp.zeros(kn_scr.shape, F32)
        vn_scr[...] = jnp.zeros(vn_scr.shape, F32)
        kn_scr[0:tq, :] = knew_ref[...]
        vn_scr[0:tq, :] = vnew_ref[...]
        lanef = lane.astype(F32)
        so = _dot_nt(qb, kn_scr[...].astype(BF16)) - slope * (trow - lanef)
        so = jnp.where(jnp.logical_and(lanef <= trow, lane < tq), so, NEG_INF)
        m_run = jnp.max(so, axis=-1, keepdims=True)
        po = jnp.exp(so - m_run)
        den = jnp.sum(po, axis=-1, keepdims=True)
        num = _diag_extract(_dot(po.astype(BF16), vn_scr[...].astype(BF16)), n_kv, rpk)

        slope1 = slope[:, 0:1]
        base = -slope1 * (float(past_len) + trow[:, 0:1])

        def body(j, carry):
            m_run, num, den = carry
            tile = st_scr[j]
            selcol = jnp.max(jnp.where(lane == j, selneg, NEG_INF), axis=-1, keepdims=True)
            mj = tile[:, hd:hd + 1] + (base + slope1 * jnp.asarray(j * bs, dtype=F32)) + selcol
            lj = tile[:, hd + 1:hd + 2]
            m_new = jnp.maximum(m_run, mj)
            a = jnp.exp(m_run - m_new)
            b = jnp.exp(mj - m_new)
            return m_new, a * num + b * tile, a * den + b * lj

        m_run, num, den = lax.fori_loop(0, nblk, body, (m_run, num, den))
        o_ref[0] = num / den


def _attn_sample(cache_kt, cache_vt, layer, page_table, qbd, slope_rows, t_rows, k_new, v_new, tq, n_kv, past_len):
    nbat, rows, kvw = qbd.shape
    page = cache_kt.shape[4]
    hd = cache_kt.shape[3]
    bs = MOBA_BLOCK
    nblk = past_len // bs
    bps = 4
    while nblk % bps:
        bps //= 2
    n_steps = nblk // bps
    assert bs == 2 * page and nblk <= LANES
    npg = 2 * bps

    def page_spec(r):
        return pl.BlockSpec((None, 1, n_kv, hd, page), lambda b, s, pt: (layer, pt[b, s * npg + r], 0, 0, 0))

    in_specs = ([page_spec(r) for r in range(npg)] + [page_spec(r) for r in range(npg)]
                + [pl.BlockSpec((1, rows, kvw), lambda b, s, pt: (b, 0, 0)),
                   pl.BlockSpec((rows, LANES), lambda b, s, pt: (0, 0)),
                   pl.BlockSpec((rows, LANES), lambda b, s, pt: (0, 0)),
                   pl.BlockSpec((tq, kvw), lambda b, s, pt: (b, 0)),
                   pl.BlockSpec((tq, kvw), lambda b, s, pt: (b, 0))])
    grid_spec = pltpu.PrefetchScalarGridSpec(
        num_scalar_prefetch=1,
        grid=(nbat, n_steps),
        in_specs=in_specs,
        out_specs=pl.BlockSpec((1, rows, LANES), lambda b, s, pt: (b, 0, 0)),
        scratch_shapes=[pltpu.VMEM((nblk, rows, LANES), F32),
                        pltpu.VMEM((kvw, LANES), F32),
                        pltpu.VMEM((LANES, kvw), F32),
                        pltpu.VMEM((LANES, kvw), F32)])
    return pl.pallas_call(
        functools.partial(_attn_sample_kernel, bps, n_kv, tq, past_len),
        out_shape=jax.ShapeDtypeStruct((nbat, rows, LANES), F32),
        grid_spec=grid_spec,
        compiler_params=_cparams(2),
        name="attn_sample",
        cost_estimate=pl.CostEstimate(flops=2 * 2 * nbat * rows * past_len * kvw,
                                      transcendentals=nbat * rows * past_len,
                                      bytes_accessed=2 * 4 * nbat * past_len * kvw),
    )(page_table, *([cache_kt] * npg), *([cache_vt] * npg), qbd, slope_rows, t_rows, k_new, v_new)


def _outproj_kernel(alpha, n_exp, ys_ref, ya_ref, gs_ref, ga_ref, x_ref, gm_ref, scf_ref, shf_ref,
                    wos_ref, woa_ref, wout_ref, g1_ref, b1_ref, wr_ref, br_ref,
                    x1_ref, hf_ref, meta_ref, cnt_ref, cnt_scr):
    tm = x_ref.shape[0]
    merged = (jax.nn.sigmoid(gs_ref[...]) * _dot(ys_ref[...].astype(BF16), wos_ref[...])
              + jax.nn.sigmoid(ga_ref[...]) * _dot(ya_ref[...], woa_ref[...]))
    upd = _dot(merged.astype(BF16), wout_ref[...])
    x1 = _layer_norm(alpha * x_ref[...] + gm_ref[0] * upd, g1_ref[...], b1_ref[...])
    x1_ref[...] = x1
    hf = x1 * scf_ref[0] + shf_ref[0]
    hf_ref[...] = hf
    logits = _dot_precise(hf, wr_ref[...]) + br_ref[...]
    lane = lax.broadcasted_iota(jnp.int32, logits.shape, 1)
    logits = jnp.where(lane < n_exp, logits, NEG_INF)
    g = logits
    picks, vals, ids = [], [], []
    for _ in range(TOP_K):
        m = jnp.max(g, axis=-1, keepdims=True)
        idx = jnp.min(jnp.where(g == m, lane, LANES), axis=-1, keepdims=True)
        pick = lane == idx
        g = jnp.where(pick, NEG_INF, g)
        picks.append(pick)
        vals.append(m)
        ids.append(idx)
    es = [jnp.exp(v - vals[0]) for v in vals]
    inv = 1.0 / (es[0] + es[1] + es[2] + es[3])

    @pl.when(pl.program_id(0) == 0)
    def _():
        cnt_scr[...] = jnp.zeros(cnt_scr.shape, F32)

    onehot = jnp.zeros(logits.shape, F32)
    for pick in picks:
        onehot = onehot + jnp.where(pick, 1.0, 0.0)
    ri = lax.broadcasted_iota(jnp.int32, (tm, tm), 0)
    ci = lax.broadcasted_iota(jnp.int32, (tm, tm), 1)
    ltri = jnp.where(ri > ci, 1.0, 0.0).astype(BF16)
    rank_mat = _dot(ltri, onehot.astype(BF16)) + cnt_scr[0:1, :]
    cnt_new = cnt_scr[0:1, :] + jnp.sum(onehot, axis=0, keepdims=True)
    cnt_scr[...] = jnp.broadcast_to(cnt_new, cnt_scr.shape)
    cnt_ref[...] = jnp.broadcast_to(cnt_new, cnt_ref.shape)
    meta = jnp.zeros(logits.shape, F32)
    for k in range(TOP_K):
        rank_k = jnp.sum(jnp.where(picks[k], rank_mat, 0.0), axis=-1, keepdims=True)
        meta = jnp.where(lane == k, ids[k].astype(F32), meta)
        meta = jnp.where(lane == TOP_K + k, rank_k, meta)
        meta = jnp.where(lane == 2 * TOP_K + k, es[k] * inv, meta)
    meta_ref[...] = meta


def _outproj(alpha, n_exp, ys, ya, proj, x, gm, scf1, shf, wos, woa, wout, g1, b1, wr, br, tm, tiles_per_group,
             gs_blk, ga_blk):
    n, d = x.shape
    r = gm.shape[1]
    grp = lambda i: (i // tiles_per_group, 0, 0)
    const = lambda i: (0, 0)
    return pl.pallas_call(
        functools.partial(_outproj_kernel, alpha, n_exp),
        out_shape=(jax.ShapeDtypeStruct((n, d), F32),
                   jax.ShapeDtypeStruct((n, d), F32),
                   jax.ShapeDtypeStruct((n, LANES), F32),
                   jax.ShapeDtypeStruct((SUBLANES, LANES), F32)),
        grid=(n // tm,),
        in_specs=[pl.BlockSpec((tm, ys.shape[1]), lambda i: (i, 0)),
                  pl.BlockSpec((tm, ya.shape[1]), lambda i: (i, 0)),
                  pl.BlockSpec((tm, d), lambda i: (i, gs_blk)),
                  pl.BlockSpec((tm, d), lambda i: (i, ga_blk)),
                  pl.BlockSpec((tm, d), lambda i: (i, 0)),
                  pl.BlockSpec((1, r, d), grp),
                  pl.BlockSpec((1, r, d), grp),
                  pl.BlockSpec((1, r, d), grp),
                  pl.BlockSpec(wos.shape, const),
                  pl.BlockSpec(woa.shape, const),
                  pl.BlockSpec(wout.shape, const),
                  pl.BlockSpec((1, d), const),
                  pl.BlockSpec((1, d), const),
                  pl.BlockSpec((d, LANES), const),
                  pl.BlockSpec((1, LANES), const)],
        out_specs=(pl.BlockSpec((tm, d), lambda i: (i, 0)),
                   pl.BlockSpec((tm, d), lambda i: (i, 0)),
                   pl.BlockSpec((tm, LANES), lambda i: (i, 0)),
                   pl.BlockSpec((SUBLANES, LANES), const)),
        scratch_shapes=[pltpu.VMEM((SUBLANES, LANES), F32)],
        compiler_params=_cparams(1),
        name="outproj",
        cost_estimate=pl.CostEstimate(flops=2 * n * d * (ys.shape[1] + ya.shape[1] + d + 3 * LANES + tm),
                                      transcendentals=2 * n * d,
                                      bytes_accessed=n * (2 * ys.shape[1] + 2 * ya.shape[1] + 4 * 5 * d)),
    )(ys, ya, proj, proj, x, gm, scf1, shf, wos, woa, wout, g1, b1, wr, br)


def _sc_gather(table, idx):
    _, dcol = table.shape
    b = idx.shape[0]
    nw = SC_CORES * SC_SUBCORES
    ch = SC_GATHER_ROWS
    assert b % (nw * 2 * ch) == 0
    b_per_w = b // nw
    n_ch = b_per_w // ch
    mesh = plsc.VectorSubcoreMesh(core_axis_name="c", subcore_axis_name="s")

    @functools.partial(
        pl.kernel, mesh=mesh,
        out_type=jax.ShapeDtypeStruct((b, dcol), table.dtype),
        scratch_types=[pltpu.VMEM((ch,), jnp.int32), pltpu.VMEM((ch,), jnp.int32),
                       pltpu.VMEM((ch, dcol), table.dtype), pltpu.VMEM((ch, dcol), table.dtype),
                       pltpu.SemaphoreType.DMA, pltpu.SemaphoreType.DMA],
        name="sc_gather",
        cost_estimate=pl.CostEstimate(flops=0, transcendentals=0,
                                      bytes_accessed=2 * b * dcol * table.dtype.itemsize + 4 * b),
    )
    def gather(table_hbm, idx_hbm, out_hbm, idx0, idx1, rows0, rows1, sem0, sem1):
        idx_v, rows_v, sems = (idx0, idx1), (rows0, rows1), (sem0, sem1)
        wid = lax.axis_index("s") * SC_CORES + lax.axis_index("c")
        base = wid * b_per_w

        def start(c, slot):
            off = pl.multiple_of(base + c * ch, ch)
            pltpu.sync_copy(idx_hbm.at[pl.ds(off, ch)], idx_v[slot])
            pltpu.async_copy(table_hbm.at[idx_v[slot]], rows_v[slot], sems[slot])

        def finish(c, slot):
            pltpu.make_async_copy(table_hbm.at[idx_v[slot]], rows_v[slot], sems[slot]).wait()
            off = pl.multiple_of(base + c * ch, ch)
            pltpu.sync_copy(rows_v[slot], out_hbm.at[pl.ds(off, ch)])

        start(0, 0)

        @pl.loop(0, n_ch, step=2)
        def _(c):
            start(c + 1, 1)
            finish(c, 0)

            @pl.when(c + 2 < n_ch)
            def _():
                start(c + 2, 0)

            finish(c + 1, 1)

    return gather(table, idx)


def _moe_experts_kernel(d_ff, te_ref, x_ref, wgu_ref, bgu_ref, wdn_ref, bdn_ref, o_ref, wgu_scr, wdn_scr):
    i = pl.program_id(0)
    prev = te_ref[jnp.maximum(i - 1, 0)]

    @pl.when(jnp.logical_or(i == 0, te_ref[i] != prev))
    def _():
        wgu_scr[...] = wgu_ref[0].astype(BF16)
        wdn_scr[...] = wdn_ref[0].astype(BF16)

    gu = _dot(x_ref[...].astype(BF16), wgu_scr[...]) + bgu_ref[0]
    gg = jnp.minimum(gu[:, :d_ff], SWIGLU_LIMIT)
    uu = jnp.clip(gu[:, d_ff:], -SWIGLU_LIMIT, SWIGLU_LIMIT)
    act = (uu + 1.0) * (gg * jax.nn.sigmoid(SWIGLU_ALPHA * gg))
    o_ref[...] = _dot(act.astype(BF16), wdn_scr[...]) + bdn_ref[0]


def _moe_experts(xs, tile_expert, layer, w_gu, b_gu4, w_down, b_down4, tme):
    p, d = xs.shape
    two_ff = w_gu.shape[3]
    d_ff = two_ff // 2
    grid_spec = pltpu.PrefetchScalarGridSpec(
        num_scalar_prefetch=1,
        grid=(p // tme,),
        in_specs=[pl.BlockSpec((tme, d), lambda i, te: (i, 0)),
                  pl.BlockSpec((None, 1, d, two_ff), lambda i, te: (layer, te[i], 0, 0)),
                  pl.BlockSpec((None, 1, 1, two_ff), lambda i, te: (layer, te[i], 0, 0)),
                  pl.BlockSpec((None, 1, d_ff, d), lambda i, te: (layer, te[i], 0, 0)),
                  pl.BlockSpec((None, 1, 1, d), lambda i, te: (layer, te[i], 0, 0))],
        out_specs=pl.BlockSpec((tme, d), lambda i, te: (i, 0)),
        scratch_shapes=[pltpu.VMEM((d, two_ff), BF16),
                        pltpu.VMEM((d_ff, d), BF16)])
    return pl.pallas_call(
        functools.partial(_moe_experts_kernel, d_ff),
        out_shape=jax.ShapeDtypeStruct((p, d), F32),
        grid_spec=grid_spec,
        compiler_params=_cparams(1),
        name="moe_experts",
        cost_estimate=pl.CostEstimate(flops=2 * p * d * 3 * d_ff, transcendentals=p * d_ff,
                                      bytes_accessed=8 * p * d + 4 * w_gu.shape[1] * 3 * d * d_ff),
    )(tile_expert, xs, w_gu, b_gu4, w_down, b_down4)


def _moe_combine_kernel(alpha, *refs):
    yk_refs = refs[:TOP_K]
    meta_ref, x1_ref, gf_ref, g2_ref, b2_ref, o_ref = refs[TOP_K:]
    meta = meta_ref[...]
    acc = meta[:, 2 * TOP_K:2 * TOP_K + 1] * yk_refs[0][...]
    for k in range(1, TOP_K):
        acc = acc + meta[:, 2 * TOP_K + k:2 * TOP_K + k + 1] * yk_refs[k][...]
    o_ref[...] = _layer_norm(alpha * x1_ref[...] + gf_ref[0] * acc, g2_ref[...], b2_ref[...])


def _moe_combine(alpha, yk, n_pad, meta, x1, gf, g2, b2, tm, tiles_per_group):
    n, d = x1.shape
    r = gf.shape[1]
    bpk = n_pad // tm
    assert n_pad % tm == 0

    def choice_spec(k):
        return pl.BlockSpec((tm, d), lambda i: (k * bpk + i, 0))

    return pl.pallas_call(
        functools.partial(_moe_combine_kernel, alpha),
        out_shape=jax.ShapeDtypeStruct((n, d), F32),
        grid=(n // tm,),
        in_specs=[choice_spec(k) for k in range(TOP_K)] + [
                  pl.BlockSpec((tm, LANES), lambda i: (i, 0)),
                  pl.BlockSpec((tm, d), lambda i: (i, 0)),
                  pl.BlockSpec((1, r, d), lambda i: (i // tiles_per_group, 0, 0)),
                  pl.BlockSpec((1, d), lambda i: (0, 0)),
                  pl.BlockSpec((1, d), lambda i: (0, 0))],
        out_specs=pl.BlockSpec((tm, d), lambda i: (i, 0)),
        compiler_params=_cparams(1),
        name="moe_combine",
        cost_estimate=pl.CostEstimate(flops=2 * TOP_K * n * d, transcendentals=n,
                                      bytes_accessed=4 * n * d * (TOP_K + 2)),
    )(*([yk] * TOP_K), meta, x1, gf, g2, b2)


def _moe_route(meta, counts, tile, tm):
    n = meta.shape[0]
    n_exp = counts.shape[0]
    gran = SC_CORES * SC_SUBCORES * 2 * SC_GATHER_ROWS
    unit = gran * tile // math.gcd(gran, tile)
    eid = meta[:, 0:TOP_K].astype(jnp.int32)
    rank = meta[:, TOP_K:2 * TOP_K].astype(jnp.int32)
    padded = -(-counts // tile) * tile
    ends = jnp.cumsum(padded)
    is_e = eid[:, :, None] == jnp.arange(n_exp, dtype=jnp.int32)
    dest = jnp.sum(jnp.where(is_e, (ends - padded)[None, None, :], 0), axis=-1) + rank
    p_rows = -(-(n * TOP_K + n_exp * tile) // unit) * unit
    tok = jnp.broadcast_to(jnp.arange(n, dtype=jnp.int32)[:, None], (n, TOP_K))
    src_tok = jnp.zeros((p_rows,), jnp.int32).at[dest.reshape(-1)].set(
        tok.reshape(-1), unique_indices=True, mode="promise_in_bounds")
    tile_start = jnp.arange(p_rows // tile, dtype=jnp.int32) * tile
    tile_expert = jnp.minimum(jnp.sum(tile_start[:, None] >= ends[None, :], axis=-1), n_exp - 1).astype(jnp.int32)
    quant = max(gran // TOP_K, tm)
    n_pad = -(-n // quant) * quant
    dest_km = jnp.pad(dest.T, ((0, 0), (0, n_pad - n))).reshape(-1)
    return dict(src_tok=src_tok, tile_expert=tile_expert, dest_km=dest_km, n_pad=n_pad, tile=tile)


def _moe_group(alpha, layer, route, hf, meta, x1, gf, g2, b2, w_gu, b_gu4, w_down, b_down4, tm, tiles_per_group):
    xs_sorted = _sc_gather(hf, route["src_tok"])
    ys_sorted = _moe_experts(xs_sorted, route["tile_expert"], layer, w_gu, b_gu4, w_down, b_down4, route["tile"])
    yk = _sc_gather(ys_sorted, route["dest_km"])
    return yk, functools.partial(_moe_combine, alpha, n_pad=route["n_pad"], meta=meta, x1=x1, gf=gf, g2=g2, b2=b2,
                                 tm=tm, tiles_per_group=tiles_per_group)


def _pad_lanes(v, fill=0.0):
    return jnp.pad(v, (0, LANES - v.shape[0]), constant_values=fill).reshape(1, LANES)


def kernel(x_prompt, x_sample, c_prompt, c_sample, cache_k, cache_v, state_ssm, state_conv, page_table, w_ada, b_ada, w_in, conv_w, conv_b, dt_bias, a_log, d_skip, ssm_norm_g, w_o_ssm, w_o_attn, w_out, ln1_g, ln1_b, ln2_g, ln2_b, w_router, b_router, w_gu, b_gu, w_down, b_down):
    depth, d, _ = w_ada.shape
    bp, tp, _ = x_prompt.shape
    bsm, ts, _ = x_sample.shape
    n_heads = dt_bias.shape[1]
    d_inner = n_heads * SSM_HEAD_DIM
    conv_ch = conv_w.shape[2]
    kvw = cache_k.shape[3] * cache_k.shape[4]
    n_kv = cache_k.shape[3]
    aw = w_o_attn.shape[1]
    n_q = aw // ATTN_HEAD_DIM
    n_exp = w_router.shape[2]
    page = cache_k.shape[2]
    past_len = page_table.shape[1] * page
    alpha = float((2.0 * depth) ** 0.25)
    assert d_inner == 2 * d and aw == d and kvw == d // 2 and n_q == 2 * n_kv
    assert conv_ch == d_inner + 2 * SSM_GROUPS * SSM_STATE and conv_ch == 3 * d
    assert tp % MOBA_BLOCK == 0 and tp % SSD_CHUNK == 0 and ts == SUBLANES and n_heads <= LANES

    o_z, o_xbc, o_dt = 0, d_inner, d_inner + conv_ch
    o_q = o_dt + n_heads
    o_k, o_v = o_q + aw, o_q + aw + kvw
    o_gs, o_ga = o_v + kvw, o_v + kvw + d
    c_z, c_q, c_gs, c_ga = 0, d_inner, d_inner + d, d_inner + 2 * d
    c_k = d_inner + 3 * d
    c_v = c_k + kvw
    c_xbc = c_v + kvw
    assert c_xbc % conv_ch == 0

    slopes = 2.0 ** (-8.0 * (jnp.arange(n_q, dtype=F32) + 1.0) / n_q)

    n_c = bsm + bp
    n_c_pad = -(-n_c // SUBLANES) * SUBLANES
    c_all = jnp.concatenate([c_sample, c_prompt, jnp.zeros((n_c_pad - n_c, d), F32)], axis=0)
    mod = _ada(c_all, w_ada, b_ada)

    grp = n_q // n_kv
    rows = n_kv * grp * ts
    r_idx = jnp.arange(rows)
    r_head = (r_idx // (grp * ts)) * grp + (r_idx // ts) % grp
    slope_rows = jnp.broadcast_to(slopes[r_head][:, None], (rows, LANES))
    t_rows = jnp.broadcast_to((r_idx % ts).astype(F32)[:, None], (rows, LANES))
    eye_kv = jnp.eye(n_kv, dtype=F32)

    cache_kt = jnp.transpose(cache_k, (0, 1, 3, 4, 2))
    cache_vt = jnp.transpose(cache_v, (0, 1, 3, 4, 2))

    xp = x_prompt.reshape(bp * tp, d)
    xs = x_sample.reshape(bsm * ts, d)
    tm_p = 1024 if tp % 1024 == 0 else MOBA_BLOCK
    tm_o = 512 if tp % 512 == 0 else MOBA_BLOCK
    n_s = bsm * ts

    def make_layer(l):
        wl = w_in[l]
        w_main = jnp.concatenate([wl[:, o_z:o_z + d_inner], wl[:, o_q:o_q + aw], wl[:, o_gs:o_gs + d],
                                  wl[:, o_ga:o_ga + d], wl[:, o_k:o_k + kvw], wl[:, o_v:o_v + kvw],
                                  wl[:, o_xbc:o_xbc + conv_ch]], axis=1).astype(BF16)
        w_dt = jnp.pad(wl[:, o_dt:o_dt + n_heads], ((0, 0), (0, LANES - n_heads))).astype(BF16)
        w_dtT = wl[:, o_dt:o_dt + n_heads].T.astype(BF16)
        dtb = _pad_lanes(dt_bias[l])
        dtbT = jnp.broadcast_to(dt_bias[l][:, None], (n_heads, SSD_CHUNK))
        a_neg = -jnp.exp(a_log[l])
        a_row = _pad_lanes(a_neg)
        aT = jnp.broadcast_to(a_neg[:, None], (n_heads, SSD_CHUNK))
        dskip = jnp.repeat(d_skip[l], SSM_HEAD_DIM).reshape(1, d_inner)
        ng = ssm_norm_g[l].reshape(1, d_inner)
        cw = conv_w[l]
        cb = conv_b[l].reshape(1, conv_ch)
        wos = w_o_ssm[l].astype(BF16)
        woa = w_o_attn[l].astype(BF16)
        wout = w_out[l].astype(BF16)
        g1, b1 = ln1_g[l].reshape(1, d), ln1_b[l].reshape(1, d)
        g2, b2 = ln2_g[l].reshape(1, d), ln2_b[l].reshape(1, d)
        wr = jnp.pad(w_router[l], ((0, 0), (0, LANES - n_exp)))
        br = _pad_lanes(b_router[l])

        mod_l = mod[l]
        sh_m, sc_m, g_m, sh_f, sc_f, g_f = [mod_l[:, i * d:(i + 1) * d] for i in range(6)]

        def group_mod(v, is_prompt):
            if is_prompt:
                return v[bsm:bsm + bp].reshape(bp, 1, d)
            return jnp.repeat(v[:bsm], ts, axis=0).reshape(1, n_s, d)

        def run(x, is_prompt, h0T, cprev8):
            bsz, t = (bp, tp) if is_prompt else (bsm, ts)
            n = bsz * t
            tm = tm_p if is_prompt else n
            tmo = tm_o if is_prompt else n
            tpg = (t // tm) if is_prompt else 1
            tpgo = (t // tmo) if is_prompt else 1
            gm = lambda v: group_mod(v, is_prompt)
            proj, dt, dtT, k_new, v_new = _inproj(x, gm(1.0 + sc_m), gm(sh_m), w_main, w_dt, w_dtT, tm, tpg,
                                                  c_k, kvw)
            dtT_seq = dtT.reshape(n_heads, bsz, t).transpose(1, 0, 2)
            lr = SSD_CHUNK if is_prompt else t
            y_s, h_newT = _ssd(proj, dt, dtT_seq, h0T, cprev8, cw, cb, dtb, dtbT, a_row, aT, dskip, ng,
                               bsz, t, lr, c_z // d_inner, c_xbc // conv_ch, d_inner, conv_ch)
            if is_prompt:
                y_a = _attn_prompt(proj, slopes, bsz, t, n_kv, c_q, c_k, c_v)
            else:
                q5 = proj[:, c_q:c_q + aw].reshape(bsz, t, n_kv, grp, ATTN_HEAD_DIM)
                q5 = q5.transpose(0, 2, 3, 1, 4).reshape(bsz, n_kv, grp * t, ATTN_HEAD_DIM)
                qbd = (q5[:, :, :, None, :] * eye_kv[None, :, None, :, None]).reshape(bsz, rows, kvw)
                o = _attn_sample(cache_kt, cache_vt, l, page_table, qbd, slope_rows, t_rows, k_new, v_new,
                                 t, n_kv, past_len)
                o = o[:, :, :ATTN_HEAD_DIM].reshape(bsz, n_kv, grp, t, ATTN_HEAD_DIM)
                y_a = o.transpose(0, 3, 1, 2, 4).reshape(n, aw).astype(BF16)
            x1, hf, meta, cnt = _outproj(alpha, n_exp, y_s, y_a, proj, x, gm(g_m), gm(1.0 + sc_f), gm(sh_f),
                                         wos, woa, wout, g1, b1, wr, br, tmo, tpgo, c_gs // d, c_ga // d)
            conv_new = proj[:, c_xbc:c_xbc + conv_ch].reshape(bsz, t, conv_ch)[:, t - (CONV_W - 1):]
            h_new = h_newT.transpose(0, 2, 1).reshape(bsz, n_heads, SSM_HEAD_DIM, SSM_STATE)
            hd4 = (bsz, t, n_kv, ATTN_HEAD_DIM)
            return dict(x1=x1, hf=hf, meta=meta, cnt=cnt[0, :n_exp].astype(jnp.int32), gf=gm(g_f),
                        k=k_new.reshape(hd4), v=v_new.reshape(hd4), h=h_new, conv=conv_new)

        return run, g2, b2

    def sample_front(l, x):
        h0_s = state_ssm[l].reshape(bsm, d_inner, SSM_STATE).transpose(0, 2, 1)
        cv0_s = jnp.pad(state_conv[l], ((0, 0), (SUBLANES - (CONV_W - 1), 0), (0, 0)))
        return layers[l][0](x, False, h0_s, cv0_s)

    layers = [make_layer(l) for l in range(depth)]
    b_gu4 = b_gu.reshape(depth, n_exp, 1, -1)
    b_down4 = b_down.reshape(depth, n_exp, 1, d)
    h0_p = jnp.zeros((bp, SSM_STATE, d_inner), F32)
    cv0_p = jnp.zeros((bp, SUBLANES, conv_ch), F32)
    outs = {k: [] for k in ("kp", "vp", "hp", "cvp", "ks", "vs", "hs", "cvs")}
    fs = None
    for l in range(depth):
        run, g2, b2 = layers[l]
        fp = run(xp, True, h0_p, cv0_p)
        route_p = _moe_route(fp["meta"], fp["cnt"], MOE_TILE, MOBA_BLOCK)
        fs = sample_front(l, xs)
        outs["kp"].append(fp["k"]); outs["vp"].append(fp["v"]); outs["hp"].append(fp["h"]); outs["cvp"].append(fp["conv"])
        outs["ks"].append(fs["k"]); outs["vs"].append(fs["v"]); outs["hs"].append(fs["h"]); outs["cvs"].append(fs["conv"])

        yk_p, combine_p = _moe_group(alpha, l, route_p, fp["hf"], fp["meta"], fp["x1"], fp["gf"], g2, b2,
                                     w_gu, b_gu4, w_down, b_down4, MOBA_BLOCK, tp // MOBA_BLOCK)
        route_s = _moe_route(fs["meta"], fs["cnt"], MOE_TILE_SMALL, n_s)
        yk_s, combine_s = _moe_group(alpha, l, route_s, fs["hf"], fs["meta"], fs["x1"], fs["gf"], g2, b2,
                                     w_gu, b_gu4, w_down, b_down4, n_s, 1)
        xs = combine_s(yk_s)
        xp = combine_p(yk_p)

    st = lambda k: jnp.stack(outs[k])
    return (xp.reshape(bp, tp, d), xs.reshape(bsm, ts, d), st("kp"), st("vp"), st("hp"), st("cvp"),
            st("ks"), st("vs"), st("hs"), st("cvs"))
```

```python
import functools
import math

import jax
import jax.numpy as jnp
from jax import lax
from jax.experimental import pallas as pl
from jax.experimental.pallas import tpu as pltpu
from jax.experimental.pallas import tpu_sc as plsc

F32 = jnp.float32
BF16 = jnp.bfloat16
NEG_INF = float("-inf")

LANES = 128
SUBLANES = 8
VMEM_LIMIT = 56 * 1024 * 1024
SC_CORES = 2
SC_SUBCORES = 16
SC_GATHER_ROWS = 32
MOE_TILE = 512
MOE_TILE_SMALL = 64

SSM_HEAD_DIM = 64
SSM_GROUPS = 4
SSM_STATE = 128
CONV_W = 4
SSD_CHUNK = 128
RMS_EPS = 1e-5
ATTN_HEAD_DIM = 64
MOBA_BLOCK = 256
MOBA_TOPK = 3
TOP_K = 4
SWIGLU_LIMIT = 7.0
SWIGLU_ALPHA = 1.702
LN_EPS = 1e-5


def _cparams(n_axes):
    return pltpu.CompilerParams(dimension_semantics=("arbitrary",) * n_axes,
                                vmem_limit_bytes=VMEM_LIMIT)


def _dot(a, b):
    return jnp.dot(a, b, preferred_element_type=F32)


def _dot_nt(a, b):
    return lax.dot_general(a, b, (((1,), (1,)), ((), ())), preferred_element_type=F32)


def _split2(a):
    hi = a.astype(BF16)
    lo = (a - hi.astype(F32)).astype(BF16)
    return hi, lo


def _split3(a):
    hi = a.astype(BF16)
    r = a - hi.astype(F32)
    mid = r.astype(BF16)
    lo = (r - mid.astype(F32)).astype(BF16)
    return hi, mid, lo


def _dot_nt_precise(a, b):
    ah, al = _split2(a)
    bh, bl = _split2(b)
    return _dot_nt(ah, bh) + (_dot_nt(ah, bl) + _dot_nt(al, bh))


def _dot_precise(a, b):
    ah, al = _split2(a)
    bh, bl = _split2(b)
    return _dot(ah, bh) + (_dot(ah, bl) + _dot(al, bh))


def _silu(x):
    return x * jax.nn.sigmoid(x)


def _softplus(x):
    return jnp.maximum(x, 0.0) + jnp.log1p(jnp.exp(-jnp.abs(x)))


def _layer_norm(x, g, b):
    mu = jnp.mean(x, axis=-1, keepdims=True)
    xc = x - mu
    var = jnp.mean(xc * xc, axis=-1, keepdims=True)
    return xc * lax.rsqrt(var + LN_EPS) * g + b


def _pack_bf16_pairs(x):
    w = x.shape[1] // 2
    bits = lax.bitcast_convert_type(x, jnp.uint32)
    top = (bits + (jnp.uint32(0x7FFF) + ((bits >> 16) & jnp.uint32(1)))) >> 16
    return top[:, :w] | (top[:, w:] << 16)


def _unpack_bf16_pairs(wd):
    lo = lax.bitcast_convert_type(wd << 16, F32)
    hi = lax.bitcast_convert_type(wd & jnp.uint32(0xFFFF0000), F32)
    return lo.astype(BF16), hi.astype(BF16)


def _topk_mask(score, lane, k):
    width = score.shape[-1]
    sel = jnp.zeros(score.shape, dtype=jnp.bool_)
    vals = []
    g = score
    for _ in range(k):
        m = jnp.max(g, axis=-1, keepdims=True)
        idx = jnp.min(jnp.where(g == m, lane, width), axis=-1, keepdims=True)
        pick = jnp.logical_and(lane == idx, m > NEG_INF)
        sel = jnp.logical_or(sel, pick)
        g = jnp.where(pick, NEG_INF, g)
        vals.append(m)
    return sel, vals


def _ada_kernel(c_ref, w_ref, b_ref, o_ref):
    s = _silu(c_ref[...]).astype(BF16)
    o_ref[0] = _dot(s, w_ref[0].astype(BF16)) + b_ref[0]


def _ada(c_all, w_ada, b_ada):
    depth, d, n6 = w_ada.shape
    rows = c_all.shape[0]
    tn = 1024
    return pl.pallas_call(
        _ada_kernel,
        out_shape=jax.ShapeDtypeStruct((depth, rows, n6), F32),
        grid=(depth, n6 // tn),
        in_specs=[pl.BlockSpec((rows, d), lambda l, j: (0, 0)),
                  pl.BlockSpec((1, d, tn), lambda l, j: (l, 0, j)),
                  pl.BlockSpec((1, 1, tn), lambda l, j: (l, 0, j))],
        out_specs=pl.BlockSpec((1, rows, tn), lambda l, j: (l, 0, j)),
        compiler_params=_cparams(2),
        name="ada",
    )(c_all, w_ada, b_ada.reshape(depth, 1, n6))


def _inproj_kernel(j_kv, kvw, x_ref, sc_ref, sh_ref, w_ref, wdt_ref, wdtT_ref, o_ref, dt_ref, dtT_ref, k_ref, v_ref,
                   h_scr):
    @pl.when(pl.program_id(1) == 0)
    def _():
        hb = (x_ref[...] * sc_ref[0] + sh_ref[0]).astype(BF16)
        h_scr[...] = hb
        dt_ref[...] = _dot(hb, wdt_ref[...])
        dtT_ref[...] = _dot_nt(wdtT_ref[...], hb)

    res = _dot(h_scr[...], w_ref[...])
    o_ref[...] = res

    @pl.when(pl.program_id(1) == j_kv)
    def _():
        k_ref[...] = res[:, 0:kvw]
        v_ref[...] = res[:, kvw:2 * kvw]


def _inproj(x, sc1, sh, w_main, w_dt, w_dtT, tm, tiles_per_group, k_col, kvw):
    n, d = x.shape
    n_main = w_main.shape[1]
    tn = 1024
    r = sc1.shape[1]
    nh = w_dtT.shape[0]
    assert k_col % tn == 0 and 2 * kvw <= tn
    grp = lambda i, j: (i // tiles_per_group, 0, 0)
    return pl.pallas_call(
        functools.partial(_inproj_kernel, k_col // tn, kvw),
        out_shape=(jax.ShapeDtypeStruct((n, n_main), F32),
                   jax.ShapeDtypeStruct((n, LANES), F32),
                   jax.ShapeDtypeStruct((nh, n), F32),
                   jax.ShapeDtypeStruct((n, kvw), F32),
                   jax.ShapeDtypeStruct((n, kvw), F32)),
        grid=(n // tm, n_main // tn),
        in_specs=[pl.BlockSpec((tm, d), lambda i, j: (i, 0)),
                  pl.BlockSpec((1, r, d), grp),
                  pl.BlockSpec((1, r, d), grp),
                  pl.BlockSpec((d, tn), lambda i, j: (0, j)),
                  pl.BlockSpec((d, LANES), lambda i, j: (0, 0)),
                  pl.BlockSpec((nh, d), lambda i, j: (0, 0))],
        out_specs=(pl.BlockSpec((tm, tn), lambda i, j: (i, j)),
                   pl.BlockSpec((tm, LANES), lambda i, j: (i, 0)),
                   pl.BlockSpec((nh, tm), lambda i, j: (0, i)),
                   pl.BlockSpec((tm, kvw), lambda i, j: (i, 0)),
                   pl.BlockSpec((tm, kvw), lambda i, j: (i, 0))),
        scratch_shapes=[pltpu.VMEM((tm, d), BF16)],
        compiler_params=_cparams(2),
        name="inproj",
        cost_estimate=pl.CostEstimate(flops=2 * n * d * (n_main + LANES + nh), transcendentals=0,
                                      bytes_accessed=4 * n * (d + n_main) + 2 * d * n_main * (n // tm)),
    )(x, sc1, sh, w_main, w_dt, w_dtT)


def _ssd_kernel(lr, d_inner, xbc_ref, z_ref, dt_ref, dtT_ref, h0_ref, cprev_ref, cw_ref, cb_ref,
                dtb_ref, dtbT_ref, a_ref, aT_ref, dskip_ref, ng_ref,
                y_ref, hout_ref, xp_scr, h_scr, y_scr, dt_scr, dtT_scr):
    L = SSD_CHUNK
    gs = SSM_GROUPS * SSM_STATE
    n_heads = d_inner // SSM_HEAD_DIM
    pairs_per_group = n_heads // SSM_GROUPS // 2
    c = pl.program_id(1)

    @pl.when(c == 0)
    def _():
        xp_scr[0:SUBLANES, :] = cprev_ref[0]
        h_scr[...] = h0_ref[0]

    if lr < L:
        xp_scr[SUBLANES + lr:SUBLANES + L, :] = jnp.zeros((L - lr, xp_scr.shape[1]), F32)
    xp_scr[SUBLANES:SUBLANES + lr, :] = xbc_ref[...]

    acc = cb_ref[...] + cw_ref[0:1, :] * xp_scr[SUBLANES - 3:SUBLANES - 3 + L, :]
    for j in range(1, CONV_W):
        acc = acc + cw_ref[j:j + 1, :] * xp_scr[SUBLANES - 3 + j:SUBLANES - 3 + j + L, :]
    tail = xp_scr[lr:lr + SUBLANES, :]
    xp_scr[0:SUBLANES, :] = tail
    xc = _silu(acc)

    if lr < L:
        dt_scr[...] = jnp.zeros(dt_scr.shape, F32)
        dt_scr[0:lr, :] = dt_ref[...]
        dtT_scr[...] = jnp.zeros(dtT_scr.shape, F32)
        dtT_scr[:, 0:lr] = dtT_ref[0]
        dt_raw = dt_scr[...]
        dtT_raw = dtT_scr[...]
    else:
        dt_raw = dt_ref[...]
        dtT_raw = dtT_ref[0]
    row = lax.broadcasted_iota(jnp.int32, (L, LANES), 0)
    col = lax.broadcasted_iota(jnp.int32, (n_heads, L), 1)
    dtv = jnp.where(row < lr, _softplus(dt_raw + dtb_ref[...]), 0.0)
    dtvT = jnp.where(col < lr, _softplus(dtT_raw + dtbT_ref[...]), 0.0)
    dA = dtv * a_ref[...]
    dAT = dtvT * aT_ref[...]

    ri = lax.broadcasted_iota(jnp.int32, (L, L), 0)
    ci = lax.broadcasted_iota(jnp.int32, (L, L), 1)
    causal = ri >= ci
    tri = jnp.where(causal, 1.0, 0.0).astype(BF16)
    triT = jnp.where(ri <= ci, 1.0, 0.0).astype(BF16)
    d1, d2, d3 = _split3(dA)
    la = _dot(tri, d1) + (_dot(tri, d2) + _dot(tri, d3))
    e1, e2, e3 = _split3(dAT)
    laT = _dot(e1, triT) + (_dot(e2, triT) + _dot(e3, triT))

    la_last = la[L - 1:L, :]
    to_end = jnp.exp(la_last - la)
    ela = jnp.exp(la)
    cdec = jnp.exp(la_last)

    lane = lax.broadcasted_iota(jnp.int32, (L, LANES), 1)
    lo_half = lane < SSM_HEAD_DIM
    lane1 = lax.broadcasted_iota(jnp.int32, (1, LANES), 1)
    eye = jnp.where(ri == ci, 1.0, 0.0).astype(BF16)

    def pair_bcast(arr, p):
        rows = arr.shape[0]
        a0 = jnp.broadcast_to(arr[:, 2 * p:2 * p + 1], (rows, LANES))
        a1 = jnp.broadcast_to(arr[:, 2 * p + 1:2 * p + 2], (rows, LANES))
        return jnp.where(lo_half[:rows], a0, a1)

    for g in range(SSM_GROUPS):
        bg = xc[:, d_inner + g * SSM_STATE:d_inner + (g + 1) * SSM_STATE].astype(BF16)
        cg = xc[:, d_inner + gs + g * SSM_STATE:d_inner + gs + (g + 1) * SSM_STATE].astype(BF16)
        cb = _dot_nt(cg, bg)
        bgT = _dot_nt(eye, bg).astype(BF16)
        for pp in range(pairs_per_group):
            p = g * pairs_per_group + pp
            sl = slice(p * LANES, (p + 1) * LANES)
            xs_p = xc[:, sl]
            xdt_p = xs_p * pair_bcast(dtv, p)
            ydiag = None
            for a in range(2):
                h = 2 * p + a
                seg = jnp.broadcast_to(la[:, h:h + 1], (L, L)) - laT[h:h + 1, :]
                decay = jnp.exp(jnp.where(causal, seg, NEG_INF))
                m = (cb * decay).astype(BF16)
                half = lo_half if a == 0 else jnp.logical_not(lo_half)
                part = _dot(m, jnp.where(half, xdt_p, 0.0).astype(BF16))
                ydiag = part if ydiag is None else ydiag + part
            hin = h_scr[:, sl]
            yoff = _dot(cg, hin.astype(BF16)) * pair_bcast(ela, p)
            y_scr[:, sl] = ydiag + yoff + dskip_ref[:, sl] * xs_p
            s_new = _dot(bgT, (xdt_p * pair_bcast(to_end, p)).astype(BF16))
            cd_p = jnp.where(lane1 < SSM_HEAD_DIM,
                             jnp.broadcast_to(cdec[:, 2 * p:2 * p + 1], (1, LANES)),
                             jnp.broadcast_to(cdec[:, 2 * p + 1:2 * p + 2], (1, LANES)))
            h_scr[:, sl] = hin * cd_p + s_new

    gw = d_inner // SSM_GROUPS
    for g in range(SSM_GROUPS):
        sl = slice(g * gw, (g + 1) * gw)
        yg = y_scr[0:lr, sl] * _silu(z_ref[:, sl])
        ms = jnp.mean(yg * yg, axis=-1, keepdims=True)
        y_ref[:, sl] = (yg * lax.rsqrt(ms + RMS_EPS) * ng_ref[:, sl]).astype(y_ref.dtype)

    @pl.when(c == pl.num_programs(1) - 1)
    def _():
        hout_ref[0] = h_scr[...]


def _ssd(proj, dt, dtT_seq, h0T, cprev8, cw, cb, dtb, dtbT, a, aT, dskip, ng, bsz, t, lr,
         z_blk, xbc_blk, d_inner, conv_ch):
    n = proj.shape[0]
    nc = t // lr
    n_heads = d_inner // SSM_HEAD_DIM
    L = SSD_CHUNK
    const2 = lambda b, c: (0, 0)
    return pl.pallas_call(
        functools.partial(_ssd_kernel, lr, d_inner),
        out_shape=(jax.ShapeDtypeStruct((n, d_inner), BF16 if lr % 16 == 0 else F32),
                   jax.ShapeDtypeStruct((bsz, SSM_STATE, d_inner), F32)),
        grid=(bsz, nc),
        in_specs=[pl.BlockSpec((lr, conv_ch), lambda b, c: (b * nc + c, xbc_blk)),
                  pl.BlockSpec((lr, d_inner), lambda b, c: (b * nc + c, z_blk)),
                  pl.BlockSpec((lr, LANES), lambda b, c: (b * nc + c, 0)),
                  pl.BlockSpec((1, n_heads, lr), lambda b, c: (b, 0, c)),
                  pl.BlockSpec((1, SSM_STATE, d_inner), lambda b, c: (b, 0, 0)),
                  pl.BlockSpec((1, SUBLANES, conv_ch), lambda b, c: (b, 0, 0)),
                  pl.BlockSpec((CONV_W, conv_ch), const2),
                  pl.BlockSpec((1, conv_ch), const2),
                  pl.BlockSpec((1, LANES), const2),
                  pl.BlockSpec((n_heads, L), const2),
                  pl.BlockSpec((1, LANES), const2),
                  pl.BlockSpec((n_heads, L), const2),
                  pl.BlockSpec((1, d_inner), const2),
                  pl.BlockSpec((1, d_inner), const2)],
        out_specs=(pl.BlockSpec((lr, d_inner), lambda b, c: (b * nc + c, 0)),
                   pl.BlockSpec((1, SSM_STATE, d_inner), lambda b, c: (b, 0, 0))),
        scratch_shapes=[pltpu.VMEM((SUBLANES + L, conv_ch), F32),
                        pltpu.VMEM((SSM_STATE, d_inner), F32),
                        pltpu.VMEM((L, d_inner), F32),
                        pltpu.VMEM((L, LANES), F32),
                        pltpu.VMEM((n_heads, L), F32)],
        compiler_params=_cparams(2),
        name="ssd",
        cost_estimate=pl.CostEstimate(
            flops=2 * bsz * nc * L * (SSM_GROUPS * L * SSM_STATE + d_inner * (L + 2 * SSM_STATE)),
            transcendentals=bsz * nc * L * (n_heads * L + conv_ch + d_inner),
            bytes_accessed=4 * n * (conv_ch + d_inner) + 2 * n * d_inner + 8 * bsz * SSM_STATE * d_inner),
    )(proj, proj, dt, dtT_seq, h0T, cprev8, cw, cb, dtb, dtbT, a, aT, dskip, ng)


def _topk_mask_rows(score, row, k):
    big = score.shape[0]
    sel = jnp.zeros(score.shape, dtype=jnp.bool_)
    g = score
    for _ in range(k):
        m = jnp.max(g, axis=0, keepdims=True)
        idx = jnp.min(jnp.where(g == m, row, big), axis=0, keepdims=True)
        pick = jnp.logical_and(row == idx, m > NEG_INF)
        sel = jnp.logical_or(sel, pick)
        g = jnp.where(pick, NEG_INF, g)
    return sel


def _attn_prompt_kernel(nb, slopes_ref, q_ref, k_ref, v_ref, o_ref,
                        kb_scr, vt_scr, km_scr, sel_scr, bias_scr, qt_scr, m_scr, l_scr, acc_scr, s_scr, al_scr):
    bs = MOBA_BLOCK
    hd = ATTN_HEAD_DIM
    pr = pl.program_id(1)
    qi = pl.program_id(2)
    nbp = km_scr.shape[0]
    n_ch = 4

    @pl.when(qi == 0)
    def _():
        km_scr[...] = jnp.zeros(km_scr.shape, F32)
        for i in range(nb):
            kblk = k_ref[i * bs:(i + 1) * bs, :]
            kb_scr[i] = kblk.astype(BF16)
            vt_scr[i] = v_ref[i * bs:(i + 1) * bs, :].T.astype(BF16)
            km_scr[i:i + 1, :] = jnp.sum(kblk, axis=0, keepdims=True) * (1.0 / bs)

    lane = lax.broadcasted_iota(jnp.int32, (bs, LANES), 1)
    qts = []
    for par in range(2):
        qp = q_ref[:, par * LANES:(par + 1) * LANES]
        qr = pltpu.roll(qp, hd, 1)
        mine = (lane >= hd) if par == 1 else (lane < hd)
        q_g0 = jnp.where(mine, qp if par == 0 else qr, 0.0)
        q_g1 = jnp.where(mine, qr if par == 0 else qp, 0.0)
        qts += [q_g0.T, q_g1.T]

    rowb = lax.broadcasted_iota(jnp.int32, (nbp, bs), 0)
    rk = lax.broadcasted_iota(jnp.int32, (bs, bs), 0)
    cq = lax.broadcasted_iota(jnp.int32, (bs, bs), 1)
    rel = (cq - rk).astype(F32)
    km = km_scr[...]
    slopes = [slopes_ref[n_ch * pr + c] for c in range(n_ch)]
    kd = kb_scr[qi]
    for c in range(n_ch):
        par = c // 2
        gate = _dot_precise(km, qts[c])
        gate = jnp.where(rowb < qi, gate, NEG_INF)
        sel_scr[c] = jnp.where(_topk_mask_rows(gate, rowb, MOBA_TOPK), 0.0, NEG_INF)
        bias = -slopes[c] * rel
        bias_scr[c] = bias
        qtb = (qts[c] * (hd ** -0.5)).astype(BF16)
        qt_scr[c] = qtb
        s = _dot(kd, qtb) + bias
        s = jnp.where(rel >= 0.0, s, NEG_INF)
        m0 = jnp.max(s, axis=0, keepdims=True)
        p = jnp.exp(s - m0)
        m_scr[c] = m0
        l_scr[c] = jnp.sum(p, axis=0, keepdims=True)
        acc_scr[c] = _dot(vt_scr[qi, par * hd:(par + 1) * hd, :], p.astype(BF16))

    def scores(j, slot):
        kj = kb_scr[j]
        dist0 = jnp.asarray((qi - j) * bs, dtype=F32)
        for c in range(n_ch):
            colb = sel_scr[c, pl.ds(j, 1), :] - slopes[c] * dist0
            sj = _dot(kj, qt_scr[c]) + bias_scr[c] + colb
            m_old = m_scr[c]
            m_new = jnp.maximum(m_old, jnp.max(sj, axis=0, keepdims=True))
            al_scr[slot, c] = jnp.exp(m_old - m_new)
            m_scr[c] = m_new
            s_scr[slot, c] = sj - m_new

    def accumulate(j, slot):
        for c in range(n_ch):
            par = c // 2
            pj = jnp.exp(s_scr[slot, c])
            alpha = al_scr[slot, c]
            l_scr[c] = alpha * l_scr[c] + jnp.sum(pj, axis=0, keepdims=True)
            pv = _dot(vt_scr[j, par * hd:(par + 1) * hd, :], pj.astype(BF16))
            acc_scr[c] = alpha * acc_scr[c] + pv

    @pl.when(qi > 0)
    def _():
        scores(0, 0)

        def body(j, carry):
            scores(j + 1, (j + 1) % 2)
            accumulate(j, j % 2)
            return carry

        lax.fori_loop(0, qi - 1, body, 0)
        accumulate(qi - 1, (qi - 1) % 2)

    for par in range(2):
        o_g0 = acc_scr[2 * par] / l_scr[2 * par]
        o_g1 = acc_scr[2 * par + 1] / l_scr[2 * par + 1]
        o_ref[:, par * LANES:(par + 1) * LANES] = jnp.concatenate([o_g0, o_g1], axis=0).T.astype(o_ref.dtype)


def _attn_prompt(proj, slopes, bsz, t, n_kv, q_col, k_col, v_col):
    n = proj.shape[0]
    bs = MOBA_BLOCK
    nb = t // bs
    nbp = -(-nb // 16) * 16
    n_pairs = n_kv // 2
    qw = 2 * LANES
    hd = ATTN_HEAD_DIM
    grid_spec = pltpu.PrefetchScalarGridSpec(
        num_scalar_prefetch=1,
        grid=(bsz, n_pairs, nb),
        in_specs=[pl.BlockSpec((bs, qw), lambda b, h, i, s: (b * nb + i, q_col // qw + h)),
                  pl.BlockSpec((t, LANES), lambda b, h, i, s: (b, k_col // LANES + h)),
                  pl.BlockSpec((t, LANES), lambda b, h, i, s: (b, v_col // LANES + h))],
        out_specs=pl.BlockSpec((bs, qw), lambda b, h, i, s: (b * nb + i, h)),
        scratch_shapes=[pltpu.VMEM((nb, bs, LANES), BF16),
                        pltpu.VMEM((nb, LANES, bs), BF16),
                        pltpu.VMEM((nbp, LANES), F32),
                        pltpu.VMEM((4, nbp, bs), F32),
                        pltpu.VMEM((4, bs, bs), F32),
                        pltpu.VMEM((4, LANES, bs), BF16),
                        pltpu.VMEM((4, 1, bs), F32),
                        pltpu.VMEM((4, 1, bs), F32),
                        pltpu.VMEM((4, hd, bs), F32),
                        pltpu.VMEM((2, 4, bs, bs), F32),
                        pltpu.VMEM((2, 4, 1, bs), F32)])
    return pl.pallas_call(
        functools.partial(_attn_prompt_kernel, nb),
        out_shape=jax.ShapeDtypeStruct((n, n_kv * LANES), BF16),
        grid_spec=grid_spec,
        compiler_params=_cparams(3),
        name="attn_prompt",
        cost_estimate=pl.CostEstimate(flops=2 * bsz * n_kv * 2 * (nb * (nb + 1) // 2) * bs * bs * 2 * LANES,
                                      transcendentals=bsz * n_kv * 2 * (nb * (nb + 1) // 2) * bs * bs,
                                      bytes_accessed=n * (4 * (qw * n_pairs + 2 * LANES * n_pairs) + 2 * qw * n_pairs)),
    )(slopes, proj, proj, proj)


def _diag_extract(o, n_kv, rows_per_kv):
    hd = ATTN_HEAD_DIM
    tiles = []
    for k in range(n_kv):
        tl = o[k * rows_per_kv:(k + 1) * rows_per_kv, (k // 2) * LANES:(k // 2 + 1) * LANES]
        if k % 2 == 1:
            tl = pltpu.roll(tl, hd, 1)
        tiles.append(tl)
    return jnp.concatenate(tiles, axis=0)


def _attn_sample_kernel(bps, n_kv, tq, past_len, pt_ref, *refs):
    npg = 2 * bps
    kp = refs[0:npg]
    vp = refs[npg:2 * npg]
    qbd_ref, slope_ref, trow_ref, knew_ref, vnew_ref, o_ref, st_scr, km_scr, kn_scr, vn_scr, sel_scr = refs[2 * npg:]
    bs = MOBA_BLOCK
    hd = ATTN_HEAD_DIM
    rows = qbd_ref.shape[1]
    kvw = qbd_ref.shape[2]
    page = kp[0].shape[-1]
    rpk = rows // n_kv
    s_id = pl.program_id(1)
    n_steps = pl.num_programs(1)
    nblk = past_len // bs

    @pl.when(s_id == 0)
    def _():
        km_scr[...] = jnp.zeros(km_scr.shape, F32)

    qf = qbd_ref[0]
    qb = (qf * (hd ** -0.5)).astype(BF16)
    slope = slope_ref[...]
    lane = lax.broadcasted_iota(jnp.int32, (rows, LANES), 1)
    lane_k = lax.broadcasted_iota(jnp.int32, (kvw, LANES), 1)
    sc0 = slope * lane.astype(F32)
    sc1 = sc0 + slope * float(page)

    for bi in range(bps):
        j = s_id * bps + bi
        kt0 = kp[2 * bi][0].reshape(kvw, page)
        kt1 = kp[2 * bi + 1][0].reshape(kvw, page)
        vt0 = vp[2 * bi][0].reshape(kvw, page)
        vt1 = vp[2 * bi + 1][0].reshape(kvw, page)
        ksum = jnp.sum(kt0 + kt1, axis=1, keepdims=True) * (1.0 / bs)
        km_scr[...] = jnp.where(lane_k == j, ksum, km_scr[...])
        s0 = _dot(qb, kt0.astype(BF16)) + sc0
        s1 = _dot(qb, kt1.astype(BF16)) + sc1
        m = jnp.maximum(jnp.max(s0, axis=-1, keepdims=True), jnp.max(s1, axis=-1, keepdims=True))
        p0 = jnp.exp(s0 - m)
        p1 = jnp.exp(s1 - m)
        l = jnp.sum(p0, axis=-1, keepdims=True) + jnp.sum(p1, axis=-1, keepdims=True)
        o = _dot_nt(p0.astype(BF16), vt0.astype(BF16)) + _dot_nt(p1.astype(BF16), vt1.astype(BF16))
        od = _diag_extract(o, n_kv, rpk)
        tile = jnp.where(lane < hd, od, jnp.where(lane == hd, m, l))
        st_scr[j] = tile.T[0:hd + SUBLANES]

    @pl.when(s_id == n_steps - 1)
    def _():
        trow = trow_ref[...]
        gate = _dot_precise(qf, km_scr[...])
        gate = jnp.where(lane < nblk, gate, NEG_INF)
        sel, _ = _topk_mask(gate, lane, MOBA_TOPK)
        selneg = jnp.where(sel, 0.0, NEG_INF)

        kn_scr[...] = jnp.zeros(kn_scr.shape, F32)
        vn_scr[...] = jnp.zeros(vn_scr.shape, F32)
        kn_scr[0:tq, :] = knew_ref[...]
        vn_scr[0:tq, :] = vnew_ref[...]
        lanef = lane.astype(F32)
        so = _dot_nt(qb, kn_scr[...].astype(BF16)) - slope * (trow - lanef)
        so = jnp.where(jnp.logical_and(lanef <= trow, lane < tq), so, NEG_INF)
        m_own = jnp.max(so, axis=-1, keepdims=True)
        po = jnp.exp(so - m_own)
        den_own = jnp.sum(po, axis=-1, keepdims=True)
        num_own = _diag_extract(_dot(po.astype(BF16), vn_scr[...].astype(BF16)), n_kv, rpk)
        own_t = jnp.where(lane < hd, num_own, jnp.where(lane == hd, m_own, den_own)).T

        sel_scr[...] = selneg.T
        slope_row = slope.T[0:1]
        base_row = -slope_row * (float(past_len) + trow.T[0:1])

        def body(j, carry):
            m_run, num, den = carry
            st = st_scr[j]
            mj = st[hd:hd + 1] + (base_row + slope_row * jnp.asarray(j * bs, dtype=F32)) + sel_scr[pl.ds(j, 1), :]
            m_new = jnp.maximum(m_run, mj)
            a = jnp.exp(m_run - m_new)
            b = jnp.exp(mj - m_new)
            return m_new, a * num + b * st[0:hd], a * den + b * st[hd + 1:hd + 2]

        init = (own_t[hd:hd + 1], own_t[0:hd], own_t[hd + 1:hd + 2])
        _, num, den = lax.fori_loop(0, nblk, body, init)
        out_t = jnp.concatenate([num / den, jnp.zeros((LANES - hd, rows), F32)], axis=0)
        o_ref[0] = out_t.T


def _attn_sample(cache_kt, cache_vt, layer, page_table, qbd, slope_rows, t_rows, k_new, v_new, tq, n_kv, past_len):
    nbat, rows, kvw = qbd.shape
    page = cache_kt.shape[4]
    hd = cache_kt.shape[3]
    bs = MOBA_BLOCK
    nblk = past_len // bs
    bps = 4
    while nblk % bps:
        bps //= 2
    n_steps = nblk // bps
    assert bs == 2 * page and nblk <= LANES
    npg = 2 * bps

    def page_spec(r):
        return pl.BlockSpec((None, 1, n_kv, hd, page), lambda b, s, pt: (layer, pt[b, s * npg + r], 0, 0, 0))

    in_specs = ([page_spec(r) for r in range(npg)] + [page_spec(r) for r in range(npg)]
                + [pl.BlockSpec((1, rows, kvw), lambda b, s, pt: (b, 0, 0)),
                   pl.BlockSpec((rows, LANES), lambda b, s, pt: (0, 0)),
                   pl.BlockSpec((rows, LANES), lambda b, s, pt: (0, 0)),
                   pl.BlockSpec((tq, kvw), lambda b, s, pt: (b, 0)),
                   pl.BlockSpec((tq, kvw), lambda b, s, pt: (b, 0))])
    grid_spec = pltpu.PrefetchScalarGridSpec(
        num_scalar_prefetch=1,
        grid=(nbat, n_steps),
        in_specs=in_specs,
        out_specs=pl.BlockSpec((1, rows, LANES), lambda b, s, pt: (b, 0, 0)),
        scratch_shapes=[pltpu.VMEM((nblk, hd + SUBLANES, rows), F32),
                        pltpu.VMEM((kvw, LANES), F32),
                        pltpu.VMEM((LANES, kvw), F32),
                        pltpu.VMEM((LANES, kvw), F32),
                        pltpu.VMEM((LANES, rows), F32)])
    assert rows == LANES
    return pl.pallas_call(
        functools.partial(_attn_sample_kernel, bps, n_kv, tq, past_len),
        out_shape=jax.ShapeDtypeStruct((nbat, rows, LANES), F32),
        grid_spec=grid_spec,
        compiler_params=_cparams(2),
        name="attn_sample",
        cost_estimate=pl.CostEstimate(flops=2 * 2 * nbat * rows * past_len * kvw,
                                      transcendentals=nbat * rows * past_len,
                                      bytes_accessed=2 * 4 * nbat * past_len * kvw),
    )(page_table, *([cache_kt] * npg), *([cache_vt] * npg), qbd, slope_rows, t_rows, k_new, v_new)


def _outproj_kernel(alpha, n_exp, ys_ref, ya_ref, gs_ref, ga_ref, x_ref, gm_ref, scf_ref, shf_ref,
                    wos_ref, woa_ref, wout_ref, g1_ref, b1_ref, wr_ref, br_ref,
                    x1_ref, hf_ref, meta_ref, cnt_ref, cnt_scr):
    tm = x_ref.shape[0]
    merged = (jax.nn.sigmoid(gs_ref[...]) * _dot(ys_ref[...].astype(BF16), wos_ref[...])
              + jax.nn.sigmoid(ga_ref[...]) * _dot(ya_ref[...], woa_ref[...]))
    upd = _dot(merged.astype(BF16), wout_ref[...])
    x1 = _layer_norm(alpha * x_ref[...] + gm_ref[0] * upd, g1_ref[...], b1_ref[...])
    x1_ref[...] = x1
    hf = x1 * scf_ref[0] + shf_ref[0]
    hf_ref[...] = _pack_bf16_pairs(hf)
    logits = _dot_precise(hf, wr_ref[...]) + br_ref[...]
    lane = lax.broadcasted_iota(jnp.int32, logits.shape, 1)
    logits = jnp.where(lane < n_exp, logits, NEG_INF)
    g = logits
    picks, vals, ids = [], [], []
    for _ in range(TOP_K):
        m = jnp.max(g, axis=-1, keepdims=True)
        idx = jnp.min(jnp.where(g == m, lane, LANES), axis=-1, keepdims=True)
        pick = lane == idx
        g = jnp.where(pick, NEG_INF, g)
        picks.append(pick)
        vals.append(m)
        ids.append(idx)
    es = [jnp.exp(v - vals[0]) for v in vals]
    inv = 1.0 / (es[0] + es[1] + es[2] + es[3])

    @pl.when(pl.program_id(0) == 0)
    def _():
        cnt_scr[...] = jnp.zeros(cnt_scr.shape, F32)

    onehot = jnp.zeros(logits.shape, F32)
    for pick in picks:
        onehot = onehot + jnp.where(pick, 1.0, 0.0)
    ri = lax.broadcasted_iota(jnp.int32, (tm, tm), 0)
    ci = lax.broadcasted_iota(jnp.int32, (tm, tm), 1)
    ltri = jnp.where(ri > ci, 1.0, 0.0).astype(BF16)
    rank_mat = _dot(ltri, onehot.astype(BF16)) + cnt_scr[0:1, :]
    cnt_new = cnt_scr[0:1, :] + jnp.sum(onehot, axis=0, keepdims=True)
    cnt_scr[...] = jnp.broadcast_to(cnt_new, cnt_scr.shape)
    cnt_ref[...] = jnp.broadcast_to(cnt_new, cnt_ref.shape)
    meta = jnp.zeros(logits.shape, F32)
    for k in range(TOP_K):
        rank_k = jnp.sum(jnp.where(picks[k], rank_mat, 0.0), axis=-1, keepdims=True)
        meta = jnp.where(lane == k, ids[k].astype(F32), meta)
        meta = jnp.where(lane == TOP_K + k, rank_k, meta)
        meta = jnp.where(lane == 2 * TOP_K + k, es[k] * inv, meta)
    meta_ref[...] = meta


def _outproj(alpha, n_exp, ys, ya, proj, x, gm, scf1, shf, wos, woa, wout, g1, b1, wr, br, tm, tiles_per_group,
             gs_blk, ga_blk):
    n, d = x.shape
    r = gm.shape[1]
    grp = lambda i: (i // tiles_per_group, 0, 0)
    const = lambda i: (0, 0)
    return pl.pallas_call(
        functools.partial(_outproj_kernel, alpha, n_exp),
        out_shape=(jax.ShapeDtypeStruct((n, d), F32),
                   jax.ShapeDtypeStruct((n, d // 2), jnp.uint32),
                   jax.ShapeDtypeStruct((n, LANES), F32),
                   jax.ShapeDtypeStruct((SUBLANES, LANES), F32)),
        grid=(n // tm,),
        in_specs=[pl.BlockSpec((tm, ys.shape[1]), lambda i: (i, 0)),
                  pl.BlockSpec((tm, ya.shape[1]), lambda i: (i, 0)),
                  pl.BlockSpec((tm, d), lambda i: (i, gs_blk)),
                  pl.BlockSpec((tm, d), lambda i: (i, ga_blk)),
                  pl.BlockSpec((tm, d), lambda i: (i, 0)),
                  pl.BlockSpec((1, r, d), grp),
                  pl.BlockSpec((1, r, d), grp),
                  pl.BlockSpec((1, r, d), grp),
                  pl.BlockSpec(wos.shape, const),
                  pl.BlockSpec(woa.shape, const),
                  pl.BlockSpec(wout.shape, const),
                  pl.BlockSpec((1, d), const),
                  pl.BlockSpec((1, d), const),
                  pl.BlockSpec((d, LANES), const),
                  pl.BlockSpec((1, LANES), const)],
        out_specs=(pl.BlockSpec((tm, d), lambda i: (i, 0)),
                   pl.BlockSpec((tm, d // 2), lambda i: (i, 0)),
                   pl.BlockSpec((tm, LANES), lambda i: (i, 0)),
                   pl.BlockSpec((SUBLANES, LANES), const)),
        scratch_shapes=[pltpu.VMEM((SUBLANES, LANES), F32)],
        compiler_params=_cparams(1),
        name="outproj",
        cost_estimate=pl.CostEstimate(flops=2 * n * d * (ys.shape[1] + ya.shape[1] + d + 3 * LANES + tm),
                                      transcendentals=2 * n * d,
                                      bytes_accessed=n * (2 * ys.shape[1] + 2 * ya.shape[1] + 4 * 5 * d)),
    )(ys, ya, proj, proj, x, gm, scf1, shf, wos, woa, wout, g1, b1, wr, br)


def _sc_gather(table, idx):
    _, dcol = table.shape
    b = idx.shape[0]
    nw = SC_CORES * SC_SUBCORES
    ch = SC_GATHER_ROWS
    assert b % (nw * 2 * ch) == 0
    b_per_w = b // nw
    n_ch = b_per_w // ch
    mesh = plsc.VectorSubcoreMesh(core_axis_name="c", subcore_axis_name="s")

    @functools.partial(
        pl.kernel, mesh=mesh,
        out_type=jax.ShapeDtypeStruct((b, dcol), table.dtype),
        scratch_types=[pltpu.VMEM((ch,), jnp.int32), pltpu.VMEM((ch,), jnp.int32),
                       pltpu.VMEM((ch, dcol), table.dtype), pltpu.VMEM((ch, dcol), table.dtype),
                       pltpu.SemaphoreType.DMA, pltpu.SemaphoreType.DMA],
        name="sc_gather",
        cost_estimate=pl.CostEstimate(flops=0, transcendentals=0,
                                      bytes_accessed=2 * b * dcol * table.dtype.itemsize + 4 * b),
    )
    def gather(table_hbm, idx_hbm, out_hbm, idx0, idx1, rows0, rows1, sem0, sem1):
        idx_v, rows_v, sems = (idx0, idx1), (rows0, rows1), (sem0, sem1)
        wid = lax.axis_index("s") * SC_CORES + lax.axis_index("c")
        base = wid * b_per_w

        def start(c, slot):
            off = pl.multiple_of(base + c * ch, ch)
            pltpu.sync_copy(idx_hbm.at[pl.ds(off, ch)], idx_v[slot])
            pltpu.async_copy(table_hbm.at[idx_v[slot]], rows_v[slot], sems[slot])

        def finish(c, slot):
            pltpu.make_async_copy(table_hbm.at[idx_v[slot]], rows_v[slot], sems[slot]).wait()
            off = pl.multiple_of(base + c * ch, ch)
            pltpu.sync_copy(rows_v[slot], out_hbm.at[pl.ds(off, ch)])

        start(0, 0)

        @pl.loop(0, n_ch, step=2)
        def _(c):
            start(c + 1, 1)
            finish(c, 0)

            @pl.when(c + 2 < n_ch)
            def _():
                start(c + 2, 0)

            finish(c + 1, 1)

    return gather(table, idx)


def _moe_experts_kernel(d_ff, te_ref, x_ref, wgu_ref, bgu_ref, wdn_ref, bdn_ref, o_ref, wgu_scr, wdn_scr):
    i = pl.program_id(0)
    prev = te_ref[jnp.maximum(i - 1, 0)]

    @pl.when(jnp.logical_or(i == 0, te_ref[i] != prev))
    def _():
        wgu_scr[...] = wgu_ref[0].astype(BF16)
        wdn_scr[...] = wdn_ref[0].astype(BF16)

    xa, xb = _unpack_bf16_pairs(x_ref[...])
    half = xa.shape[1]
    gu = (_dot(xa, wgu_scr[0:half, :]) + _dot(xb, wgu_scr[half:2 * half, :])) + bgu_ref[0]
    gg = jnp.minimum(gu[:, :d_ff], SWIGLU_LIMIT)
    uu = jnp.clip(gu[:, d_ff:], -SWIGLU_LIMIT, SWIGLU_LIMIT)
    act = (uu + 1.0) * (gg * jax.nn.sigmoid(SWIGLU_ALPHA * gg))
    o_ref[...] = _dot(act.astype(BF16), wdn_scr[...]) + bdn_ref[0]


def _moe_experts(xs, tile_expert, layer, w_gu, b_gu4, w_down, b_down4, tme):
    p = xs.shape[0]
    d = w_gu.shape[2]
    two_ff = w_gu.shape[3]
    d_ff = two_ff // 2
    grid_spec = pltpu.PrefetchScalarGridSpec(
        num_scalar_prefetch=1,
        grid=(p // tme,),
        in_specs=[pl.BlockSpec((tme, d // 2), lambda i, te: (i, 0)),
                  pl.BlockSpec((None, 1, d, two_ff), lambda i, te: (layer, te[i], 0, 0)),
                  pl.BlockSpec((None, 1, 1, two_ff), lambda i, te: (layer, te[i], 0, 0)),
                  pl.BlockSpec((None, 1, d_ff, d), lambda i, te: (layer, te[i], 0, 0)),
                  pl.BlockSpec((None, 1, 1, d), lambda i, te: (layer, te[i], 0, 0))],
        out_specs=pl.BlockSpec((tme, d), lambda i, te: (i, 0)),
        scratch_shapes=[pltpu.VMEM((d, two_ff), BF16),
                        pltpu.VMEM((d_ff, d), BF16)])
    return pl.pallas_call(
        functools.partial(_moe_experts_kernel, d_ff),
        out_shape=jax.ShapeDtypeStruct((p, d), F32),
        grid_spec=grid_spec,
        compiler_params=_cparams(1),
        name="moe_experts",
        cost_estimate=pl.CostEstimate(flops=2 * p * d * 3 * d_ff, transcendentals=p * d_ff,
                                      bytes_accessed=6 * p * d + 4 * w_gu.shape[1] * 3 * d * d_ff),
    )(tile_expert, xs, w_gu, b_gu4, w_down, b_down4)


def _moe_combine_kernel(alpha, *refs):
    yk_refs = refs[:TOP_K]
    meta_ref, x1_ref, gf_ref, g2_ref, b2_ref, o_ref = refs[TOP_K:]
    meta = meta_ref[...]
    acc = meta[:, 2 * TOP_K:2 * TOP_K + 1] * yk_refs[0][...]
    for k in range(1, TOP_K):
        acc = acc + meta[:, 2 * TOP_K + k:2 * TOP_K + k + 1] * yk_refs[k][...]
    o_ref[...] = _layer_norm(alpha * x1_ref[...] + gf_ref[0] * acc, g2_ref[...], b2_ref[...])


def _moe_combine(alpha, yk, n_pad, meta, x1, gf, g2, b2, tm, tiles_per_group):
    n, d = x1.shape
    r = gf.shape[1]
    bpk = n_pad // tm
    assert n_pad % tm == 0

    def choice_spec(k):
        return pl.BlockSpec((tm, d), lambda i: (k * bpk + i, 0))

    return pl.pallas_call(
        functools.partial(_moe_combine_kernel, alpha),
        out_shape=jax.ShapeDtypeStruct((n, d), F32),
        grid=(n // tm,),
        in_specs=[choice_spec(k) for k in range(TOP_K)] + [
                  pl.BlockSpec((tm, LANES), lambda i: (i, 0)),
                  pl.BlockSpec((tm, d), lambda i: (i, 0)),
                  pl.BlockSpec((1, r, d), lambda i: (i // tiles_per_group, 0, 0)),
                  pl.BlockSpec((1, d), lambda i: (0, 0)),
                  pl.BlockSpec((1, d), lambda i: (0, 0))],
        out_specs=pl.BlockSpec((tm, d), lambda i: (i, 0)),
        compiler_params=_cparams(1),
        name="moe_combine",
        cost_estimate=pl.CostEstimate(flops=2 * TOP_K * n * d, transcendentals=n,
                                      bytes_accessed=4 * n * d * (TOP_K + 2)),
    )(*([yk] * TOP_K), meta, x1, gf, g2, b2)


def _moe_route(meta, counts, tile, tm):
    n = meta.shape[0]
    n_exp = counts.shape[0]
    gran = SC_CORES * SC_SUBCORES * 2 * SC_GATHER_ROWS
    unit = gran * tile // math.gcd(gran, tile)
    eid = meta[:, 0:TOP_K].astype(jnp.int32)
    rank = meta[:, TOP_K:2 * TOP_K].astype(jnp.int32)
    padded = -(-counts // tile) * tile
    ends = jnp.cumsum(padded)
    is_e = eid[:, :, None] == jnp.arange(n_exp, dtype=jnp.int32)
    dest = jnp.sum(jnp.where(is_e, (ends - padded)[None, None, :], 0), axis=-1) + rank
    p_rows = -(-(n * TOP_K + n_exp * tile) // unit) * unit
    tok = jnp.broadcast_to(jnp.arange(n, dtype=jnp.int32)[:, None], (n, TOP_K))
    src_tok = jnp.zeros((p_rows,), jnp.int32).at[dest.reshape(-1)].set(
        tok.reshape(-1), unique_indices=True, mode="promise_in_bounds")
    tile_start = jnp.arange(p_rows // tile, dtype=jnp.int32) * tile
    tile_expert = jnp.minimum(jnp.sum(tile_start[:, None] >= ends[None, :], axis=-1), n_exp - 1).astype(jnp.int32)
    quant = max(gran // TOP_K, tm)
    n_pad = -(-n // quant) * quant
    dest_km = jnp.pad(dest.T, ((0, 0), (0, n_pad - n))).reshape(-1)
    return dict(src_tok=src_tok, tile_expert=tile_expert, dest_km=dest_km, n_pad=n_pad, tile=tile)


def _moe_group(alpha, layer, route, hf, meta, x1, gf, g2, b2, w_gu, b_gu4, w_down, b_down4, tm, tiles_per_group):
    xs_sorted = _sc_gather(hf, route["src_tok"])
    ys_sorted = _moe_experts(xs_sorted, route["tile_expert"], layer, w_gu, b_gu4, w_down, b_down4, route["tile"])
    yk = _sc_gather(ys_sorted, route["dest_km"])
    return yk, functools.partial(_moe_combine, alpha, n_pad=route["n_pad"], meta=meta, x1=x1, gf=gf, g2=g2, b2=b2,
                                 tm=tm, tiles_per_group=tiles_per_group)


def _pad_lanes(v, fill=0.0):
    return jnp.pad(v, (0, LANES - v.shape[0]), constant_values=fill).reshape(1, LANES)


def kernel(x_prompt, x_sample, c_prompt, c_sample, cache_k, cache_v, state_ssm, state_conv, page_table, w_ada, b_ada, w_in, conv_w, conv_b, dt_bias, a_log, d_skip, ssm_norm_g, w_o_ssm, w_o_attn, w_out, ln1_g, ln1_b, ln2_g, ln2_b, w_router, b_router, w_gu, b_gu, w_down, b_down):
    depth, d, _ = w_ada.shape
    bp, tp, _ = x_prompt.shape
    bsm, ts, _ = x_sample.shape
    n_heads = dt_bias.shape[1]
    d_inner = n_heads * SSM_HEAD_DIM
    conv_ch = conv_w.shape[2]
    kvw = cache_k.shape[3] * cache_k.shape[4]
    n_kv = cache_k.shape[3]
    aw = w_o_attn.shape[1]
    n_q = aw // ATTN_HEAD_DIM
    n_exp = w_router.shape[2]
    page = cache_k.shape[2]
    past_len = page_table.shape[1] * page
    alpha = float((2.0 * depth) ** 0.25)
    assert d_inner == 2 * d and aw == d and kvw == d // 2 and n_q == 2 * n_kv
    assert conv_ch == d_inner + 2 * SSM_GROUPS * SSM_STATE and conv_ch == 3 * d
    assert tp % MOBA_BLOCK == 0 and tp % SSD_CHUNK == 0 and ts == SUBLANES and n_heads <= LANES

    o_z, o_xbc, o_dt = 0, d_inner, d_inner + conv_ch
    o_q = o_dt + n_heads
    o_k, o_v = o_q + aw, o_q + aw + kvw
    o_gs, o_ga = o_v + kvw, o_v + kvw + d
    c_z, c_q, c_gs, c_ga = 0, d_inner, d_inner + d, d_inner + 2 * d
    c_k = d_inner + 3 * d
    c_v = c_k + kvw
    c_xbc = c_v + kvw
    assert c_xbc % conv_ch == 0

    slopes = 2.0 ** (-8.0 * (jnp.arange(n_q, dtype=F32) + 1.0) / n_q)

    n_c = bsm + bp
    n_c_pad = -(-n_c // SUBLANES) * SUBLANES
    c_all = jnp.concatenate([c_sample, c_prompt, jnp.zeros((n_c_pad - n_c, d), F32)], axis=0)
    mod = _ada(c_all, w_ada, b_ada)

    grp = n_q // n_kv
    rows = n_kv * grp * ts
    r_idx = jnp.arange(rows)
    r_head = (r_idx // (grp * ts)) * grp + (r_idx // ts) % grp
    slope_rows = jnp.broadcast_to(slopes[r_head][:, None], (rows, LANES))
    t_rows = jnp.broadcast_to((r_idx % ts).astype(F32)[:, None], (rows, LANES))
    eye_kv = jnp.eye(n_kv, dtype=F32)

    cache_kt = jnp.transpose(cache_k, (0, 1, 3, 4, 2))
    cache_vt = jnp.transpose(cache_v, (0, 1, 3, 4, 2))

    xp = x_prompt.reshape(bp * tp, d)
    xs = x_sample.reshape(bsm * ts, d)
    tm_p = 1024 if tp % 1024 == 0 else MOBA_BLOCK
    tm_o = 512 if tp % 512 == 0 else MOBA_BLOCK
    n_s = bsm * ts

    def make_layer(l):
        wl = w_in[l]
        w_main = jnp.concatenate([wl[:, o_z:o_z + d_inner], wl[:, o_q:o_q + aw], wl[:, o_gs:o_gs + d],
                                  wl[:, o_ga:o_ga + d], wl[:, o_k:o_k + kvw], wl[:, o_v:o_v + kvw],
                                  wl[:, o_xbc:o_xbc + conv_ch]], axis=1).astype(BF16)
        w_dt = jnp.pad(wl[:, o_dt:o_dt + n_heads], ((0, 0), (0, LANES - n_heads))).astype(BF16)
        w_dtT = wl[:, o_dt:o_dt + n_heads].T.astype(BF16)
        dtb = _pad_lanes(dt_bias[l])
        dtbT = jnp.broadcast_to(dt_bias[l][:, None], (n_heads, SSD_CHUNK))
        a_neg = -jnp.exp(a_log[l])
        a_row = _pad_lanes(a_neg)
        aT = jnp.broadcast_to(a_neg[:, None], (n_heads, SSD_CHUNK))
        dskip = jnp.repeat(d_skip[l], SSM_HEAD_DIM).reshape(1, d_inner)
        ng = ssm_norm_g[l].reshape(1, d_inner)
        cw = conv_w[l]
        cb = conv_b[l].reshape(1, conv_ch)
        wos = w_o_ssm[l].astype(BF16)
        woa = w_o_attn[l].astype(BF16)
        wout = w_out[l].astype(BF16)
        g1, b1 = ln1_g[l].reshape(1, d), ln1_b[l].reshape(1, d)
        g2, b2 = ln2_g[l].reshape(1, d), ln2_b[l].reshape(1, d)
        wr = jnp.pad(w_router[l], ((0, 0), (0, LANES - n_exp)))
        br = _pad_lanes(b_router[l])

        mod_l = mod[l]
        sh_m, sc_m, g_m, sh_f, sc_f, g_f = [mod_l[:, i * d:(i + 1) * d] for i in range(6)]

        def group_mod(v, is_prompt):
            if is_prompt:
                return v[bsm:bsm + bp].reshape(bp, 1, d)
            return jnp.repeat(v[:bsm], ts, axis=0).reshape(1, n_s, d)

        def run(x, is_prompt, h0T, cprev8):
            bsz, t = (bp, tp) if is_prompt else (bsm, ts)
            n = bsz * t
            tm = tm_p if is_prompt else n
            tmo = tm_o if is_prompt else n
            tpg = (t // tm) if is_prompt else 1
            tpgo = (t // tmo) if is_prompt else 1
            gm = lambda v: group_mod(v, is_prompt)
            proj, dt, dtT, k_new, v_new = _inproj(x, gm(1.0 + sc_m), gm(sh_m), w_main, w_dt, w_dtT, tm, tpg,
                                                  c_k, kvw)
            dtT_seq = dtT.reshape(n_heads, bsz, t).transpose(1, 0, 2)
            lr = SSD_CHUNK if is_prompt else t
            y_s, h_newT = _ssd(proj, dt, dtT_seq, h0T, cprev8, cw, cb, dtb, dtbT, a_row, aT, dskip, ng,
                               bsz, t, lr, c_z // d_inner, c_xbc // conv_ch, d_inner, conv_ch)
            if is_prompt:
                y_a = _attn_prompt(proj, slopes, bsz, t, n_kv, c_q, c_k, c_v)
            else:
                q5 = proj[:, c_q:c_q + aw].reshape(bsz, t, n_kv, grp, ATTN_HEAD_DIM)
                q5 = q5.transpose(0, 2, 3, 1, 4).reshape(bsz, n_kv, grp * t, ATTN_HEAD_DIM)
                qbd = (q5[:, :, :, None, :] * eye_kv[None, :, None, :, None]).reshape(bsz, rows, kvw)
                o = _attn_sample(cache_kt, cache_vt, l, page_table, qbd, slope_rows, t_rows, k_new, v_new,
                                 t, n_kv, past_len)
                o = o[:, :, :ATTN_HEAD_DIM].reshape(bsz, n_kv, grp, t, ATTN_HEAD_DIM)
                y_a = o.transpose(0, 3, 1, 2, 4).reshape(n, aw).astype(BF16)
            x1, hf, meta, cnt = _outproj(alpha, n_exp, y_s, y_a, proj, x, gm(g_m), gm(1.0 + sc_f), gm(sh_f),
                                         wos, woa, wout, g1, b1, wr, br, tmo, tpgo, c_gs // d, c_ga // d)
            conv_new = proj.reshape(bsz, t, -1)[:, t - (CONV_W - 1):, c_xbc:c_xbc + conv_ch]
            h_new = h_newT.transpose(0, 2, 1).reshape(bsz, n_heads, SSM_HEAD_DIM, SSM_STATE)
            hd4 = (bsz, t, n_kv, ATTN_HEAD_DIM)
            return dict(x1=x1, hf=hf, meta=meta, cnt=cnt[0, :n_exp].astype(jnp.int32), gf=gm(g_f),
                        k=k_new.reshape(hd4), v=v_new.reshape(hd4), h=h_new, conv=conv_new)

        return run, g2, b2

    def sample_front(l, x):
        h0_s = state_ssm[l].reshape(bsm, d_inner, SSM_STATE).transpose(0, 2, 1)
        cv0_s = jnp.pad(state_conv[l], ((0, 0), (SUBLANES - (CONV_W - 1), 0), (0, 0)))
        return layers[l][0](x, False, h0_s, cv0_s)

    layers = [make_layer(l) for l in range(depth)]
    b_gu4 = b_gu.reshape(depth, n_exp, 1, -1)
    b_down4 = b_down.reshape(depth, n_exp, 1, d)
    h0_p = jnp.zeros((bp, SSM_STATE, d_inner), F32)
    cv0_p = jnp.zeros((bp, SUBLANES, conv_ch), F32)
    outs = {k: [] for k in ("kp", "vp", "hp", "cvp", "ks", "vs", "hs", "cvs")}
    fs = None
    for l in range(depth):
        run, g2, b2 = layers[l]
        fp = run(xp, True, h0_p, cv0_p)
        route_p = _moe_route(fp["meta"], fp["cnt"], MOE_TILE, MOBA_BLOCK)
        fs = sample_front(l, xs)
        outs["kp"].append(fp["k"]); outs["vp"].append(fp["v"]); outs["hp"].append(fp["h"]); outs["cvp"].append(fp["conv"])
        outs["ks"].append(fs["k"]); outs["vs"].append(fs["v"]); outs["hs"].append(fs["h"]); outs["cvs"].append(fs["conv"])

        yk_p, combine_p = _moe_group(alpha, l, route_p, fp["hf"], fp["meta"], fp["x1"], fp["gf"], g2, b2,
                                     w_gu, b_gu4, w_down, b_down4, MOBA_BLOCK, tp // MOBA_BLOCK)
        route_s = _moe_route(fs["meta"], fs["cnt"], MOE_TILE_SMALL, n_s)
        yk_s, combine_s = _moe_group(alpha, l, route_s, fs["hf"], fs["meta"], fs["x1"], fs["gf"], g2, b2,
                                     w_gu, b_gu4, w_down, b_down4, n_s, 1)
        xs = combine_s(yk_s)
        xp = combine_p(yk_p)

    st = lambda k: jnp.stack(outs[k])
    return (xp.reshape(bp, tp, d), xs.reshape(bsm, ts, d), st("kp"), st("vp"), st("hp"), st("cvp"),
            st("ks"), st("vs"), st("hs"), st("cvs"))
```

```python
import functools
import math

import jax
import jax.numpy as jnp
from jax import lax
from jax.experimental import pallas as pl
from jax.experimental.pallas import tpu as pltpu
from jax.experimental.pallas import tpu_sc as plsc

F32 = jnp.float32
BF16 = jnp.bfloat16
NEG_INF = float("-inf")

LANES = 128
SUBLANES = 8
VMEM_LIMIT = 56 * 1024 * 1024
SC_CORES = 2
SC_SUBCORES = 16
SC_GATHER_BYTES = 128 * 1024
MOE_TILE = 512
MOE_TILE_SMALL = 64

SSM_HEAD_DIM = 64
SSM_GROUPS = 4
SSM_STATE = 128
CONV_W = 4
SSD_CHUNK = 128
RMS_EPS = 1e-5
ATTN_HEAD_DIM = 64
MOBA_BLOCK = 256
MOBA_TOPK = 3
TOP_K = 4
SWIGLU_LIMIT = 7.0
SWIGLU_ALPHA = 1.702
LN_EPS = 1e-5


def _cparams(n_axes):
    return pltpu.CompilerParams(dimension_semantics=("arbitrary",) * n_axes,
                                vmem_limit_bytes=VMEM_LIMIT)


def _dot(a, b):
    return jnp.dot(a, b, preferred_element_type=F32)


def _dot_nt(a, b):
    return lax.dot_general(a, b, (((1,), (1,)), ((), ())), preferred_element_type=F32)


def _split2(a):
    hi = a.astype(BF16)
    lo = (a - hi.astype(F32)).astype(BF16)
    return hi, lo


def _split3(a):
    hi = a.astype(BF16)
    r = a - hi.astype(F32)
    mid = r.astype(BF16)
    lo = (r - mid.astype(F32)).astype(BF16)
    return hi, mid, lo


def _dot_nt_precise(a, b):
    ah, al = _split2(a)
    bh, bl = _split2(b)
    return _dot_nt(ah, bh) + (_dot_nt(ah, bl) + _dot_nt(al, bh))


def _dot_precise(a, b):
    ah, al = _split2(a)
    bh, bl = _split2(b)
    return _dot(ah, bh) + (_dot(ah, bl) + _dot(al, bh))


def _silu(x):
    return x * jax.nn.sigmoid(x)


def _softplus(x):
    return jnp.maximum(x, 0.0) + jnp.log1p(jnp.exp(-jnp.abs(x)))


def _layer_norm(x, g, b):
    mu = jnp.mean(x, axis=-1, keepdims=True)
    xc = x - mu
    var = jnp.mean(xc * xc, axis=-1, keepdims=True)
    return xc * lax.rsqrt(var + LN_EPS) * g + b


def _pack_bf16_pairs(x):
    w = x.shape[1] // 2
    bits = lax.bitcast_convert_type(x, jnp.uint32)
    top = (bits + (jnp.uint32(0x7FFF) + ((bits >> 16) & jnp.uint32(1)))) >> 16
    return top[:, :w] | (top[:, w:] << 16)


def _unpack_bf16_pairs(wd):
    lo = lax.bitcast_convert_type(wd << 16, F32)
    hi = lax.bitcast_convert_type(wd & jnp.uint32(0xFFFF0000), F32)
    return lo.astype(BF16), hi.astype(BF16)


def _topk_mask(score, lane, k):
    width = score.shape[-1]
    sel = jnp.zeros(score.shape, dtype=jnp.bool_)
    vals = []
    g = score
    for _ in range(k):
        m = jnp.max(g, axis=-1, keepdims=True)
        idx = jnp.min(jnp.where(g == m, lane, width), axis=-1, keepdims=True)
        pick = jnp.logical_and(lane == idx, m > NEG_INF)
        sel = jnp.logical_or(sel, pick)
        g = jnp.where(pick, NEG_INF, g)
        vals.append(m)
    return sel, vals


def _ada_kernel(c_ref, w_ref, b_ref, o_ref):
    s = _silu(c_ref[...]).astype(BF16)
    o_ref[0] = _dot(s, w_ref[0].astype(BF16)) + b_ref[0]


def _ada(c_all, w_ada, b_ada):
    depth, d, n6 = w_ada.shape
    rows = c_all.shape[0]
    tn = 1024
    return pl.pallas_call(
        _ada_kernel,
        out_shape=jax.ShapeDtypeStruct((depth, rows, n6), F32),
        grid=(depth, n6 // tn),
        in_specs=[pl.BlockSpec((rows, d), lambda l, j: (0, 0)),
                  pl.BlockSpec((1, d, tn), lambda l, j: (l, 0, j)),
                  pl.BlockSpec((1, 1, tn), lambda l, j: (l, 0, j))],
        out_specs=pl.BlockSpec((1, rows, tn), lambda l, j: (l, 0, j)),
        compiler_params=_cparams(2),
        name="ada",
    )(c_all, w_ada, b_ada.reshape(depth, 1, n6))


def _inproj_kernel(j_kv, kvw, x_ref, sc_ref, sh_ref, w_ref, wdt_ref, wdtT_ref, o_ref, dt_ref, dtT_ref, k_ref, v_ref,
                   h_scr):
    @pl.when(pl.program_id(1) == 0)
    def _():
        hb = (x_ref[...] * sc_ref[0] + sh_ref[0]).astype(BF16)
        h_scr[...] = hb
        dt_ref[...] = _dot(hb, wdt_ref[...])
        dtT_ref[...] = _dot_nt(wdtT_ref[...], hb)

    res = _dot(h_scr[...], w_ref[...])
    o_ref[...] = res

    @pl.when(pl.program_id(1) == j_kv)
    def _():
        k_ref[...] = res[:, 0:kvw]
        v_ref[...] = res[:, kvw:2 * kvw]


def _inproj(x, sc1, sh, w_main, w_dt, w_dtT, tm, tiles_per_group, k_col, kvw):
    n, d = x.shape
    n_main = w_main.shape[1]
    tn = 1024
    r = sc1.shape[1]
    nh = w_dtT.shape[0]
    assert k_col % tn == 0 and 2 * kvw <= tn
    grp = lambda i, j: (i // tiles_per_group, 0, 0)
    return pl.pallas_call(
        functools.partial(_inproj_kernel, k_col // tn, kvw),
        out_shape=(jax.ShapeDtypeStruct((n, n_main), F32),
                   jax.ShapeDtypeStruct((n, LANES), F32),
                   jax.ShapeDtypeStruct((nh, n), F32),
                   jax.ShapeDtypeStruct((n, kvw), F32),
                   jax.ShapeDtypeStruct((n, kvw), F32)),
        grid=(n // tm, n_main // tn),
        in_specs=[pl.BlockSpec((tm, d), lambda i, j: (i, 0)),
                  pl.BlockSpec((1, r, d), grp),
                  pl.BlockSpec((1, r, d), grp),
                  pl.BlockSpec((d, tn), lambda i, j: (0, j)),
                  pl.BlockSpec((d, LANES), lambda i, j: (0, 0)),
                  pl.BlockSpec((nh, d), lambda i, j: (0, 0))],
        out_specs=(pl.BlockSpec((tm, tn), lambda i, j: (i, j)),
                   pl.BlockSpec((tm, LANES), lambda i, j: (i, 0)),
                   pl.BlockSpec((nh, tm), lambda i, j: (0, i)),
                   pl.BlockSpec((tm, kvw), lambda i, j: (i, 0)),
                   pl.BlockSpec((tm, kvw), lambda i, j: (i, 0))),
        scratch_shapes=[pltpu.VMEM((tm, d), BF16)],
        compiler_params=_cparams(2),
        name="inproj",
        cost_estimate=pl.CostEstimate(flops=2 * n * d * (n_main + LANES + nh), transcendentals=0,
                                      bytes_accessed=4 * n * (d + n_main) + 2 * d * n_main * (n // tm)),
    )(x, sc1, sh, w_main, w_dt, w_dtT)


def _ssd_kernel(lr, d_inner, xbc_ref, z_ref, dt_ref, dtT_ref, h0_ref, cprev_ref, cw_ref, cb_ref,
                dtb_ref, dtbT_ref, a_ref, aT_ref, dskip_ref, ng_ref,
                y_ref, hout_ref, xp_scr, h_scr, y_scr, dt_scr, dtT_scr):
    L = SSD_CHUNK
    gs = SSM_GROUPS * SSM_STATE
    n_heads = d_inner // SSM_HEAD_DIM
    pairs_per_group = n_heads // SSM_GROUPS // 2
    c = pl.program_id(1)

    @pl.when(c == 0)
    def _():
        xp_scr[0:SUBLANES, :] = cprev_ref[0]
        h_scr[...] = h0_ref[0]

    if lr < L:
        xp_scr[SUBLANES + lr:SUBLANES + L, :] = jnp.zeros((L - lr, xp_scr.shape[1]), F32)
    xp_scr[SUBLANES:SUBLANES + lr, :] = xbc_ref[...]

    acc = cb_ref[...] + cw_ref[0:1, :] * xp_scr[SUBLANES - 3:SUBLANES - 3 + L, :]
    for j in range(1, CONV_W):
        acc = acc + cw_ref[j:j + 1, :] * xp_scr[SUBLANES - 3 + j:SUBLANES - 3 + j + L, :]
    tail = xp_scr[lr:lr + SUBLANES, :]
    xp_scr[0:SUBLANES, :] = tail
    xc = _silu(acc)

    if lr < L:
        dt_scr[...] = jnp.zeros(dt_scr.shape, F32)
        dt_scr[0:lr, :] = dt_ref[...]
        dtT_scr[...] = jnp.zeros(dtT_scr.shape, F32)
        dtT_scr[:, 0:lr] = dtT_ref[0]
        dt_raw = dt_scr[...]
        dtT_raw = dtT_scr[...]
    else:
        dt_raw = dt_ref[...]
        dtT_raw = dtT_ref[0]
    row = lax.broadcasted_iota(jnp.int32, (L, LANES), 0)
    col = lax.broadcasted_iota(jnp.int32, (n_heads, L), 1)
    dtv = jnp.where(row < lr, _softplus(dt_raw + dtb_ref[...]), 0.0)
    dtvT = jnp.where(col < lr, _softplus(dtT_raw + dtbT_ref[...]), 0.0)
    dA = dtv * a_ref[...]
    dAT = dtvT * aT_ref[...]

    ri = lax.broadcasted_iota(jnp.int32, (L, L), 0)
    ci = lax.broadcasted_iota(jnp.int32, (L, L), 1)
    causal = ri >= ci
    tri = jnp.where(causal, 1.0, 0.0).astype(BF16)
    triT = jnp.where(ri <= ci, 1.0, 0.0).astype(BF16)
    d1, d2, d3 = _split3(dA)
    la = _dot(tri, d1) + (_dot(tri, d2) + _dot(tri, d3))
    e1, e2, e3 = _split3(dAT)
    laT = _dot(e1, triT) + (_dot(e2, triT) + _dot(e3, triT))

    la_last = la[L - 1:L, :]
    to_end = jnp.exp(la_last - la)
    ela = jnp.exp(la)
    cdec = jnp.exp(la_last)

    lane = lax.broadcasted_iota(jnp.int32, (L, LANES), 1)
    lo_half = lane < SSM_HEAD_DIM
    lane1 = lax.broadcasted_iota(jnp.int32, (1, LANES), 1)
    eye = jnp.where(ri == ci, 1.0, 0.0).astype(BF16)

    def pair_bcast(arr, p):
        rows = arr.shape[0]
        a0 = jnp.broadcast_to(arr[:, 2 * p:2 * p + 1], (rows, LANES))
        a1 = jnp.broadcast_to(arr[:, 2 * p + 1:2 * p + 2], (rows, LANES))
        return jnp.where(lo_half[:rows], a0, a1)

    for g in range(SSM_GROUPS):
        bg = xc[:, d_inner + g * SSM_STATE:d_inner + (g + 1) * SSM_STATE].astype(BF16)
        cg = xc[:, d_inner + gs + g * SSM_STATE:d_inner + gs + (g + 1) * SSM_STATE].astype(BF16)
        cb = _dot_nt(cg, bg)
        bgT = _dot_nt(eye, bg).astype(BF16)
        for pp in range(pairs_per_group):
            p = g * pairs_per_group + pp
            sl = slice(p * LANES, (p + 1) * LANES)
            xs_p = xc[:, sl]
            xdt_p = xs_p * pair_bcast(dtv, p)
            ydiag = None
            for a in range(2):
                h = 2 * p + a
                seg = jnp.broadcast_to(la[:, h:h + 1], (L, L)) - laT[h:h + 1, :]
                decay = jnp.exp(jnp.where(causal, seg, NEG_INF))
                m = (cb * decay).astype(BF16)
                half = lo_half if a == 0 else jnp.logical_not(lo_half)
                part = _dot(m, jnp.where(half, xdt_p, 0.0).astype(BF16))
                ydiag = part if ydiag is None else ydiag + part
            hin = h_scr[:, sl]
            yoff = _dot(cg, hin.astype(BF16)) * pair_bcast(ela, p)
            y_scr[:, sl] = ydiag + yoff + dskip_ref[:, sl] * xs_p
            s_new = _dot(bgT, (xdt_p * pair_bcast(to_end, p)).astype(BF16))
            cd_p = jnp.where(lane1 < SSM_HEAD_DIM,
                             jnp.broadcast_to(cdec[:, 2 * p:2 * p + 1], (1, LANES)),
                             jnp.broadcast_to(cdec[:, 2 * p + 1:2 * p + 2], (1, LANES)))
            h_scr[:, sl] = hin * cd_p + s_new

    gw = d_inner // SSM_GROUPS
    for g in range(SSM_GROUPS):
        sl = slice(g * gw, (g + 1) * gw)
        yg = y_scr[0:lr, sl] * _silu(z_ref[:, sl])
        ms = jnp.mean(yg * yg, axis=-1, keepdims=True)
        y_ref[:, sl] = (yg * lax.rsqrt(ms + RMS_EPS) * ng_ref[:, sl]).astype(y_ref.dtype)

    @pl.when(c == pl.num_programs(1) - 1)
    def _():
        hout_ref[0] = h_scr[...]


def _ssd(proj, dt, dtT_seq, h0T, cprev8, cw, cb, dtb, dtbT, a, aT, dskip, ng, bsz, t, lr,
         z_blk, xbc_blk, d_inner, conv_ch):
    n = proj.shape[0]
    nc = t // lr
    n_heads = d_inner // SSM_HEAD_DIM
    L = SSD_CHUNK
    const2 = lambda b, c: (0, 0)
    return pl.pallas_call(
        functools.partial(_ssd_kernel, lr, d_inner),
        out_shape=(jax.ShapeDtypeStruct((n, d_inner), BF16 if lr % 16 == 0 else F32),
                   jax.ShapeDtypeStruct((bsz, SSM_STATE, d_inner), F32)),
        grid=(bsz, nc),
        in_specs=[pl.BlockSpec((lr, conv_ch), lambda b, c: (b * nc + c, xbc_blk)),
                  pl.BlockSpec((lr, d_inner), lambda b, c: (b * nc + c, z_blk)),
                  pl.BlockSpec((lr, LANES), lambda b, c: (b * nc + c, 0)),
                  pl.BlockSpec((1, n_heads, lr), lambda b, c: (b, 0, c)),
                  pl.BlockSpec((1, SSM_STATE, d_inner), lambda b, c: (b, 0, 0)),
                  pl.BlockSpec((1, SUBLANES, conv_ch), lambda b, c: (b, 0, 0)),
                  pl.BlockSpec((CONV_W, conv_ch), const2),
                  pl.BlockSpec((1, conv_ch), const2),
                  pl.BlockSpec((1, LANES), const2),
                  pl.BlockSpec((n_heads, L), const2),
                  pl.BlockSpec((1, LANES), const2),
                  pl.BlockSpec((n_heads, L), const2),
                  pl.BlockSpec((1, d_inner), const2),
                  pl.BlockSpec((1, d_inner), const2)],
        out_specs=(pl.BlockSpec((lr, d_inner), lambda b, c: (b * nc + c, 0)),
                   pl.BlockSpec((1, SSM_STATE, d_inner), lambda b, c: (b, 0, 0))),
        scratch_shapes=[pltpu.VMEM((SUBLANES + L, conv_ch), F32),
                        pltpu.VMEM((SSM_STATE, d_inner), F32),
                        pltpu.VMEM((L, d_inner), F32),
                        pltpu.VMEM((L, LANES), F32),
                        pltpu.VMEM((n_heads, L), F32)],
        compiler_params=_cparams(2),
        name="ssd",
        cost_estimate=pl.CostEstimate(
            flops=2 * bsz * nc * L * (SSM_GROUPS * L * SSM_STATE + d_inner * (L + 2 * SSM_STATE)),
            transcendentals=bsz * nc * L * (n_heads * L + conv_ch + d_inner),
            bytes_accessed=4 * n * (conv_ch + d_inner) + 2 * n * d_inner + 8 * bsz * SSM_STATE * d_inner),
    )(proj, proj, dt, dtT_seq, h0T, cprev8, cw, cb, dtb, dtbT, a, aT, dskip, ng)


def _topk_mask_rows(score, row, k):
    big = score.shape[0]
    sel = jnp.zeros(score.shape, dtype=jnp.bool_)
    g = score
    for _ in range(k):
        m = jnp.max(g, axis=0, keepdims=True)
        idx = jnp.min(jnp.where(g == m, row, big), axis=0, keepdims=True)
        pick = jnp.logical_and(row == idx, m > NEG_INF)
        sel = jnp.logical_or(sel, pick)
        g = jnp.where(pick, NEG_INF, g)
    return sel


def _attn_prompt_kernel(nb, slopes_ref, q_ref, k_ref, v_ref, o_ref,
                        kb_scr, vt_scr, km_scr, sel_scr, bias_scr, qt_scr, m_scr, l_scr, acc_scr, s_scr, al_scr):
    bs = MOBA_BLOCK
    hd = ATTN_HEAD_DIM
    pr = pl.program_id(1)
    qi = pl.program_id(2)
    nbp = km_scr.shape[0]
    n_ch = 4

    @pl.when(qi == 0)
    def _():
        km_scr[...] = jnp.zeros(km_scr.shape, F32)
        for i in range(nb):
            kblk = k_ref[i * bs:(i + 1) * bs, :]
            kb_scr[i] = kblk.astype(BF16)
            vt_scr[i] = v_ref[i * bs:(i + 1) * bs, :].T.astype(BF16)
            km_scr[i:i + 1, :] = jnp.sum(kblk, axis=0, keepdims=True) * (1.0 / bs)

    lane = lax.broadcasted_iota(jnp.int32, (bs, LANES), 1)
    qts = []
    for par in range(2):
        qp = q_ref[:, par * LANES:(par + 1) * LANES]
        qr = pltpu.roll(qp, hd, 1)
        mine = (lane >= hd) if par == 1 else (lane < hd)
        q_g0 = jnp.where(mine, qp if par == 0 else qr, 0.0)
        q_g1 = jnp.where(mine, qr if par == 0 else qp, 0.0)
        qts += [q_g0.T, q_g1.T]

    rowb = lax.broadcasted_iota(jnp.int32, (nbp, bs), 0)
    rk = lax.broadcasted_iota(jnp.int32, (bs, bs), 0)
    cq = lax.broadcasted_iota(jnp.int32, (bs, bs), 1)
    rel = (cq - rk).astype(F32)
    km = km_scr[...]
    slopes = [slopes_ref[n_ch * pr + c] for c in range(n_ch)]
    kd = kb_scr[qi]
    for c in range(n_ch):
        par = c // 2
        gate = _dot_precise(km, qts[c])
        gate = jnp.where(rowb < qi, gate, NEG_INF)
        sel_scr[c] = jnp.where(_topk_mask_rows(gate, rowb, MOBA_TOPK), 0.0, NEG_INF)
        bias = -slopes[c] * rel
        bias_scr[c] = bias
        qtb = (qts[c] * (hd ** -0.5)).astype(BF16)
        qt_scr[c] = qtb
        s = _dot(kd, qtb) + bias
        s = jnp.where(rel >= 0.0, s, NEG_INF)
        m0 = jnp.max(s, axis=0, keepdims=True)
        p = jnp.exp(s - m0)
        m_scr[c] = m0
        l_scr[c] = jnp.sum(p, axis=0, keepdims=True)
        acc_scr[c] = _dot(vt_scr[qi, par * hd:(par + 1) * hd, :], p.astype(BF16))

    def scores(j, slot):
        kj = kb_scr[j]
        dist0 = jnp.asarray((qi - j) * bs, dtype=F32)
        for c in range(n_ch):
            colb = sel_scr[c, pl.ds(j, 1), :] - slopes[c] * dist0
            sj = _dot(kj, qt_scr[c]) + bias_scr[c] + colb
            m_old = m_scr[c]
            m_new = jnp.maximum(m_old, jnp.max(sj, axis=0, keepdims=True))
            al_scr[slot, c] = jnp.exp(m_old - m_new)
            m_scr[c] = m_new
            s_scr[slot, c] = sj - m_new

    def accumulate(j, slot):
        for c in range(n_ch):
            par = c // 2
            pj = jnp.exp(s_scr[slot, c])
            alpha = al_scr[slot, c]
            l_scr[c] = alpha * l_scr[c] + jnp.sum(pj, axis=0, keepdims=True)
            pv = _dot(vt_scr[j, par * hd:(par + 1) * hd, :], pj.astype(BF16))
            acc_scr[c] = alpha * acc_scr[c] + pv

    @pl.when(qi > 0)
    def _():
        scores(0, 0)

        def body(j, carry):
            scores(j + 1, (j + 1) % 2)
            accumulate(j, j % 2)
            return carry

        lax.fori_loop(0, qi - 1, body, 0)
        accumulate(qi - 1, (qi - 1) % 2)

    for par in range(2):
        o_g0 = acc_scr[2 * par] / l_scr[2 * par]
        o_g1 = acc_scr[2 * par + 1] / l_scr[2 * par + 1]
        o_ref[:, par * LANES:(par + 1) * LANES] = jnp.concatenate([o_g0, o_g1], axis=0).T.astype(o_ref.dtype)


def _attn_prompt(proj, slopes, bsz, t, n_kv, q_col, k_col, v_col):
    n = proj.shape[0]
    bs = MOBA_BLOCK
    nb = t // bs
    nbp = -(-nb // 16) * 16
    n_pairs = n_kv // 2
    qw = 2 * LANES
    hd = ATTN_HEAD_DIM
    grid_spec = pltpu.PrefetchScalarGridSpec(
        num_scalar_prefetch=1,
        grid=(bsz, n_pairs, nb),
        in_specs=[pl.BlockSpec((bs, qw), lambda b, h, i, s: (b * nb + i, q_col // qw + h)),
                  pl.BlockSpec((t, LANES), lambda b, h, i, s: (b, k_col // LANES + h)),
                  pl.BlockSpec((t, LANES), lambda b, h, i, s: (b, v_col // LANES + h))],
        out_specs=pl.BlockSpec((bs, qw), lambda b, h, i, s: (b * nb + i, h)),
        scratch_shapes=[pltpu.VMEM((nb, bs, LANES), BF16),
                        pltpu.VMEM((nb, LANES, bs), BF16),
                        pltpu.VMEM((nbp, LANES), F32),
                        pltpu.VMEM((4, nbp, bs), F32),
                        pltpu.VMEM((4, bs, bs), F32),
                        pltpu.VMEM((4, LANES, bs), BF16),
                        pltpu.VMEM((4, 1, bs), F32),
                        pltpu.VMEM((4, 1, bs), F32),
                        pltpu.VMEM((4, hd, bs), F32),
                        pltpu.VMEM((2, 4, bs, bs), F32),
                        pltpu.VMEM((2, 4, 1, bs), F32)])
    return pl.pallas_call(
        functools.partial(_attn_prompt_kernel, nb),
        out_shape=jax.ShapeDtypeStruct((n, n_kv * LANES), BF16),
        grid_spec=grid_spec,
        compiler_params=_cparams(3),
        name="attn_prompt",
        cost_estimate=pl.CostEstimate(flops=2 * bsz * n_kv * 2 * (nb * (nb + 1) // 2) * bs * bs * 2 * LANES,
                                      transcendentals=bsz * n_kv * 2 * (nb * (nb + 1) // 2) * bs * bs,
                                      bytes_accessed=n * (4 * (qw * n_pairs + 2 * LANES * n_pairs) + 2 * qw * n_pairs)),
    )(slopes, proj, proj, proj)


def _diag_extract(o, n_kv, rows_per_kv):
    hd = ATTN_HEAD_DIM
    tiles = []
    for k in range(n_kv):
        tl = o[k * rows_per_kv:(k + 1) * rows_per_kv, (k // 2) * LANES:(k // 2 + 1) * LANES]
        if k % 2 == 1:
            tl = pltpu.roll(tl, hd, 1)
        tiles.append(tl)
    return jnp.concatenate(tiles, axis=0)


def _attn_sample_kernel(bps, n_kv, tq, past_len, pt_ref, *refs):
    npg = 2 * bps
    kp = refs[0:npg]
    vp = refs[npg:2 * npg]
    qbd_ref, slope_ref, trow_ref, knew_ref, vnew_ref, o_ref, st_scr, km_scr, kn_scr, vn_scr, sel_scr = refs[2 * npg:]
    bs = MOBA_BLOCK
    hd = ATTN_HEAD_DIM
    rows = qbd_ref.shape[1]
    kvw = qbd_ref.shape[2]
    page = kp[0].shape[-1]
    rpk = rows // n_kv
    s_id = pl.program_id(1)
    n_steps = pl.num_programs(1)
    nblk = past_len // bs

    @pl.when(s_id == 0)
    def _():
        km_scr[...] = jnp.zeros(km_scr.shape, F32)

    qf = qbd_ref[0]
    qb = (qf * (hd ** -0.5)).astype(BF16)
    slope = slope_ref[...]
    lane = lax.broadcasted_iota(jnp.int32, (rows, LANES), 1)
    lane_k = lax.broadcasted_iota(jnp.int32, (kvw, LANES), 1)
    sc0 = slope * lane.astype(F32)
    sc1 = sc0 + slope * float(page)

    for bi in range(bps):
        j = s_id * bps + bi
        kt0 = kp[2 * bi][0].reshape(kvw, page)
        kt1 = kp[2 * bi + 1][0].reshape(kvw, page)
        vt0 = vp[2 * bi][0].reshape(kvw, page)
        vt1 = vp[2 * bi + 1][0].reshape(kvw, page)
        ksum = jnp.sum(kt0 + kt1, axis=1, keepdims=True) * (1.0 / bs)
        km_scr[...] = jnp.where(lane_k == j, ksum, km_scr[...])
        s0 = _dot(qb, kt0.astype(BF16)) + sc0
        s1 = _dot(qb, kt1.astype(BF16)) + sc1
        m = jnp.maximum(jnp.max(s0, axis=-1, keepdims=True), jnp.max(s1, axis=-1, keepdims=True))
        p0 = jnp.exp(s0 - m)
        p1 = jnp.exp(s1 - m)
        l = jnp.sum(p0, axis=-1, keepdims=True) + jnp.sum(p1, axis=-1, keepdims=True)
        o = _dot_nt(p0.astype(BF16), vt0.astype(BF16)) + _dot_nt(p1.astype(BF16), vt1.astype(BF16))
        od = _diag_extract(o, n_kv, rpk)
        tile = jnp.where(lane < hd, od, jnp.where(lane == hd, m, l))
        st_scr[j] = tile.T[0:hd + SUBLANES]

    @pl.when(s_id == n_steps - 1)
    def _():
        trow = trow_ref[...]
        gate = _dot_precise(qf, km_scr[...])
        gate = jnp.where(lane < nblk, gate, NEG_INF)
        sel, _ = _topk_mask(gate, lane, MOBA_TOPK)
        selneg = jnp.where(sel, 0.0, NEG_INF)

        kn_scr[...] = jnp.zeros(kn_scr.shape, F32)
        vn_scr[...] = jnp.zeros(vn_scr.shape, F32)
        kn_scr[0:tq, :] = knew_ref[...]
        vn_scr[0:tq, :] = vnew_ref[...]
        lanef = lane.astype(F32)
        so = _dot_nt(qb, kn_scr[...].astype(BF16)) - slope * (trow - lanef)
        so = jnp.where(jnp.logical_and(lanef <= trow, lane < tq), so, NEG_INF)
        m_own = jnp.max(so, axis=-1, keepdims=True)
        po = jnp.exp(so - m_own)
        den_own = jnp.sum(po, axis=-1, keepdims=True)
        num_own = _diag_extract(_dot(po.astype(BF16), vn_scr[...].astype(BF16)), n_kv, rpk)
        own_t = jnp.where(lane < hd, num_own, jnp.where(lane == hd, m_own, den_own)).T

        sel_scr[...] = selneg.T
        slope_row = slope.T[0:1]
        base_row = -slope_row * (float(past_len) + trow.T[0:1])

        def body(j, carry):
            m_run, num, den = carry
            st = st_scr[j]
            mj = st[hd:hd + 1] + (base_row + slope_row * jnp.asarray(j * bs, dtype=F32)) + sel_scr[pl.ds(j, 1), :]
            m_new = jnp.maximum(m_run, mj)
            a = jnp.exp(m_run - m_new)
            b = jnp.exp(mj - m_new)
            return m_new, a * num + b * st[0:hd], a * den + b * st[hd + 1:hd + 2]

        init = (own_t[hd:hd + 1], own_t[0:hd], own_t[hd + 1:hd + 2])
        _, num, den = lax.fori_loop(0, nblk, body, init)
        out_t = jnp.concatenate([num / den, jnp.zeros((LANES - hd, rows), F32)], axis=0)
        o_ref[0] = out_t.T


def _attn_sample(cache_kt, cache_vt, layer, page_table, qbd, slope_rows, t_rows, k_new, v_new, tq, n_kv, past_len):
    nbat, rows, kvw = qbd.shape
    page = cache_kt.shape[4]
    hd = cache_kt.shape[3]
    bs = MOBA_BLOCK
    nblk = past_len // bs
    bps = 4
    while nblk % bps:
        bps //= 2
    n_steps = nblk // bps
    assert bs == 2 * page and nblk <= LANES
    npg = 2 * bps

    def page_spec(r):
        return pl.BlockSpec((None, 1, n_kv, hd, page), lambda b, s, pt: (layer, pt[b, s * npg + r], 0, 0, 0))

    in_specs = ([page_spec(r) for r in range(npg)] + [page_spec(r) for r in range(npg)]
                + [pl.BlockSpec((1, rows, kvw), lambda b, s, pt: (b, 0, 0)),
                   pl.BlockSpec((rows, LANES), lambda b, s, pt: (0, 0)),
                   pl.BlockSpec((rows, LANES), lambda b, s, pt: (0, 0)),
                   pl.BlockSpec((tq, kvw), lambda b, s, pt: (b, 0)),
                   pl.BlockSpec((tq, kvw), lambda b, s, pt: (b, 0))])
    grid_spec = pltpu.PrefetchScalarGridSpec(
        num_scalar_prefetch=1,
        grid=(nbat, n_steps),
        in_specs=in_specs,
        out_specs=pl.BlockSpec((1, rows, LANES), lambda b, s, pt: (b, 0, 0)),
        scratch_shapes=[pltpu.VMEM((nblk, hd + SUBLANES, rows), F32),
                        pltpu.VMEM((kvw, LANES), F32),
                        pltpu.VMEM((LANES, kvw), F32),
                        pltpu.VMEM((LANES, kvw), F32),
                        pltpu.VMEM((LANES, rows), F32)])
    assert rows == LANES
    return pl.pallas_call(
        functools.partial(_attn_sample_kernel, bps, n_kv, tq, past_len),
        out_shape=jax.ShapeDtypeStruct((nbat, rows, LANES), F32),
        grid_spec=grid_spec,
        compiler_params=_cparams(2),
        name="attn_sample",
        cost_estimate=pl.CostEstimate(flops=2 * 2 * nbat * rows * past_len * kvw,
                                      transcendentals=nbat * rows * past_len,
                                      bytes_accessed=2 * 4 * nbat * past_len * kvw),
    )(page_table, *([cache_kt] * npg), *([cache_vt] * npg), qbd, slope_rows, t_rows, k_new, v_new)


def _outproj_kernel(alpha, n_exp, ys_ref, ya_ref, gs_ref, ga_ref, x_ref, gm_ref, scf_ref, shf_ref,
                    wos_ref, woa_ref, wout_ref, g1_ref, b1_ref, wr_ref, br_ref,
                    x1_ref, hf_ref, meta_ref, cnt_ref, cnt_scr):
    tm = x_ref.shape[0]
    merged = (jax.nn.sigmoid(gs_ref[...]) * _dot(ys_ref[...].astype(BF16), wos_ref[...])
              + jax.nn.sigmoid(ga_ref[...]) * _dot(ya_ref[...], woa_ref[...]))
    upd = _dot(merged.astype(BF16), wout_ref[...])
    x1 = _layer_norm(alpha * x_ref[...] + gm_ref[0] * upd, g1_ref[...], b1_ref[...])
    x1_ref[...] = x1
    hf = x1 * scf_ref[0] + shf_ref[0]
    hf_ref[...] = _pack_bf16_pairs(hf)
    logits = _dot_precise(hf, wr_ref[...]) + br_ref[...]
    lane = lax.broadcasted_iota(jnp.int32, logits.shape, 1)
    logits = jnp.where(lane < n_exp, logits, NEG_INF)
    g = logits
    picks, vals, ids = [], [], []
    for _ in range(TOP_K):
        m = jnp.max(g, axis=-1, keepdims=True)
        idx = jnp.min(jnp.where(g == m, lane, LANES), axis=-1, keepdims=True)
        pick = lane == idx
        g = jnp.where(pick, NEG_INF, g)
        picks.append(pick)
        vals.append(m)
        ids.append(idx)
    es = [jnp.exp(v - vals[0]) for v in vals]
    inv = 1.0 / (es[0] + es[1] + es[2] + es[3])

    @pl.when(pl.program_id(0) == 0)
    def _():
        cnt_scr[...] = jnp.zeros(cnt_scr.shape, F32)

    onehot = jnp.zeros(logits.shape, F32)
    for pick in picks:
        onehot = onehot + jnp.where(pick, 1.0, 0.0)
    ri = lax.broadcasted_iota(jnp.int32, (tm, tm), 0)
    ci = lax.broadcasted_iota(jnp.int32, (tm, tm), 1)
    ltri = jnp.where(ri > ci, 1.0, 0.0).astype(BF16)
    rank_mat = _dot(ltri, onehot.astype(BF16)) + cnt_scr[0:1, :]
    cnt_new = cnt_scr[0:1, :] + jnp.sum(onehot, axis=0, keepdims=True)
    cnt_scr[...] = jnp.broadcast_to(cnt_new, cnt_scr.shape)
    cnt_ref[...] = jnp.broadcast_to(cnt_new, cnt_ref.shape)
    meta = jnp.zeros(logits.shape, F32)
    for k in range(TOP_K):
        rank_k = jnp.sum(jnp.where(picks[k], rank_mat, 0.0), axis=-1, keepdims=True)
        meta = jnp.where(lane == k, ids[k].astype(F32), meta)
        meta = jnp.where(lane == TOP_K + k, rank_k, meta)
        meta = jnp.where(lane == 2 * TOP_K + k, es[k] * inv, meta)
    meta_ref[...] = meta


def _outproj(alpha, n_exp, ys, ya, proj, x, gm, scf1, shf, wos, woa, wout, g1, b1, wr, br, tm, tiles_per_group,
             gs_blk, ga_blk):
    n, d = x.shape
    r = gm.shape[1]
    grp = lambda i: (i // tiles_per_group, 0, 0)
    const = lambda i: (0, 0)
    return pl.pallas_call(
        functools.partial(_outproj_kernel, alpha, n_exp),
        out_shape=(jax.ShapeDtypeStruct((n, d), F32),
                   jax.ShapeDtypeStruct((n, d // 2), jnp.uint32),
                   jax.ShapeDtypeStruct((n, LANES), F32),
                   jax.ShapeDtypeStruct((SUBLANES, LANES), F32)),
        grid=(n // tm,),
        in_specs=[pl.BlockSpec((tm, ys.shape[1]), lambda i: (i, 0)),
                  pl.BlockSpec((tm, ya.shape[1]), lambda i: (i, 0)),
                  pl.BlockSpec((tm, d), lambda i: (i, gs_blk)),
                  pl.BlockSpec((tm, d), lambda i: (i, ga_blk)),
                  pl.BlockSpec((tm, d), lambda i: (i, 0)),
                  pl.BlockSpec((1, r, d), grp),
                  pl.BlockSpec((1, r, d), grp),
                  pl.BlockSpec((1, r, d), grp),
                  pl.BlockSpec(wos.shape, const),
                  pl.BlockSpec(woa.shape, const),
                  pl.BlockSpec(wout.shape, const),
                  pl.BlockSpec((1, d), const),
                  pl.BlockSpec((1, d), const),
                  pl.BlockSpec((d, LANES), const),
                  pl.BlockSpec((1, LANES), const)],
        out_specs=(pl.BlockSpec((tm, d), lambda i: (i, 0)),
                   pl.BlockSpec((tm, d // 2), lambda i: (i, 0)),
                   pl.BlockSpec((tm, LANES), lambda i: (i, 0)),
                   pl.BlockSpec((SUBLANES, LANES), const)),
        scratch_shapes=[pltpu.VMEM((SUBLANES, LANES), F32)],
        compiler_params=_cparams(1),
        name="outproj",
        cost_estimate=pl.CostEstimate(flops=2 * n * d * (ys.shape[1] + ya.shape[1] + d + 3 * LANES + tm),
                                      transcendentals=2 * n * d,
                                      bytes_accessed=n * (2 * ys.shape[1] + 2 * ya.shape[1] + 4 * 5 * d)),
    )(ys, ya, proj, proj, x, gm, scf1, shf, wos, woa, wout, g1, b1, wr, br)


def _sc_gather_quantum(row_bytes):
    return SC_CORES * SC_SUBCORES * 2 * (SC_GATHER_BYTES // row_bytes)


def _sc_gather(table, idx):
    _, dcol = table.shape
    b = idx.shape[0]
    nw = SC_CORES * SC_SUBCORES
    ch = SC_GATHER_BYTES // (dcol * table.dtype.itemsize)
    assert b % (nw * 2 * ch) == 0
    b_per_w = b // nw
    n_ch = b_per_w // ch
    mesh = plsc.VectorSubcoreMesh(core_axis_name="c", subcore_axis_name="s")

    @functools.partial(
        pl.kernel, mesh=mesh,
        out_type=jax.ShapeDtypeStruct((b, dcol), table.dtype),
        scratch_types=[pltpu.VMEM((ch,), jnp.int32), pltpu.VMEM((ch,), jnp.int32),
                       pltpu.VMEM((ch, dcol), table.dtype), pltpu.VMEM((ch, dcol), table.dtype),
                       pltpu.SemaphoreType.DMA, pltpu.SemaphoreType.DMA],
        name="sc_gather",
        cost_estimate=pl.CostEstimate(flops=0, transcendentals=0,
                                      bytes_accessed=2 * b * dcol * table.dtype.itemsize + 4 * b),
    )
    def gather(table_hbm, idx_hbm, out_hbm, idx0, idx1, rows0, rows1, sem0, sem1):
        idx_v, rows_v, sems = (idx0, idx1), (rows0, rows1), (sem0, sem1)
        wid = lax.axis_index("s") * SC_CORES + lax.axis_index("c")
        base = wid * b_per_w

        def start(c, slot):
            off = pl.multiple_of(base + c * ch, ch)
            pltpu.sync_copy(idx_hbm.at[pl.ds(off, ch)], idx_v[slot])
            pltpu.async_copy(table_hbm.at[idx_v[slot]], rows_v[slot], sems[slot])

        def finish(c, slot):
            pltpu.make_async_copy(table_hbm.at[idx_v[slot]], rows_v[slot], sems[slot]).wait()
            off = pl.multiple_of(base + c * ch, ch)
            pltpu.sync_copy(rows_v[slot], out_hbm.at[pl.ds(off, ch)])

        start(0, 0)

        @pl.loop(0, n_ch, step=2)
        def _(c):
            start(c + 1, 1)
            finish(c, 0)

            @pl.when(c + 2 < n_ch)
            def _():
                start(c + 2, 0)

            finish(c + 1, 1)

    return gather(table, idx)


def _moe_experts_kernel(d_ff, te_ref, x_ref, wgu_ref, bgu_ref, wdn_ref, bdn_ref, o_ref, wgu_scr, wdn_scr):
    i = pl.program_id(0)
    prev = te_ref[jnp.maximum(i - 1, 0)]
    n_used = te_ref[pl.num_programs(0)]

    @pl.when(i < n_used)
    def _():
        @pl.when(jnp.logical_or(i == 0, te_ref[i] != prev))
        def _():
            wgu_scr[...] = wgu_ref[0].astype(BF16)
            wdn_scr[...] = wdn_ref[0].astype(BF16)

        xa, xb = _unpack_bf16_pairs(x_ref[...])
        half = xa.shape[1]
        gu = (_dot(xa, wgu_scr[0:half, :]) + _dot(xb, wgu_scr[half:2 * half, :])) + bgu_ref[0]
        gg = jnp.minimum(gu[:, :d_ff], SWIGLU_LIMIT)
        uu = jnp.clip(gu[:, d_ff:], -SWIGLU_LIMIT, SWIGLU_LIMIT)
        act = (uu + 1.0) * (gg * jax.nn.sigmoid(SWIGLU_ALPHA * gg))
        o_ref[...] = _dot(act.astype(BF16), wdn_scr[...]) + bdn_ref[0]

    @pl.when(i >= n_used)
    def _():
        o_ref[...] = jnp.zeros(o_ref.shape, F32)


def _moe_experts(xs, tile_expert, layer, w_gu, b_gu4, w_down, b_down4, tme):
    p = xs.shape[0]
    d = w_gu.shape[2]
    two_ff = w_gu.shape[3]
    d_ff = two_ff // 2
    grid_spec = pltpu.PrefetchScalarGridSpec(
        num_scalar_prefetch=1,
        grid=(p // tme,),
        in_specs=[pl.BlockSpec((tme, d // 2), lambda i, te: (i, 0)),
                  pl.BlockSpec((None, 1, d, two_ff), lambda i, te: (layer, te[i], 0, 0)),
                  pl.BlockSpec((None, 1, 1, two_ff), lambda i, te: (layer, te[i], 0, 0)),
                  pl.BlockSpec((None, 1, d_ff, d), lambda i, te: (layer, te[i], 0, 0)),
                  pl.BlockSpec((None, 1, 1, d), lambda i, te: (layer, te[i], 0, 0))],
        out_specs=pl.BlockSpec((tme, d), lambda i, te: (i, 0)),
        scratch_shapes=[pltpu.VMEM((d, two_ff), BF16),
                        pltpu.VMEM((d_ff, d), BF16)])
    return pl.pallas_call(
        functools.partial(_moe_experts_kernel, d_ff),
        out_shape=jax.ShapeDtypeStruct((p, d), F32),
        grid_spec=grid_spec,
        compiler_params=_cparams(1),
        name="moe_experts",
        cost_estimate=pl.CostEstimate(flops=2 * p * d * 3 * d_ff, transcendentals=p * d_ff,
                                      bytes_accessed=6 * p * d + 4 * w_gu.shape[1] * 3 * d * d_ff),
    )(tile_expert, xs, w_gu, b_gu4, w_down, b_down4)


def _moe_combine_kernel(alpha, *refs):
    yk_refs = refs[:TOP_K]
    meta_ref, x1_ref, gf_ref, g2_ref, b2_ref, o_ref = refs[TOP_K:]
    meta = meta_ref[...]
    acc = meta[:, 2 * TOP_K:2 * TOP_K + 1] * yk_refs[0][...]
    for k in range(1, TOP_K):
        acc = acc + meta[:, 2 * TOP_K + k:2 * TOP_K + k + 1] * yk_refs[k][...]
    o_ref[...] = _layer_norm(alpha * x1_ref[...] + gf_ref[0] * acc, g2_ref[...], b2_ref[...])


def _moe_combine(alpha, yk, n_pad, meta, x1, gf, g2, b2, tm, tiles_per_group):
    n, d = x1.shape
    r = gf.shape[1]
    bpk = n_pad // tm
    assert n_pad % tm == 0

    def choice_spec(k):
        return pl.BlockSpec((tm, d), lambda i: (k * bpk + i, 0))

    return pl.pallas_call(
        functools.partial(_moe_combine_kernel, alpha),
        out_shape=jax.ShapeDtypeStruct((n, d), F32),
        grid=(n // tm,),
        in_specs=[choice_spec(k) for k in range(TOP_K)] + [
                  pl.BlockSpec((tm, LANES), lambda i: (i, 0)),
                  pl.BlockSpec((tm, d), lambda i: (i, 0)),
                  pl.BlockSpec((1, r, d), lambda i: (i // tiles_per_group, 0, 0)),
                  pl.BlockSpec((1, d), lambda i: (0, 0)),
                  pl.BlockSpec((1, d), lambda i: (0, 0))],
        out_specs=pl.BlockSpec((tm, d), lambda i: (i, 0)),
        compiler_params=_cparams(1),
        name="moe_combine",
        cost_estimate=pl.CostEstimate(flops=2 * TOP_K * n * d, transcendentals=n,
                                      bytes_accessed=4 * n * d * (TOP_K + 2)),
    )(*([yk] * TOP_K), meta, x1, gf, g2, b2)


def _moe_route(meta, counts, tile, tm, d):
    n = meta.shape[0]
    n_exp = counts.shape[0]
    gran_in = _sc_gather_quantum(2 * d)
    gran = _sc_gather_quantum(4 * d)
    unit = gran_in * tile // math.gcd(gran_in, tile)
    eid = meta[:, 0:TOP_K].astype(jnp.int32)
    rank = meta[:, TOP_K:2 * TOP_K].astype(jnp.int32)
    padded = -(-counts // tile) * tile
    ends = jnp.cumsum(padded)
    is_e = eid[:, :, None] == jnp.arange(n_exp, dtype=jnp.int32)
    dest = jnp.sum(jnp.where(is_e, (ends - padded)[None, None, :], 0), axis=-1) + rank
    p_rows = -(-(n * TOP_K + n_exp * tile) // unit) * unit
    tok = jnp.broadcast_to(jnp.arange(n, dtype=jnp.int32)[:, None], (n, TOP_K))
    src_tok = jnp.zeros((p_rows,), jnp.int32).at[dest.reshape(-1)].set(
        tok.reshape(-1), unique_indices=True, mode="promise_in_bounds")
    tile_start = jnp.arange(p_rows // tile, dtype=jnp.int32) * tile
    tile_expert = jnp.minimum(jnp.sum(tile_start[:, None] >= ends[None, :], axis=-1), n_exp - 1).astype(jnp.int32)
    tile_expert = jnp.concatenate([tile_expert, (ends[-1:] // tile).astype(jnp.int32)])
    quant = max(gran // TOP_K, tm)
    n_pad = -(-n // quant) * quant
    dest_km = jnp.pad(dest.T, ((0, 0), (0, n_pad - n))).reshape(-1)
    return dict(src_tok=src_tok, tile_expert=tile_expert, dest_km=dest_km, n_pad=n_pad, tile=tile)


def _moe_group(alpha, layer, route, hf, meta, x1, gf, g2, b2, w_gu, b_gu4, w_down, b_down4, tm, tiles_per_group):
    xs_sorted = _sc_gather(hf, route["src_tok"])
    ys_sorted = _moe_experts(xs_sorted, route["tile_expert"], layer, w_gu, b_gu4, w_down, b_down4, route["tile"])
    yk = _sc_gather(ys_sorted, route["dest_km"])
    return yk, functools.partial(_moe_combine, alpha, n_pad=route["n_pad"], meta=meta, x1=x1, gf=gf, g2=g2, b2=b2,
                                 tm=tm, tiles_per_group=tiles_per_group)


def _pad_lanes(v, fill=0.0):
    return jnp.pad(v, (0, LANES - v.shape[0]), constant_values=fill).reshape(1, LANES)


def kernel(x_prompt, x_sample, c_prompt, c_sample, cache_k, cache_v, state_ssm, state_conv, page_table, w_ada, b_ada, w_in, conv_w, conv_b, dt_bias, a_log, d_skip, ssm_norm_g, w_o_ssm, w_o_attn, w_out, ln1_g, ln1_b, ln2_g, ln2_b, w_router, b_router, w_gu, b_gu, w_down, b_down):
    depth, d, _ = w_ada.shape
    bp, tp, _ = x_prompt.shape
    bsm, ts, _ = x_sample.shape
    n_heads = dt_bias.shape[1]
    d_inner = n_heads * SSM_HEAD_DIM
    conv_ch = conv_w.shape[2]
    kvw = cache_k.shape[3] * cache_k.shape[4]
    n_kv = cache_k.shape[3]
    aw = w_o_attn.shape[1]
    n_q = aw // ATTN_HEAD_DIM
    n_exp = w_router.shape[2]
    page = cache_k.shape[2]
    past_len = page_table.shape[1] * page
    alpha = float((2.0 * depth) ** 0.25)
    assert d_inner == 2 * d and aw == d and kvw == d // 2 and n_q == 2 * n_kv
    assert conv_ch == d_inner + 2 * SSM_GROUPS * SSM_STATE and conv_ch == 3 * d
    assert tp % MOBA_BLOCK == 0 and tp % SSD_CHUNK == 0 and ts == SUBLANES and n_heads <= LANES

    o_z, o_xbc, o_dt = 0, d_inner, d_inner + conv_ch
    o_q = o_dt + n_heads
    o_k, o_v = o_q + aw, o_q + aw + kvw
    o_gs, o_ga = o_v + kvw, o_v + kvw + d
    c_z, c_q, c_gs, c_ga = 0, d_inner, d_inner + d, d_inner + 2 * d
    c_k = d_inner + 3 * d
    c_v = c_k + kvw
    c_xbc = c_v + kvw
    assert c_xbc % conv_ch == 0

    slopes = 2.0 ** (-8.0 * (jnp.arange(n_q, dtype=F32) + 1.0) / n_q)

    n_c = bsm + bp
    n_c_pad = -(-n_c // SUBLANES) * SUBLANES
    c_all = jnp.concatenate([c_sample, c_prompt, jnp.zeros((n_c_pad - n_c, d), F32)], axis=0)
    mod = _ada(c_all, w_ada, b_ada)

    grp = n_q // n_kv
    rows = n_kv * grp * ts
    r_idx = jnp.arange(rows)
    r_head = (r_idx // (grp * ts)) * grp + (r_idx // ts) % grp
    slope_rows = jnp.broadcast_to(slopes[r_head][:, None], (rows, LANES))
    t_rows = jnp.broadcast_to((r_idx % ts).astype(F32)[:, None], (rows, LANES))
    eye_kv = jnp.eye(n_kv, dtype=F32)

    cache_kt = jnp.transpose(cache_k, (0, 1, 3, 4, 2))
    cache_vt = jnp.transpose(cache_v, (0, 1, 3, 4, 2))

    xp = x_prompt.reshape(bp * tp, d)
    xs = x_sample.reshape(bsm * ts, d)
    tm_p = 1024 if tp % 1024 == 0 else MOBA_BLOCK
    tm_o = 512 if tp % 512 == 0 else MOBA_BLOCK
    n_s = bsm * ts

    def make_layer(l):
        wl = w_in[l]
        w_main = jnp.concatenate([wl[:, o_z:o_z + d_inner], wl[:, o_q:o_q + aw], wl[:, o_gs:o_gs + d],
                                  wl[:, o_ga:o_ga + d], wl[:, o_k:o_k + kvw], wl[:, o_v:o_v + kvw],
                                  wl[:, o_xbc:o_xbc + conv_ch]], axis=1).astype(BF16)
        w_dt = jnp.pad(wl[:, o_dt:o_dt + n_heads], ((0, 0), (0, LANES - n_heads))).astype(BF16)
        w_dtT = wl[:, o_dt:o_dt + n_heads].T.astype(BF16)
        dtb = _pad_lanes(dt_bias[l])
        dtbT = jnp.broadcast_to(dt_bias[l][:, None], (n_heads, SSD_CHUNK))
        a_neg = -jnp.exp(a_log[l])
        a_row = _pad_lanes(a_neg)
        aT = jnp.broadcast_to(a_neg[:, None], (n_heads, SSD_CHUNK))
        dskip = jnp.repeat(d_skip[l], SSM_HEAD_DIM).reshape(1, d_inner)
        ng = ssm_norm_g[l].reshape(1, d_inner)
        cw = conv_w[l]
        cb = conv_b[l].reshape(1, conv_ch)
        wos = w_o_ssm[l].astype(BF16)
        woa = w_o_attn[l].astype(BF16)
        wout = w_out[l].astype(BF16)
        g1, b1 = ln1_g[l].reshape(1, d), ln1_b[l].reshape(1, d)
        g2, b2 = ln2_g[l].reshape(1, d), ln2_b[l].reshape(1, d)
        wr = jnp.pad(w_router[l], ((0, 0), (0, LANES - n_exp)))
        br = _pad_lanes(b_router[l])

        mod_l = mod[l]
        sh_m, sc_m, g_m, sh_f, sc_f, g_f = [mod_l[:, i * d:(i + 1) * d] for i in range(6)]

        def group_mod(v, is_prompt):
            if is_prompt:
                return v[bsm:bsm + bp].reshape(bp, 1, d)
            return jnp.repeat(v[:bsm], ts, axis=0).reshape(1, n_s, d)

        def run(x, is_prompt, h0T, cprev8):
            bsz, t = (bp, tp) if is_prompt else (bsm, ts)
            n = bsz * t
            tm = tm_p if is_prompt else n
            tmo = tm_o if is_prompt else n
            tpg = (t // tm) if is_prompt else 1
            tpgo = (t // tmo) if is_prompt else 1
            gm = lambda v: group_mod(v, is_prompt)
            proj, dt, dtT, k_new, v_new = _inproj(x, gm(1.0 + sc_m), gm(sh_m), w_main, w_dt, w_dtT, tm, tpg,
                                                  c_k, kvw)
            dtT_seq = dtT.reshape(n_heads, bsz, t).transpose(1, 0, 2)
            lr = SSD_CHUNK if is_prompt else t
            y_s, h_newT = _ssd(proj, dt, dtT_seq, h0T, cprev8, cw, cb, dtb, dtbT, a_row, aT, dskip, ng,
                               bsz, t, lr, c_z // d_inner, c_xbc // conv_ch, d_inner, conv_ch)
            if is_prompt:
                y_a = _attn_prompt(proj, slopes, bsz, t, n_kv, c_q, c_k, c_v)
            else:
                q5 = proj[:, c_q:c_q + aw].reshape(bsz, t, n_kv, grp, ATTN_HEAD_DIM)
                q5 = q5.transpose(0, 2, 3, 1, 4).reshape(bsz, n_kv, grp * t, ATTN_HEAD_DIM)
                qbd = (q5[:, :, :, None, :] * eye_kv[None, :, None, :, None]).reshape(bsz, rows, kvw)
                o = _attn_sample(cache_kt, cache_vt, l, page_table, qbd, slope_rows, t_rows, k_new, v_new,
                                 t, n_kv, past_len)
                o = o[:, :, :ATTN_HEAD_DIM].reshape(bsz, n_kv, grp, t, ATTN_HEAD_DIM)
                y_a = o.transpose(0, 3, 1, 2, 4).reshape(n, aw).astype(BF16)
            x1, hf, meta, cnt = _outproj(alpha, n_exp, y_s, y_a, proj, x, gm(g_m), gm(1.0 + sc_f), gm(sh_f),
                                         wos, woa, wout, g1, b1, wr, br, tmo, tpgo, c_gs // d, c_ga // d)
            conv_new = proj.reshape(bsz, t, -1)[:, t - (CONV_W - 1):, c_xbc:c_xbc + conv_ch]
            h_new = h_newT.transpose(0, 2, 1).reshape(bsz, n_heads, SSM_HEAD_DIM, SSM_STATE)
            hd4 = (bsz, t, n_kv, ATTN_HEAD_DIM)
            return dict(x1=x1, hf=hf, meta=meta, cnt=cnt[0, :n_exp].astype(jnp.int32), gf=gm(g_f),
                        k=k_new.reshape(hd4), v=v_new.reshape(hd4), h=h_new, conv=conv_new)

        return run, g2, b2

    def sample_front(l, x):
        h0_s = state_ssm[l].reshape(bsm, d_inner, SSM_STATE).transpose(0, 2, 1)
        cv0_s = jnp.pad(state_conv[l], ((0, 0), (SUBLANES - (CONV_W - 1), 0), (0, 0)))
        return layers[l][0](x, False, h0_s, cv0_s)

    layers = [make_layer(l) for l in range(depth)]
    b_gu4 = b_gu.reshape(depth, n_exp, 1, -1)
    b_down4 = b_down.reshape(depth, n_exp, 1, d)
    h0_p = jnp.zeros((bp, SSM_STATE, d_inner), F32)
    cv0_p = jnp.zeros((bp, SUBLANES, conv_ch), F32)
    outs = {k: [] for k in ("kp", "vp", "hp", "cvp", "ks", "vs", "hs", "cvs")}
    fs = None
    for l in range(depth):
        run, g2, b2 = layers[l]
        fp = run(xp, True, h0_p, cv0_p)
        route_p = _moe_route(fp["meta"], fp["cnt"], MOE_TILE, MOBA_BLOCK, d)
        fs = sample_front(l, xs)
        outs["kp"].append(fp["k"]); outs["vp"].append(fp["v"]); outs["hp"].append(fp["h"]); outs["cvp"].append(fp["conv"])
        outs["ks"].append(fs["k"]); outs["vs"].append(fs["v"]); outs["hs"].append(fs["h"]); outs["cvs"].append(fs["conv"])

        yk_p, combine_p = _moe_group(alpha, l, route_p, fp["hf"], fp["meta"], fp["x1"], fp["gf"], g2, b2,
                                     w_gu, b_gu4, w_down, b_down4, MOBA_BLOCK, tp // MOBA_BLOCK)
        route_s = _moe_route(fs["meta"], fs["cnt"], MOE_TILE_SMALL, n_s, d)
        yk_s, combine_s = _moe_group(alpha, l, route_s, fs["hf"], fs["meta"], fs["x1"], fs["gf"], g2, b2,
                                     w_gu, b_gu4, w_down, b_down4, n_s, 1)
        xs = combine_s(yk_s)
        xp = combine_p(yk_p)

    st = lambda k: jnp.stack(outs[k])
    return (xp.reshape(bp, tp, d), xs.reshape(bsm, ts, d), st("kp"), st("vp"), st("hp"), st("cvp"),
            st("ks"), st("vs"), st("hs"), st("cvs"))
```

```python
import functools
import math

import jax
import jax.numpy as jnp
from jax import lax
from jax.experimental import pallas as pl
from jax.experimental.pallas import tpu as pltpu
from jax.experimental.pallas import tpu_sc as plsc

F32 = jnp.float32
BF16 = jnp.bfloat16
NEG_INF = float("-inf")

LANES = 128
SUBLANES = 8
VMEM_LIMIT = 56 * 1024 * 1024
SC_CORES = 2
SC_SUBCORES = 16
SC_GATHER_BYTES = 128 * 1024
MOE_TILE = 512
MOE_TILE_SMALL = 64

SSM_HEAD_DIM = 64
SSM_GROUPS = 4
SSM_STATE = 128
CONV_W = 4
SSD_CHUNK = 128
RMS_EPS = 1e-5
ATTN_HEAD_DIM = 64
MOBA_BLOCK = 256
MOBA_TOPK = 3
TOP_K = 4
SWIGLU_LIMIT = 7.0
SWIGLU_ALPHA = 1.702
LN_EPS = 1e-5


def _cparams(n_axes):
    return pltpu.CompilerParams(dimension_semantics=("arbitrary",) * n_axes,
                                vmem_limit_bytes=VMEM_LIMIT)


def _dot(a, b):
    return jnp.dot(a, b, preferred_element_type=F32)


def _dot_nt(a, b):
    return lax.dot_general(a, b, (((1,), (1,)), ((), ())), preferred_element_type=F32)


def _split2(a):
    hi = a.astype(BF16)
    lo = (a - hi.astype(F32)).astype(BF16)
    return hi, lo


def _split3(a):
    hi = a.astype(BF16)
    r = a - hi.astype(F32)
    mid = r.astype(BF16)
    lo = (r - mid.astype(F32)).astype(BF16)
    return hi, mid, lo


def _dot_nt_precise(a, b):
    ah, al = _split2(a)
    bh, bl = _split2(b)
    return _dot_nt(ah, bh) + (_dot_nt(ah, bl) + _dot_nt(al, bh))


def _dot_precise(a, b):
    ah, al = _split2(a)
    bh, bl = _split2(b)
    return _dot(ah, bh) + (_dot(ah, bl) + _dot(al, bh))


def _silu(x):
    return x * jax.nn.sigmoid(x)


def _softplus(x):
    return jnp.maximum(x, 0.0) + jnp.log1p(jnp.exp(-jnp.abs(x)))


def _layer_norm(x, g, b):
    mu = jnp.mean(x, axis=-1, keepdims=True)
    xc = x - mu
    var = jnp.mean(xc * xc, axis=-1, keepdims=True)
    return xc * lax.rsqrt(var + LN_EPS) * g + b


def _pack_bf16_pairs(x):
    w = x.shape[1] // 2
    bits = lax.bitcast_convert_type(x, jnp.uint32)
    top = (bits + (jnp.uint32(0x7FFF) + ((bits >> 16) & jnp.uint32(1)))) >> 16
    return top[:, :w] | (top[:, w:] << 16)


def _unpack_bf16_pairs(wd):
    lo = lax.bitcast_convert_type(wd << 16, F32)
    hi = lax.bitcast_convert_type(wd & jnp.uint32(0xFFFF0000), F32)
    return lo.astype(BF16), hi.astype(BF16)


def _topk_mask(score, lane, k):
    width = score.shape[-1]
    sel = jnp.zeros(score.shape, dtype=jnp.bool_)
    vals = []
    g = score
    for _ in range(k):
        m = jnp.max(g, axis=-1, keepdims=True)
        idx = jnp.min(jnp.where(g == m, lane, width), axis=-1, keepdims=True)
        pick = jnp.logical_and(lane == idx, m > NEG_INF)
        sel = jnp.logical_or(sel, pick)
        g = jnp.where(pick, NEG_INF, g)
        vals.append(m)
    return sel, vals


def _ada_kernel(c_ref, w_ref, b_ref, o_ref):
    s = _silu(c_ref[...]).astype(BF16)
    o_ref[0] = _dot(s, w_ref[0].astype(BF16)) + b_ref[0]


def _ada(c_all, w_ada, b_ada):
    depth, d, n6 = w_ada.shape
    rows = c_all.shape[0]
    tn = 1024
    return pl.pallas_call(
        _ada_kernel,
        out_shape=jax.ShapeDtypeStruct((depth, rows, n6), F32),
        grid=(depth, n6 // tn),
        in_specs=[pl.BlockSpec((rows, d), lambda l, j: (0, 0)),
                  pl.BlockSpec((1, d, tn), lambda l, j: (l, 0, j)),
                  pl.BlockSpec((1, 1, tn), lambda l, j: (l, 0, j))],
        out_specs=pl.BlockSpec((1, rows, tn), lambda l, j: (l, 0, j)),
        compiler_params=_cparams(2),
        name="ada",
    )(c_all, w_ada, b_ada.reshape(depth, 1, n6))


def _inproj_kernel(j_kv, kvw, x_ref, sc_ref, sh_ref, w_ref, wdt_ref, wdtT_ref, o_ref, dt_ref, dtT_ref, k_ref, v_ref,
                   h_scr):
    @pl.when(pl.program_id(1) == 0)
    def _():
        hb = (x_ref[...] * sc_ref[0] + sh_ref[0]).astype(BF16)
        h_scr[...] = hb
        dt_ref[...] = _dot(hb, wdt_ref[...])
        dtT_ref[...] = _dot_nt(wdtT_ref[...], hb)

    res = _dot(h_scr[...], w_ref[...])
    o_ref[...] = res

    @pl.when(pl.program_id(1) == j_kv)
    def _():
        k_ref[...] = res[:, 0:kvw]
        v_ref[...] = res[:, kvw:2 * kvw]


def _inproj(x, sc1, sh, w_main, w_dt, w_dtT, tm, tiles_per_group, k_col, kvw):
    n, d = x.shape
    n_main = w_main.shape[1]
    tn = 1024
    r = sc1.shape[1]
    nh = w_dtT.shape[0]
    assert k_col % tn == 0 and 2 * kvw <= tn
    grp = lambda i, j: (i // tiles_per_group, 0, 0)
    return pl.pallas_call(
        functools.partial(_inproj_kernel, k_col // tn, kvw),
        out_shape=(jax.ShapeDtypeStruct((n, n_main), F32),
                   jax.ShapeDtypeStruct((n, LANES), F32),
                   jax.ShapeDtypeStruct((nh, n), F32),
                   jax.ShapeDtypeStruct((n, kvw), F32),
                   jax.ShapeDtypeStruct((n, kvw), F32)),
        grid=(n // tm, n_main // tn),
        in_specs=[pl.BlockSpec((tm, d), lambda i, j: (i, 0)),
                  pl.BlockSpec((1, r, d), grp),
                  pl.BlockSpec((1, r, d), grp),
                  pl.BlockSpec((d, tn), lambda i, j: (0, j)),
                  pl.BlockSpec((d, LANES), lambda i, j: (0, 0)),
                  pl.BlockSpec((nh, d), lambda i, j: (0, 0))],
        out_specs=(pl.BlockSpec((tm, tn), lambda i, j: (i, j)),
                   pl.BlockSpec((tm, LANES), lambda i, j: (i, 0)),
                   pl.BlockSpec((nh, tm), lambda i, j: (0, i)),
                   pl.BlockSpec((tm, kvw), lambda i, j: (i, 0)),
                   pl.BlockSpec((tm, kvw), lambda i, j: (i, 0))),
        scratch_shapes=[pltpu.VMEM((tm, d), BF16)],
        compiler_params=_cparams(2),
        name="inproj",
        cost_estimate=pl.CostEstimate(flops=2 * n * d * (n_main + LANES + nh), transcendentals=0,
                                      bytes_accessed=4 * n * (d + n_main) + 2 * d * n_main * (n // tm)),
    )(x, sc1, sh, w_main, w_dt, w_dtT)


def _ssd_kernel(lr, d_inner, xbc_ref, z_ref, dt_ref, dtT_ref, h0_ref, cprev_ref, cw_ref, cb_ref,
                dtb_ref, dtbT_ref, a_ref, aT_ref, dskip_ref, ng_ref,
                y_ref, hout_ref, xp_scr, h_scr, y_scr, dt_scr, dtT_scr):
    L = SSD_CHUNK
    gs = SSM_GROUPS * SSM_STATE
    n_heads = d_inner // SSM_HEAD_DIM
    pairs_per_group = n_heads // SSM_GROUPS // 2
    c = pl.program_id(1)

    @pl.when(c == 0)
    def _():
        xp_scr[0:SUBLANES, :] = cprev_ref[0]
        h_scr[...] = h0_ref[0]

    if lr < L:
        xp_scr[SUBLANES + lr:SUBLANES + L, :] = jnp.zeros((L - lr, xp_scr.shape[1]), F32)
    xp_scr[SUBLANES:SUBLANES + lr, :] = xbc_ref[...]

    x0 = xp_scr[SUBLANES:SUBLANES + L, :]
    tail8 = xp_scr[0:SUBLANES, :]
    row8 = lax.broadcasted_iota(jnp.int32, (SUBLANES, xp_scr.shape[1]), 0)
    acc = cb_ref[...]
    for j in range(CONV_W):
        k = CONV_W - 1 - j
        if k == 0:
            tap = x0
        else:
            rolled = pltpu.roll(x0, k, 0)
            head = jnp.where(row8 < k, pltpu.roll(tail8, k, 0), rolled[0:SUBLANES])
            tap = jnp.concatenate([head, rolled[SUBLANES:]], axis=0)
        acc = acc + cw_ref[j:j + 1, :] * tap
    tail = xp_scr[lr:lr + SUBLANES, :]
    xp_scr[0:SUBLANES, :] = tail
    xc = _silu(acc)

    if lr < L:
        dt_scr[...] = jnp.zeros(dt_scr.shape, F32)
        dt_scr[0:lr, :] = dt_ref[...]
        dtT_scr[...] = jnp.zeros(dtT_scr.shape, F32)
        dtT_scr[:, 0:lr] = dtT_ref[0]
        dt_raw = dt_scr[...]
        dtT_raw = dtT_scr[...]
    else:
        dt_raw = dt_ref[...]
        dtT_raw = dtT_ref[0]
    row = lax.broadcasted_iota(jnp.int32, (L, LANES), 0)
    col = lax.broadcasted_iota(jnp.int32, (n_heads, L), 1)
    dtv = jnp.where(row < lr, _softplus(dt_raw + dtb_ref[...]), 0.0)
    dtvT = jnp.where(col < lr, _softplus(dtT_raw + dtbT_ref[...]), 0.0)
    dA = dtv * a_ref[...]
    dAT = dtvT * aT_ref[...]

    ri = lax.broadcasted_iota(jnp.int32, (L, L), 0)
    ci = lax.broadcasted_iota(jnp.int32, (L, L), 1)
    causal = ri >= ci
    tri = jnp.where(causal, 1.0, 0.0).astype(BF16)
    triT = jnp.where(ri <= ci, 1.0, 0.0).astype(BF16)
    d1, d2, d3 = _split3(dA)
    la = _dot(tri, d1) + (_dot(tri, d2) + _dot(tri, d3))
    e1, e2, e3 = _split3(dAT)
    laT = _dot(e1, triT) + (_dot(e2, triT) + _dot(e3, triT))

    la_last = la[L - 1:L, :]
    to_end = jnp.exp(la_last - la)
    ela = jnp.exp(la)
    cdec = jnp.exp(la_last)

    lane = lax.broadcasted_iota(jnp.int32, (L, LANES), 1)
    lo_half = lane < SSM_HEAD_DIM
    lane1 = lax.broadcasted_iota(jnp.int32, (1, LANES), 1)
    eye = jnp.where(ri == ci, 1.0, 0.0).astype(BF16)

    def pair_bcast(arr, p):
        rows = arr.shape[0]
        a0 = jnp.broadcast_to(arr[:, 2 * p:2 * p + 1], (rows, LANES))
        a1 = jnp.broadcast_to(arr[:, 2 * p + 1:2 * p + 2], (rows, LANES))
        return jnp.where(lo_half[:rows], a0, a1)

    for g in range(SSM_GROUPS):
        bg = xc[:, d_inner + g * SSM_STATE:d_inner + (g + 1) * SSM_STATE].astype(BF16)
        cg = xc[:, d_inner + gs + g * SSM_STATE:d_inner + gs + (g + 1) * SSM_STATE].astype(BF16)
        cb = _dot_nt(cg, bg)
        bgT = _dot_nt(eye, bg).astype(BF16)
        for pp in range(pairs_per_group):
            p = g * pairs_per_group + pp
            sl = slice(p * LANES, (p + 1) * LANES)
            xs_p = xc[:, sl]
            xdt_p = xs_p * pair_bcast(dtv, p)
            ydiag = None
            for a in range(2):
                h = 2 * p + a
                seg = jnp.broadcast_to(la[:, h:h + 1], (L, L)) - laT[h:h + 1, :]
                decay = jnp.exp(jnp.where(causal, seg, NEG_INF))
                m = (cb * decay).astype(BF16)
                half = lo_half if a == 0 else jnp.logical_not(lo_half)
                part = _dot(m, jnp.where(half, xdt_p, 0.0).astype(BF16))
                ydiag = part if ydiag is None else ydiag + part
            hin = h_scr[:, sl]
            yoff = _dot(cg, hin.astype(BF16)) * pair_bcast(ela, p)
            y_scr[:, sl] = ydiag + yoff + dskip_ref[:, sl] * xs_p
            s_new = _dot(bgT, (xdt_p * pair_bcast(to_end, p)).astype(BF16))
            cd_p = jnp.where(lane1 < SSM_HEAD_DIM,
                             jnp.broadcast_to(cdec[:, 2 * p:2 * p + 1], (1, LANES)),
                             jnp.broadcast_to(cdec[:, 2 * p + 1:2 * p + 2], (1, LANES)))
            h_scr[:, sl] = hin * cd_p + s_new

    gw = d_inner // SSM_GROUPS
    for g in range(SSM_GROUPS):
        sl = slice(g * gw, (g + 1) * gw)
        yg = y_scr[0:lr, sl] * _silu(z_ref[:, sl])
        ms = jnp.mean(yg * yg, axis=-1, keepdims=True)
        y_ref[:, sl] = (yg * lax.rsqrt(ms + RMS_EPS) * ng_ref[:, sl]).astype(y_ref.dtype)

    @pl.when(c == pl.num_programs(1) - 1)
    def _():
        hout_ref[0] = h_scr[...]


def _ssd(proj, dt, dtT_seq, h0T, cprev8, cw, cb, dtb, dtbT, a, aT, dskip, ng, bsz, t, lr,
         z_blk, xbc_blk, d_inner, conv_ch):
    n = proj.shape[0]
    nc = t // lr
    n_heads = d_inner // SSM_HEAD_DIM
    L = SSD_CHUNK
    const2 = lambda b, c: (0, 0)
    return pl.pallas_call(
        functools.partial(_ssd_kernel, lr, d_inner),
        out_shape=(jax.ShapeDtypeStruct((n, d_inner), BF16 if lr % 16 == 0 else F32),
                   jax.ShapeDtypeStruct((bsz, SSM_STATE, d_inner), F32)),
        grid=(bsz, nc),
        in_specs=[pl.BlockSpec((lr, conv_ch), lambda b, c: (b * nc + c, xbc_blk)),
                  pl.BlockSpec((lr, d_inner), lambda b, c: (b * nc + c, z_blk)),
                  pl.BlockSpec((lr, LANES), lambda b, c: (b * nc + c, 0)),
                  pl.BlockSpec((1, n_heads, lr), lambda b, c: (b, 0, c)),
                  pl.BlockSpec((1, SSM_STATE, d_inner), lambda b, c: (b, 0, 0)),
                  pl.BlockSpec((1, SUBLANES, conv_ch), lambda b, c: (b, 0, 0)),
                  pl.BlockSpec((CONV_W, conv_ch), const2),
                  pl.BlockSpec((1, conv_ch), const2),
                  pl.BlockSpec((1, LANES), const2),
                  pl.BlockSpec((n_heads, L), const2),
                  pl.BlockSpec((1, LANES), const2),
                  pl.BlockSpec((n_heads, L), const2),
                  pl.BlockSpec((1, d_inner), const2),
                  pl.BlockSpec((1, d_inner), const2)],
        out_specs=(pl.BlockSpec((lr, d_inner), lambda b, c: (b * nc + c, 0)),
                   pl.BlockSpec((1, SSM_STATE, d_inner), lambda b, c: (b, 0, 0))),
        scratch_shapes=[pltpu.VMEM((SUBLANES + L, conv_ch), F32),
                        pltpu.VMEM((SSM_STATE, d_inner), F32),
                        pltpu.VMEM((L, d_inner), F32),
                        pltpu.VMEM((L, LANES), F32),
                        pltpu.VMEM((n_heads, L), F32)],
        compiler_params=_cparams(2),
        name="ssd",
        cost_estimate=pl.CostEstimate(
            flops=2 * bsz * nc * L * (SSM_GROUPS * L * SSM_STATE + d_inner * (L + 2 * SSM_STATE)),
            transcendentals=bsz * nc * L * (n_heads * L + conv_ch + d_inner),
            bytes_accessed=4 * n * (conv_ch + d_inner) + 2 * n * d_inner + 8 * bsz * SSM_STATE * d_inner),
    )(proj, proj, dt, dtT_seq, h0T, cprev8, cw, cb, dtb, dtbT, a, aT, dskip, ng)


def _topk_mask_rows(score, row, k):
    big = score.shape[0]
    sel = jnp.zeros(score.shape, dtype=jnp.bool_)
    g = score
    for _ in range(k):
        m = jnp.max(g, axis=0, keepdims=True)
        idx = jnp.min(jnp.where(g == m, row, big), axis=0, keepdims=True)
        pick = jnp.logical_and(row == idx, m > NEG_INF)
        sel = jnp.logical_or(sel, pick)
        g = jnp.where(pick, NEG_INF, g)
    return sel


def _attn_prompt_kernel(nb, slopes_ref, q_ref, k_ref, v_ref, o_ref,
                        kb_scr, vt_scr, km_scr, sel_scr, bias_scr, qt_scr, m_scr, l_scr, acc_scr, s_scr, al_scr):
    bs = MOBA_BLOCK
    hd = ATTN_HEAD_DIM
    pr = pl.program_id(1)
    qi = pl.program_id(2)
    nbp = km_scr.shape[0]
    n_ch = 4

    @pl.when(qi == 0)
    def _():
        km_scr[...] = jnp.zeros(km_scr.shape, F32)
        for i in range(nb):
            kblk = k_ref[i * bs:(i + 1) * bs, :]
            kb_scr[i] = kblk.astype(BF16)
            vt_scr[i] = v_ref[i * bs:(i + 1) * bs, :].T.astype(BF16)
            km_scr[i:i + 1, :] = jnp.sum(kblk, axis=0, keepdims=True) * (1.0 / bs)

    lane = lax.broadcasted_iota(jnp.int32, (bs, LANES), 1)
    qts = []
    for par in range(2):
        qp = q_ref[:, par * LANES:(par + 1) * LANES]
        qr = pltpu.roll(qp, hd, 1)
        mine = (lane >= hd) if par == 1 else (lane < hd)
        q_g0 = jnp.where(mine, qp if par == 0 else qr, 0.0)
        q_g1 = jnp.where(mine, qr if par == 0 else qp, 0.0)
        qts += [q_g0.T, q_g1.T]

    rowb = lax.broadcasted_iota(jnp.int32, (nbp, bs), 0)
    rk = lax.broadcasted_iota(jnp.int32, (bs, bs), 0)
    cq = lax.broadcasted_iota(jnp.int32, (bs, bs), 1)
    rel = (cq - rk).astype(F32)
    km = km_scr[...]
    slopes = [slopes_ref[n_ch * pr + c] for c in range(n_ch)]
    kd = kb_scr[qi]
    for c in range(n_ch):
        par = c // 2
        gate = _dot_precise(km, qts[c])
        gate = jnp.where(rowb < qi, gate, NEG_INF)
        sel_scr[c] = jnp.where(_topk_mask_rows(gate, rowb, MOBA_TOPK), 0.0, NEG_INF)
        bias = -slopes[c] * rel
        bias_scr[c] = bias
        qtb = (qts[c] * (hd ** -0.5)).astype(BF16)
        qt_scr[c] = qtb
        s = _dot(kd, qtb) + bias
        s = jnp.where(rel >= 0.0, s, NEG_INF)
        m0 = jnp.max(s, axis=0, keepdims=True)
        p = jnp.exp(s - m0)
        m_scr[c] = m0
        l_scr[c] = jnp.sum(p, axis=0, keepdims=True)
        acc_scr[c] = _dot(vt_scr[qi, par * hd:(par + 1) * hd, :], p.astype(BF16))

    def scores(j, slot):
        kj = kb_scr[j]
        dist0 = jnp.asarray((qi - j) * bs, dtype=F32)
        for c in range(n_ch):
            colb = sel_scr[c, pl.ds(j, 1), :] - slopes[c] * dist0
            sj = _dot(kj, qt_scr[c]) + bias_scr[c] + colb
            m_old = m_scr[c]
            m_new = jnp.maximum(m_old, jnp.max(sj, axis=0, keepdims=True))
            al_scr[slot, c] = jnp.exp(m_old - m_new)
            m_scr[c] = m_new
            s_scr[slot, c] = sj - m_new

    def accumulate(j, slot):
        for c in range(n_ch):
            par = c // 2
            pj = jnp.exp(s_scr[slot, c])
            alpha = al_scr[slot, c]
            l_scr[c] = alpha * l_scr[c] + jnp.sum(pj, axis=0, keepdims=True)
            pv = _dot(vt_scr[j, par * hd:(par + 1) * hd, :], pj.astype(BF16))
            acc_scr[c] = alpha * acc_scr[c] + pv

    @pl.when(qi > 0)
    def _():
        scores(0, 0)

        def body(j, carry):
            scores(j + 1, (j + 1) % 2)
            accumulate(j, j % 2)
            return carry

        lax.fori_loop(0, qi - 1, body, 0)
        accumulate(qi - 1, (qi - 1) % 2)

    for par in range(2):
        o_g0 = acc_scr[2 * par] / l_scr[2 * par]
        o_g1 = acc_scr[2 * par + 1] / l_scr[2 * par + 1]
        o_ref[:, par * LANES:(par + 1) * LANES] = jnp.concatenate([o_g0, o_g1], axis=0).T.astype(o_ref.dtype)


def _attn_prompt(proj, slopes, bsz, t, n_kv, q_col, k_col, v_col):
    n = proj.shape[0]
    bs = MOBA_BLOCK
    nb = t // bs
    nbp = -(-nb // 16) * 16
    n_pairs = n_kv // 2
    qw = 2 * LANES
    hd = ATTN_HEAD_DIM
    grid_spec = pltpu.PrefetchScalarGridSpec(
        num_scalar_prefetch=1,
        grid=(bsz, n_pairs, nb),
        in_specs=[pl.BlockSpec((bs, qw), lambda b, h, i, s: (b * nb + i, q_col // qw + h)),
                  pl.BlockSpec((t, LANES), lambda b, h, i, s: (b, k_col // LANES + h)),
                  pl.BlockSpec((t, LANES), lambda b, h, i, s: (b, v_col // LANES + h))],
        out_specs=pl.BlockSpec((bs, qw), lambda b, h, i, s: (b * nb + i, h)),
        scratch_shapes=[pltpu.VMEM((nb, bs, LANES), BF16),
                        pltpu.VMEM((nb, LANES, bs), BF16),
                        pltpu.VMEM((nbp, LANES), F32),
                        pltpu.VMEM((4, nbp, bs), F32),
                        pltpu.VMEM((4, bs, bs), F32),
                        pltpu.VMEM((4, LANES, bs), BF16),
                        pltpu.VMEM((4, 1, bs), F32),
                        pltpu.VMEM((4, 1, bs), F32),
                        pltpu.VMEM((4, hd, bs), F32),
                        pltpu.VMEM((2, 4, bs, bs), F32),
                        pltpu.VMEM((2, 4, 1, bs), F32)])
    return pl.pallas_call(
        functools.partial(_attn_prompt_kernel, nb),
        out_shape=jax.ShapeDtypeStruct((n, n_kv * LANES), BF16),
        grid_spec=grid_spec,
        compiler_params=_cparams(3),
        name="attn_prompt",
        cost_estimate=pl.CostEstimate(flops=2 * bsz * n_kv * 2 * (nb * (nb + 1) // 2) * bs * bs * 2 * LANES,
                                      transcendentals=bsz * n_kv * 2 * (nb * (nb + 1) // 2) * bs * bs,
                                      bytes_accessed=n * (4 * (qw * n_pairs + 2 * LANES * n_pairs) + 2 * qw * n_pairs)),
    )(slopes, proj, proj, proj)


def _diag_extract(o, n_kv, rows_per_kv):
    hd = ATTN_HEAD_DIM
    tiles = []
    for k in range(n_kv):
        tl = o[k * rows_per_kv:(k + 1) * rows_per_kv, (k // 2) * LANES:(k // 2 + 1) * LANES]
        if k % 2 == 1:
            tl = pltpu.roll(tl, hd, 1)
        tiles.append(tl)
    return jnp.concatenate(tiles, axis=0)


def _attn_sample_kernel(bps, n_kv, tq, past_len, pt_ref, *refs):
    npg = 2 * bps
    kp = refs[0:npg]
    vp = refs[npg:2 * npg]
    qbd_ref, slope_ref, trow_ref, knew_ref, vnew_ref, o_ref, st_scr, km_scr, kn_scr, vn_scr, sel_scr = refs[2 * npg:]
    bs = MOBA_BLOCK
    hd = ATTN_HEAD_DIM
    rows = qbd_ref.shape[1]
    kvw = qbd_ref.shape[2]
    page = kp[0].shape[-1]
    rpk = rows // n_kv
    s_id = pl.program_id(1)
    n_steps = pl.num_programs(1)
    nblk = past_len // bs

    @pl.when(s_id == 0)
    def _():
        km_scr[...] = jnp.zeros(km_scr.shape, F32)

    qf = qbd_ref[0]
    qb = (qf * (hd ** -0.5)).astype(BF16)
    slope = slope_ref[...]
    lane = lax.broadcasted_iota(jnp.int32, (rows, LANES), 1)
    lane_k = lax.broadcasted_iota(jnp.int32, (kvw, LANES), 1)
    sc0 = slope * lane.astype(F32)
    sc1 = sc0 + slope * float(page)

    for bi in range(bps):
        j = s_id * bps + bi
        kt0 = kp[2 * bi][0].reshape(kvw, page)
        kt1 = kp[2 * bi + 1][0].reshape(kvw, page)
        vt0 = vp[2 * bi][0].reshape(kvw, page)
        vt1 = vp[2 * bi + 1][0].reshape(kvw, page)
        ksum = jnp.sum(kt0 + kt1, axis=1, keepdims=True) * (1.0 / bs)
        km_scr[...] = jnp.where(lane_k == j, ksum, km_scr[...])
        s0 = _dot(qb, kt0.astype(BF16)) + sc0
        s1 = _dot(qb, kt1.astype(BF16)) + sc1
        m = jnp.maximum(jnp.max(s0, axis=-1, keepdims=True), jnp.max(s1, axis=-1, keepdims=True))
        p0 = jnp.exp(s0 - m)
        p1 = jnp.exp(s1 - m)
        l = jnp.sum(p0, axis=-1, keepdims=True) + jnp.sum(p1, axis=-1, keepdims=True)
        o = _dot_nt(p0.astype(BF16), vt0.astype(BF16)) + _dot_nt(p1.astype(BF16), vt1.astype(BF16))
        od = _diag_extract(o, n_kv, rpk)
        tile = jnp.where(lane < hd, od, jnp.where(lane == hd, m, l))
        st_scr[j] = tile.T[0:hd + SUBLANES]

    @pl.when(s_id == n_steps - 1)
    def _():
        trow = trow_ref[...]
        gate = _dot_precise(qf, km_scr[...])
        gate = jnp.where(lane < nblk, gate, NEG_INF)
        sel, _ = _topk_mask(gate, lane, MOBA_TOPK)
        selneg = jnp.where(sel, 0.0, NEG_INF)

        kn_scr[...] = jnp.zeros(kn_scr.shape, F32)
        vn_scr[...] = jnp.zeros(vn_scr.shape, F32)
        kn_scr[0:tq, :] = knew_ref[...]
        vn_scr[0:tq, :] = vnew_ref[...]
        lanef = lane.astype(F32)
        so = _dot_nt(qb, kn_scr[...].astype(BF16)) - slope * (trow - lanef)
        so = jnp.where(jnp.logical_and(lanef <= trow, lane < tq), so, NEG_INF)
        m_own = jnp.max(so, axis=-1, keepdims=True)
        po = jnp.exp(so - m_own)
        den_own = jnp.sum(po, axis=-1, keepdims=True)
        num_own = _diag_extract(_dot(po.astype(BF16), vn_scr[...].astype(BF16)), n_kv, rpk)
        own_t = jnp.where(lane < hd, num_own, jnp.where(lane == hd, m_own, den_own)).T

        sel_scr[...] = selneg.T
        slope_row = slope.T[0:1]
        base_row = -slope_row * (float(past_len) + trow.T[0:1])

        def body(j, carry):
            m_run, num, den = carry
            st = st_scr[j]
            mj = st[hd:hd + 1] + (base_row + slope_row * jnp.asarray(j * bs, dtype=F32)) + sel_scr[pl.ds(j, 1), :]
            m_new = jnp.maximum(m_run, mj)
            a = jnp.exp(m_run - m_new)
            b = jnp.exp(mj - m_new)
            return m_new, a * num + b * st[0:hd], a * den + b * st[hd + 1:hd + 2]

        init = (own_t[hd:hd + 1], own_t[0:hd], own_t[hd + 1:hd + 2])
        _, num, den = lax.fori_loop(0, nblk, body, init)
        out_t = jnp.concatenate([num / den, jnp.zeros((LANES - hd, rows), F32)], axis=0)
        o_ref[0] = out_t.T


def _attn_sample(cache_kt, cache_vt, layer, page_table, qbd, slope_rows, t_rows, k_new, v_new, tq, n_kv, past_len):
    nbat, rows, kvw = qbd.shape
    page = cache_kt.shape[4]
    hd = cache_kt.shape[3]
    bs = MOBA_BLOCK
    nblk = past_len // bs
    bps = 4
    while nblk % bps:
        bps //= 2
    n_steps = nblk // bps
    assert bs == 2 * page and nblk <= LANES
    npg = 2 * bps

    def page_spec(r):
        return pl.BlockSpec((None, 1, n_kv, hd, page), lambda b, s, pt: (layer, pt[b, s * npg + r], 0, 0, 0))

    in_specs = ([page_spec(r) for r in range(npg)] + [page_spec(r) for r in range(npg)]
                + [pl.BlockSpec((1, rows, kvw), lambda b, s, pt: (b, 0, 0)),
                   pl.BlockSpec((rows, LANES), lambda b, s, pt: (0, 0)),
                   pl.BlockSpec((rows, LANES), lambda b, s, pt: (0, 0)),
                   pl.BlockSpec((tq, kvw), lambda b, s, pt: (b, 0)),
                   pl.BlockSpec((tq, kvw), lambda b, s, pt: (b, 0))])
    grid_spec = pltpu.PrefetchScalarGridSpec(
        num_scalar_prefetch=1,
        grid=(nbat, n_steps),
        in_specs=in_specs,
        out_specs=pl.BlockSpec((1, rows, LANES), lambda b, s, pt: (b, 0, 0)),
        scratch_shapes=[pltpu.VMEM((nblk, hd + SUBLANES, rows), F32),
                        pltpu.VMEM((kvw, LANES), F32),
                        pltpu.VMEM((LANES, kvw), F32),
                        pltpu.VMEM((LANES, kvw), F32),
                        pltpu.VMEM((LANES, rows), F32)])
    assert rows == LANES
    return pl.pallas_call(
        functools.partial(_attn_sample_kernel, bps, n_kv, tq, past_len),
        out_shape=jax.ShapeDtypeStruct((nbat, rows, LANES), F32),
        grid_spec=grid_spec,
        compiler_params=_cparams(2),
        name="attn_sample",
        cost_estimate=pl.CostEstimate(flops=2 * 2 * nbat * rows * past_len * kvw,
                                      transcendentals=nbat * rows * past_len,
                                      bytes_accessed=2 * 4 * nbat * past_len * kvw),
    )(page_table, *([cache_kt] * npg), *([cache_vt] * npg), qbd, slope_rows, t_rows, k_new, v_new)


def _outproj_kernel(alpha, n_exp, ys_ref, ya_ref, gs_ref, ga_ref, x_ref, gm_ref, scf_ref, shf_ref,
                    wos_ref, woa_ref, wout_ref, g1_ref, b1_ref, wr_ref, br_ref,
                    x1_ref, hf_ref, meta_ref, cnt_ref, cnt_scr):
    tm = x_ref.shape[0]
    merged = (jax.nn.sigmoid(gs_ref[...]) * _dot(ys_ref[...].astype(BF16), wos_ref[...])
              + jax.nn.sigmoid(ga_ref[...]) * _dot(ya_ref[...], woa_ref[...]))
    upd = _dot(merged.astype(BF16), wout_ref[...])
    x1 = _layer_norm(alpha * x_ref[...] + gm_ref[0] * upd, g1_ref[...], b1_ref[...])
    x1_ref[...] = x1
    hf = x1 * scf_ref[0] + shf_ref[0]
    hf_ref[...] = _pack_bf16_pairs(hf)
    logits = _dot_precise(hf, wr_ref[...]) + br_ref[...]
    lane = lax.broadcasted_iota(jnp.int32, logits.shape, 1)
    logits = jnp.where(lane < n_exp, logits, NEG_INF)
    g = logits
    picks, vals, ids = [], [], []
    for _ in range(TOP_K):
        m = jnp.max(g, axis=-1, keepdims=True)
        idx = jnp.min(jnp.where(g == m, lane, LANES), axis=-1, keepdims=True)
        pick = lane == idx
        g = jnp.where(pick, NEG_INF, g)
        picks.append(pick)
        vals.append(m)
        ids.append(idx)
    es = [jnp.exp(v - vals[0]) for v in vals]
    inv = 1.0 / (es[0] + es[1] + es[2] + es[3])

    @pl.when(pl.program_id(0) == 0)
    def _():
        cnt_scr[...] = jnp.zeros(cnt_scr.shape, F32)

    onehot = jnp.zeros(logits.shape, F32)
    for pick in picks:
        onehot = onehot + jnp.where(pick, 1.0, 0.0)
    ri = lax.broadcasted_iota(jnp.int32, (tm, tm), 0)
    ci = lax.broadcasted_iota(jnp.int32, (tm, tm), 1)
    ltri = jnp.where(ri > ci, 1.0, 0.0).astype(BF16)
    rank_mat = _dot(ltri, onehot.astype(BF16)) + cnt_scr[0:1, :]
    cnt_new = cnt_scr[0:1, :] + jnp.sum(onehot, axis=0, keepdims=True)
    cnt_scr[...] = jnp.broadcast_to(cnt_new, cnt_scr.shape)
    cnt_ref[...] = jnp.broadcast_to(cnt_new, cnt_ref.shape)
    meta = jnp.zeros(logits.shape, F32)
    for k in range(TOP_K):
        rank_k = jnp.sum(jnp.where(picks[k], rank_mat, 0.0), axis=-1, keepdims=True)
        meta = jnp.where(lane == k, ids[k].astype(F32), meta)
        meta = jnp.where(lane == TOP_K + k, rank_k, meta)
        meta = jnp.where(lane == 2 * TOP_K + k, es[k] * inv, meta)
    meta_ref[...] = meta


def _outproj(alpha, n_exp, ys, ya, proj, x, gm, scf1, shf, wos, woa, wout, g1, b1, wr, br, tm, tiles_per_group,
             gs_blk, ga_blk):
    n, d = x.shape
    r = gm.shape[1]
    grp = lambda i: (i // tiles_per_group, 0, 0)
    const = lambda i: (0, 0)
    return pl.pallas_call(
        functools.partial(_outproj_kernel, alpha, n_exp),
        out_shape=(jax.ShapeDtypeStruct((n, d), F32),
                   jax.ShapeDtypeStruct((n, d // 2), jnp.uint32),
                   jax.ShapeDtypeStruct((n, LANES), F32),
                   jax.ShapeDtypeStruct((SUBLANES, LANES), F32)),
        grid=(n // tm,),
        in_specs=[pl.BlockSpec((tm, ys.shape[1]), lambda i: (i, 0)),
                  pl.BlockSpec((tm, ya.shape[1]), lambda i: (i, 0)),
                  pl.BlockSpec((tm, d), lambda i: (i, gs_blk)),
                  pl.BlockSpec((tm, d), lambda i: (i, ga_blk)),
                  pl.BlockSpec((tm, d), lambda i: (i, 0)),
                  pl.BlockSpec((1, r, d), grp),
                  pl.BlockSpec((1, r, d), grp),
                  pl.BlockSpec((1, r, d), grp),
                  pl.BlockSpec(wos.shape, const),
                  pl.BlockSpec(woa.shape, const),
                  pl.BlockSpec(wout.shape, const),
                  pl.BlockSpec((1, d), const),
                  pl.BlockSpec((1, d), const),
                  pl.BlockSpec((d, LANES), const),
                  pl.BlockSpec((1, LANES), const)],
        out_specs=(pl.BlockSpec((tm, d), lambda i: (i, 0)),
                   pl.BlockSpec((tm, d // 2), lambda i: (i, 0)),
                   pl.BlockSpec((tm, LANES), lambda i: (i, 0)),
                   pl.BlockSpec((SUBLANES, LANES), const)),
        scratch_shapes=[pltpu.VMEM((SUBLANES, LANES), F32)],
        compiler_params=_cparams(1),
        name="outproj",
        cost_estimate=pl.CostEstimate(flops=2 * n * d * (ys.shape[1] + ya.shape[1] + d + 3 * LANES + tm),
                                      transcendentals=2 * n * d,
                                      bytes_accessed=n * (2 * ys.shape[1] + 2 * ya.shape[1] + 4 * 5 * d)),
    )(ys, ya, proj, proj, x, gm, scf1, shf, wos, woa, wout, g1, b1, wr, br)


def _sc_gather_quantum(row_bytes):
    return SC_CORES * SC_SUBCORES * 2 * (SC_GATHER_BYTES // row_bytes)


def _sc_gather(table, idx):
    _, dcol = table.shape
    b = idx.shape[0]
    nw = SC_CORES * SC_SUBCORES
    ch = SC_GATHER_BYTES // (dcol * table.dtype.itemsize)
    assert b % (nw * 2 * ch) == 0
    b_per_w = b // nw
    n_ch = b_per_w // ch
    mesh = plsc.VectorSubcoreMesh(core_axis_name="c", subcore_axis_name="s")

    @functools.partial(
        pl.kernel, mesh=mesh,
        out_type=jax.ShapeDtypeStruct((b, dcol), table.dtype),
        scratch_types=[pltpu.VMEM((ch,), jnp.int32), pltpu.VMEM((ch,), jnp.int32),
                       pltpu.VMEM((ch, dcol), table.dtype), pltpu.VMEM((ch, dcol), table.dtype),
                       pltpu.SemaphoreType.DMA, pltpu.SemaphoreType.DMA],
        name="sc_gather",
        cost_estimate=pl.CostEstimate(flops=0, transcendentals=0,
                                      bytes_accessed=2 * b * dcol * table.dtype.itemsize + 4 * b),
    )
    def gather(table_hbm, idx_hbm, out_hbm, idx0, idx1, rows0, rows1, sem0, sem1):
        idx_v, rows_v, sems = (idx0, idx1), (rows0, rows1), (sem0, sem1)
        wid = lax.axis_index("s") * SC_CORES + lax.axis_index("c")
        base = wid * b_per_w

        def start(c, slot):
            off = pl.multiple_of(base + c * ch, ch)
            pltpu.sync_copy(idx_hbm.at[pl.ds(off, ch)], idx_v[slot])
            pltpu.async_copy(table_hbm.at[idx_v[slot]], rows_v[slot], sems[slot])

        def finish(c, slot):
            pltpu.make_async_copy(table_hbm.at[idx_v[slot]], rows_v[slot], sems[slot]).wait()
            off = pl.multiple_of(base + c * ch, ch)
            pltpu.sync_copy(rows_v[slot], out_hbm.at[pl.ds(off, ch)])

        start(0, 0)

        @pl.loop(0, n_ch, step=2)
        def _(c):
            start(c + 1, 1)
            finish(c, 0)

            @pl.when(c + 2 < n_ch)
            def _():
                start(c + 2, 0)

            finish(c + 1, 1)

    return gather(table, idx)


def _moe_experts_kernel(d_ff, te_ref, x_ref, wgu_ref, bgu_ref, wdn_ref, bdn_ref, o_ref, wgu_scr, wdn_scr):
    i = pl.program_id(0)
    prev = te_ref[jnp.maximum(i - 1, 0)]
    n_used = te_ref[pl.num_programs(0)]

    @pl.when(i < n_used)
    def _():
        @pl.when(jnp.logical_or(i == 0, te_ref[i] != prev))
        def _():
            wgu_scr[...] = wgu_ref[0].astype(BF16)
            wdn_scr[...] = wdn_ref[0].astype(BF16)

        xa, xb = _unpack_bf16_pairs(x_ref[...])
        half = xa.shape[1]
        gu = (_dot(xa, wgu_scr[0:half, :]) + _dot(xb, wgu_scr[half:2 * half, :])) + bgu_ref[0]
        gg = jnp.minimum(gu[:, :d_ff], SWIGLU_LIMIT)
        uu = jnp.clip(gu[:, d_ff:], -SWIGLU_LIMIT, SWIGLU_LIMIT)
        act = (uu + 1.0) * (gg * jax.nn.sigmoid(SWIGLU_ALPHA * gg))
        o_ref[...] = _dot(act.astype(BF16), wdn_scr[...]) + bdn_ref[0]

    @pl.when(i >= n_used)
    def _():
        o_ref[...] = jnp.zeros(o_ref.shape, F32)


def _moe_experts(xs, tile_expert, layer, w_gu, b_gu4, w_down, b_down4, tme):
    p = xs.shape[0]
    d = w_gu.shape[2]
    two_ff = w_gu.shape[3]
    d_ff = two_ff // 2
    grid_spec = pltpu.PrefetchScalarGridSpec(
        num_scalar_prefetch=1,
        grid=(p // tme,),
        in_specs=[pl.BlockSpec((tme, d // 2), lambda i, te: (i, 0)),
                  pl.BlockSpec((None, 1, d, two_ff), lambda i, te: (layer, te[i], 0, 0)),
                  pl.BlockSpec((None, 1, 1, two_ff), lambda i, te: (layer, te[i], 0, 0)),
                  pl.BlockSpec((None, 1, d_ff, d), lambda i, te: (layer, te[i], 0, 0)),
                  pl.BlockSpec((None, 1, 1, d), lambda i, te: (layer, te[i], 0, 0))],
        out_specs=pl.BlockSpec((tme, d), lambda i, te: (i, 0)),
        scratch_shapes=[pltpu.VMEM((d, two_ff), BF16),
                        pltpu.VMEM((d_ff, d), BF16)])
    return pl.pallas_call(
        functools.partial(_moe_experts_kernel, d_ff),
        out_shape=jax.ShapeDtypeStruct((p, d), F32),
        grid_spec=grid_spec,
        compiler_params=_cparams(1),
        name="moe_experts",
        cost_estimate=pl.CostEstimate(flops=2 * p * d * 3 * d_ff, transcendentals=p * d_ff,
                                      bytes_accessed=6 * p * d + 4 * w_gu.shape[1] * 3 * d * d_ff),
    )(tile_expert, xs, w_gu, b_gu4, w_down, b_down4)


def _moe_combine_kernel(alpha, *refs):
    yk_refs = refs[:TOP_K]
    meta_ref, x1_ref, gf_ref, g2_ref, b2_ref, o_ref = refs[TOP_K:]
    meta = meta_ref[...]
    acc = meta[:, 2 * TOP_K:2 * TOP_K + 1] * yk_refs[0][...]
    for k in range(1, TOP_K):
        acc = acc + meta[:, 2 * TOP_K + k:2 * TOP_K + k + 1] * yk_refs[k][...]
    o_ref[...] = _layer_norm(alpha * x1_ref[...] + gf_ref[0] * acc, g2_ref[...], b2_ref[...])


def _moe_combine(alpha, yk, n_pad, meta, x1, gf, g2, b2, tm, tiles_per_group):
    n, d = x1.shape
    r = gf.shape[1]
    bpk = n_pad // tm
    assert n_pad % tm == 0

    def choice_spec(k):
        return pl.BlockSpec((tm, d), lambda i: (k * bpk + i, 0))

    return pl.pallas_call(
        functools.partial(_moe_combine_kernel, alpha),
        out_shape=jax.ShapeDtypeStruct((n, d), F32),
        grid=(n // tm,),
        in_specs=[choice_spec(k) for k in range(TOP_K)] + [
                  pl.BlockSpec((tm, LANES), lambda i: (i, 0)),
                  pl.BlockSpec((tm, d), lambda i: (i, 0)),
                  pl.BlockSpec((1, r, d), lambda i: (i // tiles_per_group, 0, 0)),
                  pl.BlockSpec((1, d), lambda i: (0, 0)),
                  pl.BlockSpec((1, d), lambda i: (0, 0))],
        out_specs=pl.BlockSpec((tm, d), lambda i: (i, 0)),
        compiler_params=_cparams(1),
        name="moe_combine",
        cost_estimate=pl.CostEstimate(flops=2 * TOP_K * n * d, transcendentals=n,
                                      bytes_accessed=4 * n * d * (TOP_K + 2)),
    )(*([yk] * TOP_K), meta, x1, gf, g2, b2)


def _moe_route(meta, counts, tile, tm, d):
    n = meta.shape[0]
    n_exp = counts.shape[0]
    gran_in = _sc_gather_quantum(2 * d)
    gran = _sc_gather_quantum(4 * d)
    unit = gran_in * tile // math.gcd(gran_in, tile)
    eid = meta[:, 0:TOP_K].astype(jnp.int32)
    rank = meta[:, TOP_K:2 * TOP_K].astype(jnp.int32)
    padded = -(-counts // tile) * tile
    ends = jnp.cumsum(padded)
    is_e = eid[:, :, None] == jnp.arange(n_exp, dtype=jnp.int32)
    dest = jnp.sum(jnp.where(is_e, (ends - padded)[None, None, :], 0), axis=-1) + rank
    p_rows = -(-(n * TOP_K + n_exp * tile) // unit) * unit
    tok = jnp.broadcast_to(jnp.arange(n, dtype=jnp.int32)[:, None], (n, TOP_K))
    src_tok = jnp.zeros((p_rows,), jnp.int32).at[dest.reshape(-1)].set(
        tok.reshape(-1), unique_indices=True, mode="promise_in_bounds")
    tile_start = jnp.arange(p_rows // tile, dtype=jnp.int32) * tile
    tile_expert = jnp.minimum(jnp.sum(tile_start[:, None] >= ends[None, :], axis=-1), n_exp - 1).astype(jnp.int32)
    tile_expert = jnp.concatenate([tile_expert, (ends[-1:] // tile).astype(jnp.int32)])
    quant = max(gran // TOP_K, tm)
    n_pad = -(-n // quant) * quant
    dest_km = jnp.pad(dest.T, ((0, 0), (0, n_pad - n))).reshape(-1)
    return dict(src_tok=src_tok, tile_expert=tile_expert, dest_km=dest_km, n_pad=n_pad, tile=tile)


def _moe_group(alpha, layer, route, hf, meta, x1, gf, g2, b2, w_gu, b_gu4, w_down, b_down4, tm, tiles_per_group):
    xs_sorted = _sc_gather(hf, route["src_tok"])
    ys_sorted = _moe_experts(xs_sorted, route["tile_expert"], layer, w_gu, b_gu4, w_down, b_down4, route["tile"])
    yk = _sc_gather(ys_sorted, route["dest_km"])
    return yk, functools.partial(_moe_combine, alpha, n_pad=route["n_pad"], meta=meta, x1=x1, gf=gf, g2=g2, b2=b2,
                                 tm=tm, tiles_per_group=tiles_per_group)


def _pad_lanes(v, fill=0.0):
    return jnp.pad(v, (0, LANES - v.shape[0]), constant_values=fill).reshape(1, LANES)


def kernel(x_prompt, x_sample, c_prompt, c_sample, cache_k, cache_v, state_ssm, state_conv, page_table, w_ada, b_ada, w_in, conv_w, conv_b, dt_bias, a_log, d_skip, ssm_norm_g, w_o_ssm, w_o_attn, w_out, ln1_g, ln1_b, ln2_g, ln2_b, w_router, b_router, w_gu, b_gu, w_down, b_down):
    depth, d, _ = w_ada.shape
    bp, tp, _ = x_prompt.shape
    bsm, ts, _ = x_sample.shape
    n_heads = dt_bias.shape[1]
    d_inner = n_heads * SSM_HEAD_DIM
    conv_ch = conv_w.shape[2]
    kvw = cache_k.shape[3] * cache_k.shape[4]
    n_kv = cache_k.shape[3]
    aw = w_o_attn.shape[1]
    n_q = aw // ATTN_HEAD_DIM
    n_exp = w_router.shape[2]
    page = cache_k.shape[2]
    past_len = page_table.shape[1] * page
    alpha = float((2.0 * depth) ** 0.25)
    assert d_inner == 2 * d and aw == d and kvw == d // 2 and n_q == 2 * n_kv
    assert conv_ch == d_inner + 2 * SSM_GROUPS * SSM_STATE and conv_ch == 3 * d
    assert tp % MOBA_BLOCK == 0 and tp % SSD_CHUNK == 0 and ts == SUBLANES and n_heads <= LANES

    o_z, o_xbc, o_dt = 0, d_inner, d_inner + conv_ch
    o_q = o_dt + n_heads
    o_k, o_v = o_q + aw, o_q + aw + kvw
    o_gs, o_ga = o_v + kvw, o_v + kvw + d
    c_z, c_q, c_gs, c_ga = 0, d_inner, d_inner + d, d_inner + 2 * d
    c_k = d_inner + 3 * d
    c_v = c_k + kvw
    c_xbc = c_v + kvw
    assert c_xbc % conv_ch == 0

    slopes = 2.0 ** (-8.0 * (jnp.arange(n_q, dtype=F32) + 1.0) / n_q)

    n_c = bsm + bp
    n_c_pad = -(-n_c // SUBLANES) * SUBLANES
    c_all = jnp.concatenate([c_sample, c_prompt, jnp.zeros((n_c_pad - n_c, d), F32)], axis=0)
    mod = _ada(c_all, w_ada, b_ada)

    grp = n_q // n_kv
    rows = n_kv * grp * ts
    r_idx = jnp.arange(rows)
    r_head = (r_idx // (grp * ts)) * grp + (r_idx // ts) % grp
    slope_rows = jnp.broadcast_to(slopes[r_head][:, None], (rows, LANES))
    t_rows = jnp.broadcast_to((r_idx % ts).astype(F32)[:, None], (rows, LANES))
    eye_kv = jnp.eye(n_kv, dtype=F32)

    cache_kt = jnp.transpose(cache_k, (0, 1, 3, 4, 2))
    cache_vt = jnp.transpose(cache_v, (0, 1, 3, 4, 2))

    xp = x_prompt.reshape(bp * tp, d)
    xs = x_sample.reshape(bsm * ts, d)
    tm_p = 1024 if tp % 1024 == 0 else MOBA_BLOCK
    tm_o = 512 if tp % 512 == 0 else MOBA_BLOCK
    n_s = bsm * ts

    def make_layer(l):
        wl = w_in[l]
        w_main = jnp.concatenate([wl[:, o_z:o_z + d_inner], wl[:, o_q:o_q + aw], wl[:, o_gs:o_gs + d],
                                  wl[:, o_ga:o_ga + d], wl[:, o_k:o_k + kvw], wl[:, o_v:o_v + kvw],
                                  wl[:, o_xbc:o_xbc + conv_ch]], axis=1).astype(BF16)
        w_dt = jnp.pad(wl[:, o_dt:o_dt + n_heads], ((0, 0), (0, LANES - n_heads))).astype(BF16)
        w_dtT = wl[:, o_dt:o_dt + n_heads].T.astype(BF16)
        dtb = _pad_lanes(dt_bias[l])
        dtbT = jnp.broadcast_to(dt_bias[l][:, None], (n_heads, SSD_CHUNK))
        a_neg = -jnp.exp(a_log[l])
        a_row = _pad_lanes(a_neg)
        aT = jnp.broadcast_to(a_neg[:, None], (n_heads, SSD_CHUNK))
        dskip = jnp.repeat(d_skip[l], SSM_HEAD_DIM).reshape(1, d_inner)
        ng = ssm_norm_g[l].reshape(1, d_inner)
        cw = conv_w[l]
        cb = conv_b[l].reshape(1, conv_ch)
        wos = w_o_ssm[l].astype(BF16)
        woa = w_o_attn[l].astype(BF16)
        wout = w_out[l].astype(BF16)
        g1, b1 = ln1_g[l].reshape(1, d), ln1_b[l].reshape(1, d)
        g2, b2 = ln2_g[l].reshape(1, d), ln2_b[l].reshape(1, d)
        wr = jnp.pad(w_router[l], ((0, 0), (0, LANES - n_exp)))
        br = _pad_lanes(b_router[l])

        mod_l = mod[l]
        sh_m, sc_m, g_m, sh_f, sc_f, g_f = [mod_l[:, i * d:(i + 1) * d] for i in range(6)]

        def group_mod(v, is_prompt):
            if is_prompt:
                return v[bsm:bsm + bp].reshape(bp, 1, d)
            return jnp.repeat(v[:bsm], ts, axis=0).reshape(1, n_s, d)

        def run(x, is_prompt, h0T, cprev8):
            bsz, t = (bp, tp) if is_prompt else (bsm, ts)
            n = bsz * t
            tm = tm_p if is_prompt else n
            tmo = tm_o if is_prompt else n
            tpg = (t // tm) if is_prompt else 1
            tpgo = (t // tmo) if is_prompt else 1
            gm = lambda v: group_mod(v, is_prompt)
            proj, dt, dtT, k_new, v_new = _inproj(x, gm(1.0 + sc_m), gm(sh_m), w_main, w_dt, w_dtT, tm, tpg,
                                                  c_k, kvw)
            dtT_seq = dtT.reshape(n_heads, bsz, t).transpose(1, 0, 2)
            lr = SSD_CHUNK if is_prompt else t
            y_s, h_newT = _ssd(proj, dt, dtT_seq, h0T, cprev8, cw, cb, dtb, dtbT, a_row, aT, dskip, ng,
                               bsz, t, lr, c_z // d_inner, c_xbc // conv_ch, d_inner, conv_ch)
            if is_prompt:
                y_a = _attn_prompt(proj, slopes, bsz, t, n_kv, c_q, c_k, c_v)
            else:
                q5 = proj[:, c_q:c_q + aw].reshape(bsz, t, n_kv, grp, ATTN_HEAD_DIM)
                q5 = q5.transpose(0, 2, 3, 1, 4).reshape(bsz, n_kv, grp * t, ATTN_HEAD_DIM)
                qbd = (q5[:, :, :, None, :] * eye_kv[None, :, None, :, None]).reshape(bsz, rows, kvw)
                o = _attn_sample(cache_kt, cache_vt, l, page_table, qbd, slope_rows, t_rows, k_new, v_new,
                                 t, n_kv, past_len)
                o = o[:, :, :ATTN_HEAD_DIM].reshape(bsz, n_kv, grp, t, ATTN_HEAD_DIM)
                y_a = o.transpose(0, 3, 1, 2, 4).reshape(n, aw).astype(BF16)
            x1, hf, meta, cnt = _outproj(alpha, n_exp, y_s, y_a, proj, x, gm(g_m), gm(1.0 + sc_f), gm(sh_f),
                                         wos, woa, wout, g1, b1, wr, br, tmo, tpgo, c_gs // d, c_ga // d)
            conv_new = proj.reshape(bsz, t, -1)[:, t - (CONV_W - 1):, c_xbc:c_xbc + conv_ch]
            h_new = h_newT.transpose(0, 2, 1).reshape(bsz, n_heads, SSM_HEAD_DIM, SSM_STATE)
            hd4 = (bsz, t, n_kv, ATTN_HEAD_DIM)
            return dict(x1=x1, hf=hf, meta=meta, cnt=cnt[0, :n_exp].astype(jnp.int32), gf=gm(g_f),
                        k=k_new.reshape(hd4), v=v_new.reshape(hd4), h=h_new, conv=conv_new)

        return run, g2, b2

    def sample_front(l, x):
        h0_s = state_ssm[l].reshape(bsm, d_inner, SSM_STATE).transpose(0, 2, 1)
        cv0_s = jnp.pad(state_conv[l], ((0, 0), (SUBLANES - (CONV_W - 1), 0), (0, 0)))
        return layers[l][0](x, False, h0_s, cv0_s)

    layers = [make_layer(l) for l in range(depth)]
    b_gu4 = b_gu.reshape(depth, n_exp, 1, -1)
    b_down4 = b_down.reshape(depth, n_exp, 1, d)
    h0_p = jnp.zeros((bp, SSM_STATE, d_inner), F32)
    cv0_p = jnp.zeros((bp, SUBLANES, conv_ch), F32)
    outs = {k: [] for k in ("kp", "vp", "hp", "cvp", "ks", "vs", "hs", "cvs")}
    fs = None
    for l in range(depth):
        run, g2, b2 = layers[l]
        fp = run(xp, True, h0_p, cv0_p)
        route_p = _moe_route(fp["meta"], fp["cnt"], MOE_TILE, MOBA_BLOCK, d)
        fs = sample_front(l, xs)
        outs["kp"].append(fp["k"]); outs["vp"].append(fp["v"]); outs["hp"].append(fp["h"]); outs["cvp"].append(fp["conv"])
        outs["ks"].append(fs["k"]); outs["vs"].append(fs["v"]); outs["hs"].append(fs["h"]); outs["cvs"].append(fs["conv"])

        yk_p, combine_p = _moe_group(alpha, l, route_p, fp["hf"], fp["meta"], fp["x1"], fp["gf"], g2, b2,
                                     w_gu, b_gu4, w_down, b_down4, MOBA_BLOCK, tp // MOBA_BLOCK)
        route_s = _moe_route(fs["meta"], fs["cnt"], MOE_TILE_SMALL, n_s, d)
        yk_s, combine_s = _moe_group(alpha, l, route_s, fs["hf"], fs["meta"], fs["x1"], fs["gf"], g2, b2,
                                     w_gu, b_gu4, w_down, b_down4, n_s, 1)
        xs = combine_s(yk_s)
        xp = combine_p(yk_p)

    st = lambda k: jnp.stack(outs[k])
    return (xp.reshape(bp, tp, d), xs.reshape(bsm, ts, d), st("kp"), st("vp"), st("hp"), st("cvp"),
            st("ks"), st("vs"), st("hs"), st("cvs"))
```
